```python
import math
import jax
import jax.numpy as jnp
from jax import lax
import numpy as np

D_MODEL = 1024
BATCH = 2
SEQ = 8192
DEPTH = 2
DEC_BATCH = 32
DEC_SEQ = 4
PAST_LEN = 16384
PAGE_SIZE = 128

N_MIXERS = 2
N_SSM_LAYERS = (DEPTH + 1) // 2
N_ATT_LAYERS = DEPTH // 2

SSM_EXPAND = 2
D_INNER = SSM_EXPAND * D_MODEL
SSM_HEAD_DIM = 64
SSM_HEADS = D_INNER // SSM_HEAD_DIM
SSM_GROUPS = 8
SSM_STATE = 128
CONV_WIDTH = 4
CONV_DIM = D_INNER + 2 * SSM_GROUPS * SSM_STATE
SSM_IN_DIM = D_INNER + CONV_DIM + SSM_HEADS
SSD_CHUNK = 128

ATT_HEADS = 16
ATT_HEAD_DIM = D_MODEL // ATT_HEADS
KV_HEADS = 4
KV_DIM = KV_HEADS * ATT_HEAD_DIM
CMP_STRIDE = 16
CMP_BLOCK = 2 * CMP_STRIDE
CMP_HIDDEN = 128
SLC_BLOCK = 64
SLC_TOP_N = 16
WINDOW = 512
N_BRANCH = 3
ATT_IN_DIM = D_MODEL + 6 * KV_DIM + N_BRANCH * ATT_HEADS
Q_BLOCK = 128

D_FF = 4 * D_MODEL
NORM_EPS = 1e-5
NEG_INF = -1e30

kernel_name = 'hybrid_ssd_nsa_decode_step'


def rms_norm(x, g):
    xf = x.astype(jnp.float32)
    y = xf * lax.rsqrt(jnp.mean(xf * xf, axis=-1, keepdims=True) + NORM_EPS) * g.astype(jnp.float32)
    return y.astype(x.dtype)


def sq_relu_mlp(h, w_up, w_down):
    return jnp.square(jax.nn.relu(h @ w_up)) @ w_down


def alibi_slopes():
    return 2.0 ** (-8.0 * jnp.arange(1, ATT_HEADS + 1, dtype=jnp.float32) / ATT_HEADS)


def pad_rows(x, length):
    return jnp.pad(x, ((0, 0), (0, length - x.shape[1])) + ((0, 0),) * (x.ndim - 2))


def ssd_chunked(x, dt, a, bm, cm, init_state):
    b, t, nh, p = x.shape
    g, n = bm.shape[2], bm.shape[3]
    r = nh // g
    l = min(SSD_CHUNK, t)
    n_c = -(-t // l)
    tp = n_c * l
    x, dt, bm, cm = pad_rows(x, tp), pad_rows(dt, tp), pad_rows(bm, tp), pad_rows(cm, tp)
    xc = x.reshape(b, n_c, l, g, r, p)
    dtc = dt.reshape(b, n_c, l, g, r)
    bc = bm.reshape(b, n_c, l, g, n)
    cc = cm.reshape(b, n_c, l, g, n)
    acs = jnp.cumsum(dtc * a.reshape(g, r), axis=2)
    xdt = xc * dtc[..., None]
    seg = acs[:, :, :, None] - acs[:, :, None, :]
    causal = jnp.tril(jnp.ones((l, l), dtype=bool))[None, None, :, :, None, None]
    lmat = jnp.exp(jnp.where(causal, seg, -jnp.inf))
    cb = jnp.einsum('bclgn,bcsgn->bclsg', cc, bc)
    y_diag = jnp.einsum('bclsg,bclsgr,bcsgrp->bclgrp', cb, lmat, xdt)
    decay_in = jnp.exp(acs[:, :, -1:] - acs)
    chunk_states = jnp.einsum('bclgn,bclgr,bclgrp->bcgrpn', bc, decay_in, xdt)
    chunk_decay = jnp.exp(acs[:, :, -1])

    def step(state, inp):
        st, dec = inp
        return dec[..., None, None] * state + st, state

    final, prev = lax.scan(step, init_state.reshape(b, g, r, p, n),
                           (jnp.moveaxis(chunk_states, 1, 0), jnp.moveaxis(chunk_decay, 1, 0)))
    prev = jnp.moveaxis(prev, 0, 1)
    y_off = jnp.einsum('bclgn,bcgrpn,bclgr->bclgrp', cc, prev, jnp.exp(acs))
    y = (y_diag + y_off).reshape(b, tp, nh, p)[:, :t]
    return y, final.reshape(b, nh, p, n)


def mamba_mixer(h, conv_state, ssm_state, w_in, conv_w, conv_b, dt_bias, a_log, d_skip, gate_norm, w_out):
    b, t, _ = h.shape
    f32 = jnp.float32
    z, xbc, dt = jnp.split(h @ w_in, [D_INNER, D_INNER + CONV_DIM], axis=-1)
    xpad = jnp.concatenate([conv_state.astype(xbc.dtype), xbc], axis=1)
    conv = conv_b + sum(xpad[:, k:k + t] * conv_w[:, k] for k in range(CONV_WIDTH))
    xbc = jax.nn.silu(conv.astype(f32))
    xs, bm, cm = jnp.split(xbc, [D_INNER, D_INNER + SSM_GROUPS * SSM_STATE], axis=-1)
    xs = xs.reshape(b, t, SSM_HEADS, SSM_HEAD_DIM)
    bm = bm.reshape(b, t, SSM_GROUPS, SSM_STATE)
    cm = cm.reshape(b, t, SSM_GROUPS, SSM_STATE)
    dt = jax.nn.softplus(dt.astype(f32) + dt_bias.astype(f32))
    a = -jnp.exp(a_log.astype(f32))
    y, new_ssm = ssd_chunked(xs, dt, a, bm, cm, ssm_state.astype(f32))
    y = y + d_skip.astype(f32)[:, None] * xs
    y = y.reshape(b, t, D_INNER) * jax.nn.silu(z.astype(f32))
    yg = y.reshape(b, t, SSM_GROUPS, D_INNER // SSM_GROUPS)
    yg = yg * lax.rsqrt(jnp.mean(yg * yg, axis=-1, keepdims=True) + NORM_EPS)
    y = yg.reshape(b, t, D_INNER) * gate_norm.astype(f32)
    return y.astype(h.dtype) @ w_out, xpad[:, -(CONV_WIDTH - 1):], new_ssm


def nsa_project(h, w_in):
    b, t, _ = h.shape
    splits = [D_MODEL + i * KV_DIM for i in range(7)]
    q, kc, vc, ks, vs, kw, vw, gl = jnp.split(h @ w_in, splits, axis=-1)
    kvr = lambda u: u.reshape(b, t, KV_HEADS, ATT_HEAD_DIM)
    return (q.reshape(b, t, ATT_HEADS, ATT_HEAD_DIM), kvr(kc), kvr(vc), kvr(ks), kvr(vs), kvr(kw), kvr(vw),
            gl.reshape(b, t, ATT_HEADS, N_BRANCH))


def compress_kv(kv, pos_emb, w1, w2):
    b, l, g, d = kv.shape
    sub = kv.reshape(b, l // CMP_STRIDE, CMP_STRIDE, g, d)
    w1r = w1.reshape(CMP_BLOCK, d, CMP_HIDDEN)
    u_first = jnp.einsum('bsjgd,jdh->bsgh', sub, w1r[:CMP_STRIDE])
    u_second = jnp.einsum('bsjgd,jdh->bsgh', sub, w1r[CMP_STRIDE:])
    pos_term = jnp.einsum('jd,jdh->h', pos_emb, w1r)
    hid = jax.nn.gelu(u_first[:, :-1] + u_second[:, 1:] + pos_term)
    return hid @ w2


def nsa_attend(q, gate_logits, q_pos, kc, vc, c_end, ks_blk, vs_blk, kw, vw, kw_pos, slopes):
    f32 = jnp.float32
    b, tq = q.shape[:2]
    r = ATT_HEADS // KV_HEADS
    qg = q.astype(f32).reshape(b, tq, KV_HEADS, r, ATT_HEAD_DIM) * (ATT_HEAD_DIM ** -0.5)
    sl = slopes.reshape(KV_HEADS, r)
    tpos = q_pos.astype(f32)

    d_c = tpos[:, None] - c_end.astype(f32)[None, :]
    ok_c = d_c >= 0
    s_c = jnp.einsum('btgrd,bcgd->btgrc', qg, kc.astype(f32)) - sl[:, :, None] * d_c[:, None, None, :]
    s_c = jnp.where(ok_c[:, None, None, :], s_c, NEG_INF)
    p_c = jax.nn.softmax(s_c, axis=-1) * jnp.any(ok_c, axis=-1)[:, None, None, None].astype(f32)
    o_c = jnp.einsum('btgrc,bcgd->btgrd', p_c, vc.astype(f32))

    n_sel = ks_blk.shape[1]
    imp = p_c.sum(axis=3)
    imp_sub = jnp.pad(imp, ((0, 0), (0, 0), (0, 0), (0, 1))) + jnp.pad(imp, ((0, 0), (0, 0), (0, 0), (1, 0)))
    p_slc = imp_sub.reshape(b, tq, KV_HEADS, n_sel, SLC_BLOCK // CMP_STRIDE).sum(axis=-1)
    blk = jnp.arange(n_sel)
    cur = (q_pos // SLC_BLOCK)[:, None]
    forced = (blk == 0) | (blk == cur) | (blk == cur - 1)
    reachable = blk * SLC_BLOCK <= q_pos[:, None]
    score = jnp.where(forced[:, None, :], jnp.inf, jnp.where(reachable[:, None, :], p_slc, -jnp.inf))
    _, idx = lax.top_k(score, min(SLC_TOP_N, n_sel))
    bi = jnp.arange(b)[:, None, None, None]
    gi = jnp.arange(KV_HEADS)[None, None, :, None]
    k_sel = ks_blk[bi, idx, :, gi]
    v_sel = vs_blk[bi, idx, :, gi]
    kpos = idx[..., None] * SLC_BLOCK + jnp.arange(SLC_BLOCK)
    d_s = tpos[None, :, None, None, None] - kpos.astype(f32)
    s_s = jnp.einsum('btgrd,btgnkd->btgrnk', qg, k_sel.astype(f32)) - sl[:, :, None, None] * d_s[:, :, :, None]
    s_s = jnp.where(d_s[:, :, :, None] >= 0, s_s, NEG_INF)
    p_s = jax.nn.softmax(s_s.reshape(b, tq, KV_HEADS, r, -1), axis=-1).reshape(s_s.shape)
    o_s = jnp.einsum('btgrnk,btgnkd->btgrd', p_s, v_sel.astype(f32))

    d_w = tpos[:, None] - kw_pos.astype(f32)[None, :]
    ok_w = (d_w >= 0) & (d_w < WINDOW) & (kw_pos >= 0)[None, :]
    s_w = jnp.einsum('btgrd,bkgd->btgrk', qg, kw.astype(f32)) - sl[:, :, None] * d_w[:, None, None, :]
    s_w = jnp.where(ok_w[:, None, None, :], s_w, NEG_INF)
    o_w = jnp.einsum('btgrk,bkgd->btgrd', jax.nn.softmax(s_w, axis=-1), vw.astype(f32))

    gates = jax.nn.sigmoid(gate_logits.astype(f32)).reshape(b, tq, KV_HEADS, r, N_BRANCH)
    o = gates[..., 0:1] * o_c + gates[..., 1:2] * o_s + gates[..., 2:3] * o_w
    return o.reshape(b, tq, ATT_HEADS * ATT_HEAD_DIM)


def nsa_prompt(h, slopes, w_in, cmp_pos, cmp_w1, cmp_w2, w_out):
    b, t, _ = h.shape
    q, kc, vc, ks, vs, kw, vw, gl = nsa_project(h, w_in)
    l_pad = -(-t // SLC_BLOCK) * SLC_BLOCK
    kcc = compress_kv(pad_rows(kc, l_pad), cmp_pos[0], cmp_w1[0], cmp_w2[0])
    vcc = compress_kv(pad_rows(vc, l_pad), cmp_pos[1], cmp_w1[1], cmp_w2[1])
    c_end = jnp.arange(kcc.shape[1]) * CMP_STRIDE + (CMP_BLOCK - 1)
    ks_blk = pad_rows(ks, l_pad).reshape(b, l_pad // SLC_BLOCK, SLC_BLOCK, KV_HEADS, ATT_HEAD_DIM)
    vs_blk = pad_rows(vs, l_pad).reshape(b, l_pad // SLC_BLOCK, SLC_BLOCK, KV_HEADS, ATT_HEAD_DIM)
    kw_pad = jnp.pad(kw, ((0, 0), (WINDOW, 0), (0, 0), (0, 0)))
    vw_pad = jnp.pad(vw, ((0, 0), (WINDOW, 0), (0, 0), (0, 0)))

    def q_block(i):
        qs = i * Q_BLOCK
        qb = lax.dynamic_slice_in_dim(q, qs, Q_BLOCK, axis=1)
        gb = lax.dynamic_slice_in_dim(gl, qs, Q_BLOCK, axis=1)
        kwb = lax.dynamic_slice_in_dim(kw_pad, qs, Q_BLOCK + WINDOW, axis=1)
        vwb = lax.dynamic_slice_in_dim(vw_pad, qs, Q_BLOCK + WINDOW, axis=1)
        q_pos = qs + jnp.arange(Q_BLOCK)
        kw_pos = qs - WINDOW + jnp.arange(Q_BLOCK + WINDOW)
        return nsa_attend(qb, gb, q_pos, kcc, vcc, c_end, ks_blk, vs_blk, kwb, vwb, kw_pos, slopes)

    o = lax.map(q_block, jnp.arange(t // Q_BLOCK))
    o = jnp.moveaxis(o, 0, 1).reshape(b, t, ATT_HEADS * ATT_HEAD_DIM)
    win_rows = min(WINDOW, t)
    return (o.astype(h.dtype) @ w_out, jnp.stack([kc, vc], axis=2), jnp.stack([ks, vs], axis=2),
            jnp.stack([kw, vw], axis=2)[:, t - win_rows:])


def nsa_sample(h, cache_cmp, cache_slc, win_buf, page_table, slopes, w_in, cmp_pos, cmp_w1, cmp_w2, w_out):
    b, t, _ = h.shape
    past_len = page_table.shape[1] * PAGE_SIZE
    q, kc, vc, ks, vs, kw, vw, gl = nsa_project(h, w_in)
    new_cmp = jnp.stack([kc, vc], axis=2)
    new_slc = jnp.stack([ks, vs], axis=2)
    new_win = jnp.stack([kw, vw], axis=2)
    l_all = past_len + t
    l_pad = -(-l_all // SLC_BLOCK) * SLC_BLOCK

    def assemble(pool, new):
        past = pool[page_table].reshape(b, past_len, 2, KV_HEADS, ATT_HEAD_DIM)
        zeros = jnp.zeros((b, l_pad - l_all) + new.shape[2:], new.dtype)
        return jnp.concatenate([past.astype(new.dtype), new, zeros], axis=1)

    cmp_all = assemble(cache_cmp, new_cmp)
    slc_all = assemble(cache_slc, new_slc)
    kcc = compress_kv(cmp_all[:, :, 0], cmp_pos[0], cmp_w1[0], cmp_w2[0])
    vcc = compress_kv(cmp_all[:, :, 1], cmp_pos[1], cmp_w1[1], cmp_w2[1])
    c_end = jnp.arange(kcc.shape[1]) * CMP_STRIDE + (CMP_BLOCK - 1)
    ks_blk = slc_all[:, :, 0].reshape(b, l_pad // SLC_BLOCK, SLC_BLOCK, KV_HEADS, ATT_HEAD_DIM)
    vs_blk = slc_all[:, :, 1].reshape(b, l_pad // SLC_BLOCK, SLC_BLOCK, KV_HEADS, ATT_HEAD_DIM)
    w_rows = win_buf.shape[1]
    win_all = jnp.concatenate([win_buf.astype(new_win.dtype), new_win], axis=1)
    kw_pos = past_len - w_rows + jnp.arange(w_rows + t)
    q_pos = past_len + jnp.arange(t)
    o = nsa_attend(q, gl, q_pos, kcc, vcc, c_end, ks_blk, vs_blk, win_all[:, :, 0], win_all[:, :, 1], kw_pos, slopes)
    return o.astype(h.dtype) @ w_out, new_cmp, new_slc, win_all[:, -w_rows:]


def setup_inputs(seed: int = 0) -> dict:
    key = jax.random.key(seed)
    k = jax.random.split(key, 32)
    f32 = jnp.float32

    def nrm(kk, shape, scale=1.0):
        return jax.random.normal(kk, shape, f32) * scale

    n_pages = PAST_LEN // PAGE_SIZE
    n_used = DEC_BATCH * n_pages
    n_phys = n_used + n_used // 4
    win = min(WINDOW, PAST_LEN)
    ls, la = N_SSM_LAYERS, N_ATT_LAYERS
    page_table = jax.random.permutation(k[7], n_phys)[:n_used].reshape(DEC_BATCH, n_pages).astype(jnp.int32)
    dt0 = jnp.exp(jax.random.uniform(k[8], (ls, SSM_HEADS), f32, math.log(1e-3), math.log(1e-1)))
    return {
        'x_prompt': nrm(k[0], (BATCH, SEQ, D_MODEL)),
        'x_sample': nrm(k[1], (DEC_BATCH, DEC_SEQ, D_MODEL)),
        'state_ssm': nrm(k[2], (ls, DEC_BATCH, SSM_HEADS, SSM_HEAD_DIM, SSM_STATE), 0.1),
        'state_conv': nrm(k[3], (ls, DEC_BATCH, CONV_WIDTH - 1, CONV_DIM)),
        'cache_cmp_kv': nrm(k[4], (la, n_phys, PAGE_SIZE, 2, KV_HEADS, ATT_HEAD_DIM)),
        'cache_slc_kv': nrm(k[5], (la, n_phys, PAGE_SIZE, 2, KV_HEADS, ATT_HEAD_DIM)),
        'state_win_kv': nrm(k[6], (la, DEC_BATCH, win, 2, KV_HEADS, ATT_HEAD_DIM)),
        'page_table': page_table,
        'ssm_norm': 1.0 + nrm(k[9], (ls, D_MODEL), 0.02),
        'ssm_w_in': nrm(k[10], (ls, D_MODEL, SSM_IN_DIM), D_MODEL ** -0.5),
        'ssm_conv_w': jax.random.uniform(k[11], (ls, CONV_DIM, CONV_WIDTH), f32, -0.5, 0.5),
        'ssm_conv_b': nrm(k[12], (ls, CONV_DIM), 0.02),
        'ssm_dt_bias': dt0 + jnp.log(-jnp.expm1(-dt0)),
        'ssm_a_log': jnp.log(jax.random.uniform(k[13], (ls, SSM_HEADS), f32, 1.0, 16.0)),
        'ssm_d': 1.0 + nrm(k[14], (ls, SSM_HEADS), 0.1),
        'ssm_gate_norm': 1.0 + nrm(k[15], (ls, D_INNER), 0.02),
        'ssm_w_out': nrm(k[16], (ls, D_INNER, D_MODEL), D_INNER ** -0.5),
        'att_norm': 1.0 + nrm(k[17], (la, D_MODEL), 0.02),
        'att_w_in': nrm(k[18], (la, D_MODEL, ATT_IN_DIM), D_MODEL ** -0.5),
        'att_cmp_pos': nrm(k[19], (la, 2, CMP_BLOCK, ATT_HEAD_DIM), 0.1),
        'att_cmp_w1': nrm(k[20], (la, 2, CMP_BLOCK * ATT_HEAD_DIM, CMP_HIDDEN), (CMP_BLOCK * ATT_HEAD_DIM) ** -0.5),
        'att_cmp_w2': nrm(k[21], (la, 2, CMP_HIDDEN, ATT_HEAD_DIM), CMP_HIDDEN ** -0.5),
        'att_w_out': nrm(k[22], (la, D_MODEL, D_MODEL), D_MODEL ** -0.5),
        'mlp_norm': 1.0 + nrm(k[23], (DEPTH, D_MODEL), 0.02),
        'mlp_w_up': nrm(k[24], (DEPTH, D_MODEL, D_FF), D_MODEL ** -0.5),
        'mlp_w_down': nrm(k[25], (DEPTH, D_FF, D_MODEL), D_FF ** -0.5),
        'final_norm': 1.0 + nrm(k[26], (D_MODEL,), 0.02),
    }


def reference(x_prompt, x_sample, state_ssm, state_conv, cache_cmp_kv, cache_slc_kv, state_win_kv, page_table,
              ssm_norm, ssm_w_in, ssm_conv_w, ssm_conv_b, ssm_dt_bias, ssm_a_log, ssm_d, ssm_gate_norm, ssm_w_out,
              att_norm, att_w_in, att_cmp_pos, att_cmp_w1, att_cmp_w2, att_w_out,
              mlp_norm, mlp_w_up, mlp_w_down, final_norm):
    slopes = alibi_slopes()
    hp, hs = x_prompt, x_sample
    ssm_p, conv_p, ssm_s, conv_s = [], [], [], []
    cmp_p, slc_p, win_p, cmp_s, slc_s, win_s = [], [], [], [], [], []
    for i in range(DEPTH):
        j = i // N_MIXERS
        if i % N_MIXERS == 0:
            prm = (ssm_w_in[j], ssm_conv_w[j], ssm_conv_b[j], ssm_dt_bias[j], ssm_a_log[j], ssm_d[j],
                   ssm_gate_norm[j], ssm_w_out[j])
            zc = jnp.zeros((hp.shape[0], CONV_WIDTH - 1, CONV_DIM), hp.dtype)
            zs = jnp.zeros((hp.shape[0], SSM_HEADS, SSM_HEAD_DIM, SSM_STATE), jnp.float32)
            yp, cp, sp = mamba_mixer(rms_norm(hp, ssm_norm[j]), zc, zs, *prm)
            ys, cs, ss = mamba_mixer(rms_norm(hs, ssm_norm[j]), state_conv[j], state_ssm[j], *prm)
            conv_p.append(cp); ssm_p.append(sp); conv_s.append(cs); ssm_s.append(ss)
        else:
            prm = (att_w_in[j], att_cmp_pos[j], att_cmp_w1[j], att_cmp_w2[j], att_w_out[j])
            yp, kp, lp, wp = nsa_prompt(rms_norm(hp, att_norm[j]), slopes, *prm)
            ys, kq, lq, wq = nsa_sample(rms_norm(hs, att_norm[j]), cache_cmp_kv[j], cache_slc_kv[j],
                                        state_win_kv[j], page_table, slopes, *prm)
            cmp_p.append(kp); slc_p.append(lp); win_p.append(wp)
            cmp_s.append(kq); slc_s.append(lq); win_s.append(wq)
        hp = hp + yp
        hs = hs + ys
        hp = hp + sq_relu_mlp(rms_norm(hp, mlp_norm[i]), mlp_w_up[i], mlp_w_down[i])
        hs = hs + sq_relu_mlp(rms_norm(hs, mlp_norm[i]), mlp_w_up[i], mlp_w_down[i])
    y_prompt = rms_norm(hp, final_norm)
    y_sample = rms_norm(hs, final_norm)
    return (y_prompt, y_sample,
            jnp.stack(ssm_p), jnp.stack(conv_p), jnp.stack(cmp_p), jnp.stack(slc_p), jnp.stack(win_p),
            jnp.stack(ssm_s), jnp.stack(conv_s), jnp.stack(cmp_s), jnp.stack(slc_s), jnp.stack(win_s))
```

```python
import functools

import numpy as np
import jax
import jax.numpy as jnp
from jax import lax
from jax.experimental import pallas as pl
from jax.experimental.pallas import tpu as pltpu

F32 = jnp.float32
BF16 = jnp.bfloat16

D_MODEL = 1024
D_INNER = 2048
SSM_HEADS = 32
SSM_HEAD_DIM = 64
SSM_GROUPS = 8
SSM_STATE = 128
CONV_WIDTH = 4
CONV_DIM = 4096
SSM_IN_PAD = 6400
SSD_CHUNK = 128
ATT_HEADS = 16
HEAD_DIM = 64
KV_HEADS = 4
GQA = ATT_HEADS // KV_HEADS
KV_DIM = KV_HEADS * HEAD_DIM
CMP_STRIDE = 16
CMP_HIDDEN = 128
SLC_BLOCK = 64
SLC_SHIFT = 6
SLC_TOP_N = 16
WINDOW = 512
N_BRANCH = 3
ATT_IN_PAD = 2688
PAGE_SIZE = 128
NORM_EPS = 1e-5
NEG_INF = -1e30
LANES = 128
VMEM_LIMIT = 56 * 1024 * 1024


def _cparams(sem):
    return pltpu.CompilerParams(dimension_semantics=sem, vmem_limit_bytes=VMEM_LIMIT)


def _dot(a, b):
    return jnp.dot(a, b, preferred_element_type=F32)


def _dot_nt(a, b):
    return lax.dot_general(a, b, (((1,), (1,)), ((), ())), preferred_element_type=F32)


def _split_dot(a, onehot, left=False, terms=3):
    acc = None
    rem = a
    for _ in range(terms):
        hi = rem.astype(BF16)
        part = _dot(onehot, hi) if left else _dot(hi, onehot)
        acc = part if acc is None else acc + part
        rem = rem - hi.astype(F32)
    return acc


def _silu(x):
    return x / (1.0 + jnp.exp(-x))


def _mm_kernel(*refs, has_norm, act, has_res, has_fnorm):
    it = iter(refs)
    x_ref = next(it)
    g_ref = next(it) if has_norm else None
    w_ref = next(it)
    res_ref = next(it) if has_res else None
    fg_ref = next(it) if has_fnorm else None
    o_ref = next(it)
    xn_ref = next(it) if has_norm else None
    if has_norm:
        @pl.when(pl.program_id(1) == 0)
        def _():
            x = x_ref[...].astype(F32)
            ms = jnp.mean(x * x, axis=-1, keepdims=True)
            xn_ref[...] = (x * lax.rsqrt(ms + NORM_EPS) * g_ref[...]).astype(BF16)
        a = xn_ref[...]
    else:
        a = x_ref[...].astype(BF16)
    y = _dot(a, w_ref[...])
    if act == "relu2":
        y = jnp.square(jnp.maximum(y, 0.0))
    if has_res:
        y = y + res_ref[...]
    if has_fnorm:
        ms = jnp.mean(y * y, axis=-1, keepdims=True)
        y = y * lax.rsqrt(ms + NORM_EPS) * fg_ref[...]
    o_ref[...] = y.astype(o_ref.dtype)


def fused_matmul(x, w, *, gain=None, res=None, fgain=None, act=None, out_dtype=F32, tm, tn, name):
    m, k = x.shape
    n = w.shape[1]
    assert m % tm == 0 and n % tn == 0 and w.shape[0] == k
    assert fgain is None or tn == n
    in_specs = [pl.BlockSpec((tm, k), lambda i, j: (i, 0))]
    args = [x]
    if gain is not None:
        in_specs.append(pl.BlockSpec((1, k), lambda i, j: (0, 0)))
        args.append(gain.reshape(1, k).astype(F32))
    in_specs.append(pl.BlockSpec((k, tn), lambda i, j: (0, j)))
    args.append(w)
    if res is not None:
        in_specs.append(pl.BlockSpec((tm, tn), lambda i, j: (i, j)))
        args.append(res)
    if fgain is not None:
        in_specs.append(pl.BlockSpec((1, n), lambda i, j: (0, 0)))
        args.append(fgain.reshape(1, n).astype(F32))
    scratch = [pltpu.VMEM((tm, k), BF16)] if gain is not None else []
    kern = functools.partial(_mm_kernel, has_norm=gain is not None, act=act, has_res=res is not None,
                             has_fnorm=fgain is not None)
    return pl.pallas_call(
        kern,
        grid=(m // tm, n // tn),
        in_specs=in_specs,
        out_specs=pl.BlockSpec((tm, tn), lambda i, j: (i, j)),
        out_shape=jax.ShapeDtypeStruct((m, n), out_dtype),
        scratch_shapes=scratch,
        compiler_params=_cparams(("parallel", "arbitrary")),
        name=name,
    )(*args)


def _row_tile(m, pref):
    for t in (pref, 512, 256, 128, 64, 32, 16, 8):
        if t <= pref and m % t == 0:
            return t
    raise ValueError(m)


def mlp_block(h, norm_g, w_up, w_down, final_g, name):
    tm = _row_tile(h.shape[0], 512)
    a = fused_matmul(h, w_up, gain=norm_g, act="relu2", out_dtype=BF16, tm=tm, tn=1024, name=name + "_up")
    if final_g is None:
        return fused_matmul(a, w_down, res=h, tm=tm, tn=512, name=name + "_down")
    return fused_matmul(a, w_down, res=h, fgain=final_g, tm=tm, tn=D_MODEL, name=name + "_down")


def _ssd_kernel(z_ref, x_ref, bc_ref, dt_ref, cw_ref, cb_ref, dtb_ref, alog_ref, dsk_ref, gn_ref,
                c0_ref, s0_ref, tri_ref, ex_ref,
                y_ref, sout_ref,
                ext_ref, st_ref, xs_ref, b_ref, c_ref, *, t_valid, n_chunks):
    c = pl.program_id(1)
    L = SSD_CHUNK

    @pl.when(c == 0)
    def _():
        ext_ref[0:8, :] = c0_ref[0]
        st_ref[...] = s0_ref[0].T

    @pl.when(c > 0)
    def _():
        ext_ref[0:8, :] = ext_ref[L:L + 8, :]

    ext_ref[8:L + 8, 0:D_INNER] = x_ref[...]
    ext_ref[8:L + 8, D_INNER:CONV_DIM] = bc_ref[...]

    cblk = 512
    for i in range(CONV_DIM // cblk):
        sl = slice(i * cblk, (i + 1) * cblk)
        acc = cb_ref[:, sl]
        for k in range(CONV_WIDTH):
            acc = acc + cw_ref[k:k + 1, sl] * ext_ref[5 + k:5 + k + L, sl]
        v = _silu(acc)
        if i < 4:
            xs_ref[:, sl] = v
        elif i < 6:
            b_ref[:, (i - 4) * cblk:(i - 3) * cblk] = v
        else:
            c_ref[:, (i - 6) * cblk:(i - 5) * cblk] = v

    dt_raw = dt_ref[...] + dtb_ref[...]
    dt = jnp.maximum(dt_raw, 0.0) + jnp.log1p(jnp.exp(-jnp.abs(dt_raw)))
    if t_valid % L:
        row = c * L + lax.broadcasted_iota(jnp.int32, (L, LANES), 0)
        dt = jnp.where(row < t_valid, dt, 0.0)
    da = dt * (-jnp.exp(alog_ref[...]))
    acs = _split_dot(da, tri_ref[...], left=True)
    acs_t = acs.T
    expd = _split_dot(jnp.concatenate([acs, dt], axis=0), ex_ref[...])
    a_exp = expd[0:L]
    dt_exp = expd[L:2 * L]

    li = lax.broadcasted_iota(jnp.int32, (L, L), 0)
    si = lax.broadcasted_iota(jnp.int32, (L, L), 1)
    causal = si <= li
    first_half = si < SSM_HEAD_DIM
    gw = D_INNER // SSM_GROUPS
    for g in range(SSM_GROUPS):
        gs = slice(g * gw, (g + 1) * gw)
        bg = b_ref[:, g * SSM_STATE:(g + 1) * SSM_STATE]
        cg = c_ref[:, g * SSM_STATE:(g + 1) * SSM_STATE].astype(BF16)
        cbm = _dot_nt(cg, bg.astype(BF16))
        ag = a_exp[:, gs]
        xg = xs_ref[:, gs]
        xdt = xg * dt_exp[:, gs]
        a_last = ag[L - 1:L, :]
        y_off = _dot(cg, st_ref[:, gs].astype(BF16)) * jnp.exp(ag)
        ys = []
        for p in range(2):
            h0 = 4 * g + 2 * p
            xp = xdt[:, p * LANES:(p + 1) * LANES].astype(BF16)
            outs = []
            for h in (h0, h0 + 1):
                seg = acs[:, h:h + 1] - acs_t[h:h + 1, :]
                lm = jnp.exp(jnp.where(causal, seg, NEG_INF))
                outs.append(_dot((cbm * lm).astype(BF16), xp))
            ys.append(jnp.where(first_half, outs[0], outs[1]))
        y = jnp.concatenate(ys, axis=1) + y_off
        xd = (xdt * jnp.exp(a_last - ag)).astype(BF16)
        st_ref[:, gs] = st_ref[:, gs] * jnp.exp(a_last) + _dot(bg.T.astype(BF16), xd)
        zg = z_ref[:, gs]
        yy = (y + dsk_ref[:, gs] * xg) * _silu(zg)
        ms = jnp.mean(yy * yy, axis=-1, keepdims=True)
        y_ref[:, gs] = (yy * lax.rsqrt(ms + NORM_EPS) * gn_ref[:, gs]).astype(y_ref.dtype)

    @pl.when(c == n_chunks - 1)
    def _():
        sout_ref[0] = st_ref[...].T


def ssd_mixer(zx, conv0, st0, prm, *, batch, t_pad, t_valid, name):
    n_chunks = t_pad // SSD_CHUNK
    L = SSD_CHUNK
    conv_w, conv_b, dt_bias, a_log, d_skip, gate_norm = prm
    lane_pad = lambda v: jnp.pad(v.astype(F32), (0, LANES - SSM_HEADS)).reshape(1, LANES)
    tri = jnp.asarray(np.tril(np.ones((L, L), np.float32)), BF16)
    ex_np = np.zeros((LANES, D_INNER), np.float32)
    ex_np[np.arange(D_INNER) // SSM_HEAD_DIM, np.arange(D_INNER)] = 1.0
    ex = jnp.asarray(ex_np, BF16)
    c0 = jnp.pad(conv0.astype(F32), ((0, 0), (8 - (CONV_WIDTH - 1), 0), (0, 0)))
    row = lambda b, c: (b * n_chunks + c)
    const = lambda shape: pl.BlockSpec(shape, lambda b, c: (0,) * len(shape))
    kern = functools.partial(_ssd_kernel, t_valid=t_valid, n_chunks=n_chunks)
    y, s_out = pl.pallas_call(
        kern,
        grid=(batch, n_chunks),
        in_specs=[
            pl.BlockSpec((L, D_INNER), lambda b, c: (row(b, c), 0)),
            pl.BlockSpec((L, D_INNER), lambda b, c: (row(b, c), 1)),
            pl.BlockSpec((L, D_INNER), lambda b, c: (row(b, c), 2)),
            pl.BlockSpec((L, LANES), lambda b, c: (row(b, c), (D_INNER + CONV_DIM) // LANES)),
            const((CONV_WIDTH, CONV_DIM)), const((1, CONV_DIM)), const((1, LANES)), const((1, LANES)),
            const((1, D_INNER)), const((1, D_INNER)),
            pl.BlockSpec((1, 8, CONV_DIM), lambda b, c: (b, 0, 0)),
            pl.BlockSpec((1, D_INNER, SSM_STATE), lambda b, c: (b, 0, 0)),
            const((L, L)), const((LANES, D_INNER)),
        ],
        out_specs=[
            pl.BlockSpec((L, D_INNER), lambda b, c: (row(b, c), 0)),
            pl.BlockSpec((1, D_INNER, SSM_STATE), lambda b, c: (b, 0, 0)),
        ],
        out_shape=[
            jax.ShapeDtypeStruct((batch * t_pad, D_INNER), BF16),
            jax.ShapeDtypeStruct((batch, D_INNER, SSM_STATE), F32),
        ],
        scratch_shapes=[
            pltpu.VMEM((L + 8, CONV_DIM), F32),
            pltpu.VMEM((SSM_STATE, D_INNER), F32),
            pltpu.VMEM((L, D_INNER), F32),
            pltpu.VMEM((L, SSM_GROUPS * SSM_STATE), F32),
            pltpu.VMEM((L, SSM_GROUPS * SSM_STATE), F32),
        ],
        compiler_params=_cparams(("parallel", "arbitrary")),
        name=name,
    )(zx, zx, zx, zx,
      conv_w.astype(F32).T, conv_b.astype(F32).reshape(1, CONV_DIM), lane_pad(dt_bias), lane_pad(a_log),
      jnp.repeat(d_skip.astype(F32), SSM_HEAD_DIM).reshape(1, D_INNER), gate_norm.astype(F32).reshape(1, D_INNER),
      c0, st0.astype(F32).reshape(batch, D_INNER, SSM_STATE), tri, ex)
    return y, s_out.reshape(batch, SSM_HEADS, SSM_HEAD_DIM, SSM_STATE)


CMP_PAGES = 16


def _cmp_u_kernel(pt_ref, *refs):
    del pt_ref
    pages = refs[:CMP_PAGES]
    perm_ref, w_ref, uf_ref, us_ref, y_ref = refs[CMP_PAGES:]
    for p, page in enumerate(pages):
        yp = _dot(perm_ref[...], page[...].astype(BF16))
        for j in range(CMP_STRIDE):
            y_ref[j, p * 8:(p + 1) * 8, :] = yp[j * 8:(j + 1) * 8, :]
    nq = 4
    accs = [None] * nq
    for j in range(CMP_STRIDE):
        xj = y_ref[j].astype(BF16)
        for q in range(nq):
            part = _dot(xj[:, q * LANES:(q + 1) * LANES], w_ref[q // 2, j])
            accs[q] = part if accs[q] is None else accs[q] + part
    for q in range(nq):
        for gl in range(2):
            base = gl * 2 * CMP_HIDDEN
            uf_ref[0, 2 * q + gl] = accs[q][:, base:base + CMP_HIDDEN]
            us_ref[0, 2 * q + gl] = accs[q][:, base + CMP_HIDDEN:base + 2 * CMP_HIDDEN]


def cmp_project(src2d, page_ids, col_block, w1, name):
    batch, n_pages = page_ids.shape
    assert n_pages % CMP_PAGES == 0
    n_sub = n_pages * (PAGE_SIZE // CMP_STRIDE)
    rows = CMP_PAGES * (PAGE_SIZE // CMP_STRIDE)
    w1r = w1.astype(F32).reshape(2, 2 * CMP_STRIDE, HEAD_DIM, CMP_HIDDEN)
    wj = jnp.concatenate([w1r[:, :CMP_STRIDE], w1r[:, CMP_STRIDE:]], axis=-1)
    zero = jnp.zeros_like(wj)
    wpad = jnp.concatenate([jnp.concatenate([wj, zero], axis=-1), jnp.concatenate([zero, wj], axis=-1)], axis=-2)
    wpad = wpad.astype(BF16)
    perm = np.zeros((PAGE_SIZE, PAGE_SIZE), np.float32)
    for j in range(CMP_STRIDE):
        for s in range(PAGE_SIZE // CMP_STRIDE):
            perm[j * 8 + s, CMP_STRIDE * s + j] = 1.0
    perm = jnp.asarray(perm, BF16)

    def page_spec(p):
        return pl.BlockSpec((PAGE_SIZE, 4 * LANES), lambda b, c, pt: (pt[b, c * CMP_PAGES + p], col_block))

    u_shape = jax.ShapeDtypeStruct((batch, 2 * KV_HEADS, n_sub, CMP_HIDDEN), F32)
    u_spec = pl.BlockSpec((1, 2 * KV_HEADS, rows, CMP_HIDDEN), lambda b, c, pt: (b, 0, c, 0))
    return pl.pallas_call(
        _cmp_u_kernel,
        grid_spec=pltpu.PrefetchScalarGridSpec(
            num_scalar_prefetch=1,
            grid=(batch, n_pages // CMP_PAGES),
            in_specs=[page_spec(p) for p in range(CMP_PAGES)] + [
                pl.BlockSpec((PAGE_SIZE, PAGE_SIZE), lambda b, c, pt: (0, 0)),
                pl.BlockSpec((2, CMP_STRIDE, LANES, 4 * LANES), lambda b, c, pt: (0, 0, 0, 0)),
            ],
            out_specs=[u_spec, u_spec],
            scratch_shapes=[pltpu.VMEM((CMP_STRIDE, rows, 4 * LANES), F32)],
        ),
        out_shape=[u_shape, u_shape],
        compiler_params=_cparams(("parallel", "arbitrary")),
        name=name,
    )(page_ids, *([src2d] * CMP_PAGES), perm, wpad)


def _cmp_out_kernel(uf_ref, us_ref, pos_ref, w1_ref, w2_ref, o_ref, sh_ref, *, n_sub):
    sh_ref[0:n_sub, :] = us_ref[0, 0]
    sh_ref[n_sub:n_sub + 8, :] = jnp.zeros((8, CMP_HIDDEN), F32)
    pos_term = _dot(pos_ref[0].astype(BF16), w1_ref[0])[0:1, :]
    hid = jax.nn.gelu(uf_ref[0, 0] + sh_ref[pl.ds(1, n_sub), :] + pos_term)
    out = _dot(hid.astype(BF16), w2_ref[0])
    rows = lax.broadcasted_iota(jnp.int32, out.shape, 0)
    o_ref[0, 0] = jnp.where(rows < n_sub - 1, out, 0.0).astype(o_ref.dtype)


def cmp_finish(uf, us, pos_emb, w1, w2, name):
    batch, nkvg, n_sub, _ = uf.shape
    flat = 2 * CMP_STRIDE * HEAD_DIM
    pos = jnp.pad(pos_emb.astype(F32).reshape(2, 1, flat), ((0, 0), (0, 7), (0, 0)))
    u_spec = pl.BlockSpec((1, 1, n_sub, CMP_HIDDEN), lambda b, k: (b, k, 0, 0))
    return pl.pallas_call(
        functools.partial(_cmp_out_kernel, n_sub=n_sub),
        grid=(batch, nkvg),
        in_specs=[u_spec, u_spec,
                  pl.BlockSpec((1, 8, flat), lambda b, k: (k // KV_HEADS, 0, 0)),
                  pl.BlockSpec((1, flat, CMP_HIDDEN), lambda b, k: (k // KV_HEADS, 0, 0)),
                  pl.BlockSpec((1, CMP_HIDDEN, HEAD_DIM), lambda b, k: (k // KV_HEADS, 0, 0))],
        out_specs=pl.BlockSpec((1, 1, n_sub, HEAD_DIM), lambda b, k: (b, k, 0, 0)),
        out_shape=jax.ShapeDtypeStruct((batch, nkvg, n_sub, HEAD_DIM), BF16),
        scratch_shapes=[pltpu.VMEM((n_sub + 8, CMP_HIDDEN), F32)],
        compiler_params=_cparams(("parallel", "parallel")),
        name=name,
    )(uf, us, pos, w1.astype(BF16), w2.astype(BF16))


def _cmp_attn_kernel(q_ref, kc_ref, vc_ref, slope_ref, wsel_ref, o_ref, sel_ref, *, tq, ncp, nblk_pad, n_sel, q_base,
                     want_idx):
    qi = pl.program_id(2)
    r = GQA * tq
    q = (q_ref[0].reshape(r, HEAD_DIM) * (HEAD_DIM ** -0.5)).astype(BF16)
    s = _dot_nt(q, kc_ref[0, 0])
    tpos = q_base + qi * tq + lax.broadcasted_iota(jnp.int32, (tq, 1), 0)
    c_end = CMP_STRIDE * lax.broadcasted_iota(jnp.int32, (1, ncp), 1) + (2 * CMP_STRIDE - 1)
    d = (tpos - c_end).astype(F32)
    ok = d >= 0.0
    s3 = s.reshape(GQA, tq, ncp) - slope_ref[0] * d[None]
    s3 = jnp.where(ok[None], s3, NEG_INF)
    m = jnp.max(s3, axis=-1, keepdims=True)
    e = jnp.exp(s3 - m)
    any_ok = (tpos >= 2 * CMP_STRIDE - 1).astype(F32)
    p = e / jnp.sum(e, axis=-1, keepdims=True) * any_ok[None]
    o = _dot(p.reshape(r, ncp).astype(BF16), vc_ref[0, 0])
    o_ref[0] = o.reshape(GQA, tq, HEAD_DIM)

    imp = jnp.sum(p, axis=0)
    p_slc = _split_dot(imp, wsel_ref[...], terms=2)
    tq_l = max(tq, LANES)
    if tq < LANES:
        p_slc = jnp.concatenate([p_slc, jnp.zeros((tq_l - tq, nblk_pad), F32)], axis=0)
    sc = p_slc.T
    blk = lax.broadcasted_iota(jnp.int32, (nblk_pad, tq_l), 0)
    tp = q_base + qi * tq + lax.broadcasted_iota(jnp.int32, (nblk_pad, tq_l), 1)
    cur = jnp.right_shift(tp, SLC_SHIFT)
    forced = (blk == 0) | (blk == cur) | (blk == cur - 1)
    reach = blk * SLC_BLOCK <= tp
    sc = jnp.where(forced, 1e30, jnp.where(reach, sc, -1.0))
    sc = jnp.where(blk < n_sel, sc, -3.0)
    chosen = jnp.zeros((nblk_pad, tq_l), F32)
    idx_rows = []
    for _ in range(min(SLC_TOP_N, n_sel)):
        mx = jnp.max(sc, axis=0, keepdims=True)
        idx = jnp.min(jnp.where(sc == mx, blk, nblk_pad), axis=0, keepdims=True)
        hit = blk == idx
        chosen = jnp.where(hit, 1.0, chosen)
        sc = jnp.where(hit, -2.0, sc)
        idx_rows.append(idx)
    if want_idx:
        sel_ref[0, 0] = jnp.concatenate(idx_rows, axis=0)
    else:
        sel_ref[0, 0] = ((chosen.T[0:tq] - 1.0) * 1e30).astype(sel_ref.dtype)


def cmp_attention(q_h, kvc, slopes, *, tq, n_sel, q_base, want_idx, name):
    b, _, t, _ = q_h.shape
    ncp = kvc.shape[2]
    nblk_pad = -(-n_sel // LANES) * LANES
    c = np.arange(ncp)[:, None]
    j = np.arange(nblk_pad)[None, :]
    wsel = ((c // 4 == j).astype(np.float32) + ((c + 1) // 4 == j).astype(np.float32)) * (c < ncp - 1)
    tq_l = max(tq, LANES)
    if want_idx:
        sel_shape = jax.ShapeDtypeStruct((b, KV_HEADS, SLC_TOP_N, (t // tq) * tq_l), jnp.int32)
        sel_spec = pl.BlockSpec((1, 1, SLC_TOP_N, tq_l), lambda bi, g, qi: (bi, g, 0, qi))
    else:
        sel_shape = jax.ShapeDtypeStruct((b, KV_HEADS, t, nblk_pad), BF16)
        sel_spec = pl.BlockSpec((1, 1, tq, nblk_pad), lambda bi, g, qi: (bi, g, qi, 0))
    kern = functools.partial(_cmp_attn_kernel, tq=tq, ncp=ncp, nblk_pad=nblk_pad, n_sel=n_sel, q_base=q_base,
                             want_idx=want_idx)
    return pl.pallas_call(
        kern,
        grid=(b, KV_HEADS, t // tq),
        in_specs=[
            pl.BlockSpec((1, GQA, tq, HEAD_DIM), lambda bi, g, qi: (bi, g, qi, 0)),
            pl.BlockSpec((1, 1, ncp, HEAD_DIM), lambda bi, g, qi: (bi, g, 0, 0)),
            pl.BlockSpec((1, 1, ncp, HEAD_DIM), lambda bi, g, qi: (bi, KV_HEADS + g, 0, 0)),
            pl.BlockSpec((1, GQA, 1, 1), lambda bi, g, qi: (g, 0, 0, 0)),
            pl.BlockSpec((ncp, nblk_pad), lambda bi, g, qi: (0, 0)),
        ],
        out_specs=[pl.BlockSpec((1, GQA, tq, HEAD_DIM), lambda bi, g, qi: (bi, g, qi, 0)), sel_spec],
        out_shape=[jax.ShapeDtypeStruct((b, ATT_HEADS, t, HEAD_DIM), F32), sel_shape],
        compiler_params=_cparams(("parallel", "parallel", "parallel")),
        name=name,
    )(q_h, kvc, kvc, slopes.reshape(KV_HEADS, GQA, 1, 1), jnp.asarray(wsel, BF16))


SEL_FEAT = 256


def _sel_kernel(qa_ref, ka_ref, v_ref, slope_ref, o_ref, m_ref, l_ref, acc_ref, *, tq, tk):
    qi = pl.program_id(2)
    r = GQA * tq
    t0 = qi * tq
    m_ref[...] = jnp.full((r, 1), NEG_INF, F32)
    l_ref[...] = jnp.zeros((r, 1), F32)
    acc_ref[...] = jnp.zeros((r, HEAD_DIM), F32)
    qa = qa_ref[0, 0].reshape(r, SEL_FEAT)
    tpos = t0 + lax.broadcasted_iota(jnp.int32, (tq, 1), 0)

    def body(j, carry):
        k0 = pl.multiple_of(j * tk, tk)
        s = _dot_nt(qa, ka_ref[0, 0, pl.ds(k0, tk), :])
        kpos = k0 + lax.broadcasted_iota(jnp.int32, (1, tk), 1)
        rel = (kpos - t0).astype(F32)
        s3 = s.reshape(GQA, tq, tk) + slope_ref[0] * rel[None]
        s3 = jnp.where((kpos <= tpos)[None], s3, NEG_INF)
        s = s3.reshape(r, tk)
        m_prev = m_ref[...]
        m_new = jnp.maximum(m_prev, jnp.max(s, axis=-1, keepdims=True))
        alpha = jnp.exp(m_prev - m_new)
        p = jnp.exp(s - m_new)
        l_ref[...] = alpha * l_ref[...] + jnp.sum(p, axis=-1, keepdims=True)
        acc_ref[...] = alpha * acc_ref[...] + _dot(p.astype(BF16), v_ref[0, 0, pl.ds(k0, tk), :])
        m_ref[...] = m_new
        return carry

    lax.fori_loop(0, (t0 + tq + tk - 1) // tk, body, 0)
    o_ref[0] = (acc_ref[...] / l_ref[...]).reshape(GQA, tq, HEAD_DIM)


def sel_attention_dense(qa, ka, v, slopes, *, tq, tk, name):
    b, _, _, t, _ = qa.shape
    r = GQA * tq
    return pl.pallas_call(
        functools.partial(_sel_kernel, tq=tq, tk=tk),
        grid=(b, KV_HEADS, t // tq),
        in_specs=[
            pl.BlockSpec((1, 1, GQA, tq, SEL_FEAT), lambda bi, g, qi: (bi, g, 0, qi, 0)),
            pl.BlockSpec((1, 1, t, SEL_FEAT), lambda bi, g, qi: (bi, g, 0, 0)),
            pl.BlockSpec((1, 1, t, HEAD_DIM), lambda bi, g, qi: (bi, g, 0, 0)),
            pl.BlockSpec((1, GQA, 1, 1), lambda bi, g, qi: (g, 0, 0, 0)),
        ],
        out_specs=pl.BlockSpec((1, GQA, tq, HEAD_DIM), lambda bi, g, qi: (bi, g, qi, 0)),
        out_shape=jax.ShapeDtypeStruct((b, ATT_HEADS, t, HEAD_DIM), F32),
        scratch_shapes=[pltpu.VMEM((r, 1), F32), pltpu.VMEM((r, 1), F32), pltpu.VMEM((r, HEAD_DIM), F32)],
        compiler_params=_cparams(("parallel", "parallel", "arbitrary")),
        name=name,
    )(qa, ka, v, slopes.reshape(KV_HEADS, GQA, 1, 1))


def _sel_gather_kernel(idx_ref, pt_ref, q_ref, slope_ref, cache_ref, new_ref, o_ref, kbuf, vbuf, sem, *,
                       n_tok, tq, past_len, n_steps):
    b = pl.program_id(0)
    g = pl.program_id(1)
    step = b * KV_HEADS + g
    slot = step % 2
    new_blk = past_len // SLC_BLOCK
    halves = PAGE_SIZE // SLC_BLOCK

    def block_copy(bb, gg, sl, t, n, kv, blk, newest):
        buf = vbuf if kv else kbuf
        dst = buf.at[sl, t, pl.ds(n * SLC_BLOCK, SLC_BLOCK), :]
        col = kv * KV_HEADS + gg
        if newest:
            src = new_ref.at[bb, :, col, :]
        else:
            page = pt_ref[bb * (past_len // PAGE_SIZE) + jnp.minimum(blk, new_blk - 1) // halves]
            src = cache_ref.at[page, pl.ds((blk % halves) * SLC_BLOCK, SLC_BLOCK), col, :]
        return pltpu.make_async_copy(src, dst, sem.at[sl])

    def for_each_block(bb, gg, fn):
        for t in range(n_tok):
            for n in range(SLC_TOP_N):
                blk = idx_ref[((bb * KV_HEADS + gg) * SLC_TOP_N + n) * n_tok + t]
                for kv in range(2):
                    fn(t, n, kv, blk)

    def start(bb, gg, sl):
        def one(t, n, kv, blk):
            @pl.when(blk >= new_blk)
            def _():
                block_copy(bb, gg, sl, t, n, kv, blk, True).start()

            @pl.when(blk < new_blk)
            def _():
                block_copy(bb, gg, sl, t, n, kv, blk, False).start()
        for_each_block(bb, gg, one)

    @pl.when(step == 0)
    def _():
        start(b, g, slot)

    @pl.when(step + 1 < n_steps)
    def _():
        nxt = step + 1
        start(nxt // KV_HEADS, nxt % KV_HEADS, 1 - slot)

    for_each_block(b, g, lambda t, n, kv, blk: block_copy(b, g, slot, t, n, kv, blk, False).wait())

    r = GQA * tq
    q = (q_ref[0].reshape(r, HEAD_DIM) * (HEAD_DIM ** -0.5)).astype(BF16)
    nk = SLC_TOP_N * SLC_BLOCK
    lane = lax.broadcasted_iota(jnp.int32, (1, nk), 1)
    trow = lax.broadcasted_iota(jnp.int32, (tq, 1), 0)
    tpos = past_len + trow
    acc = jnp.zeros((GQA, tq, HEAD_DIM), F32)
    for t in range(n_tok):
        kpos = jnp.bitwise_and(lane, SLC_BLOCK - 1)
        slot_of_lane = jnp.right_shift(lane, SLC_SHIFT)
        for n in range(SLC_TOP_N):
            blk = idx_ref[((b * KV_HEADS + g) * SLC_TOP_N + n) * n_tok + t]
            kpos = jnp.where(slot_of_lane == n, kpos + blk * SLC_BLOCK, kpos)
        d = (tpos - kpos).astype(F32)
        s = _dot_nt(q, kbuf[slot, t].astype(BF16))
        s3 = s.reshape(GQA, tq, nk) - slope_ref[0] * d[None]
        s3 = jnp.where((d >= 0.0)[None], s3, NEG_INF)
        m = jnp.max(s3, axis=-1, keepdims=True)
        e = jnp.exp(s3 - m)
        p = e / jnp.sum(e, axis=-1, keepdims=True)
        o = _dot(p.reshape(r, nk).astype(BF16), vbuf[slot, t].astype(BF16)).reshape(GQA, tq, HEAD_DIM)
        acc = jnp.where((trow == t)[None], o, acc)
    o_ref[0] = acc


def sel_attention_gather(q_h, idx, page_table, cache, new_kv, slopes, *, n_tok, past_len, name):
    b, _, tq, _ = q_h.shape
    n_steps = b * KV_HEADS
    nk = SLC_TOP_N * SLC_BLOCK
    kern = functools.partial(_sel_gather_kernel, n_tok=n_tok, tq=tq, past_len=past_len, n_steps=n_steps)
    return pl.pallas_call(
        kern,
        grid_spec=pltpu.PrefetchScalarGridSpec(
            num_scalar_prefetch=2,
            grid=(b, KV_HEADS),
            in_specs=[
                pl.BlockSpec((1, GQA, tq, HEAD_DIM), lambda bi, g, *_: (bi, g, 0, 0)),
                pl.BlockSpec((1, GQA, 1, 1), lambda bi, g, *_: (g, 0, 0, 0)),
                pl.BlockSpec(memory_space=pl.ANY),
                pl.BlockSpec(memory_space=pl.ANY),
            ],
            out_specs=pl.BlockSpec((1, GQA, tq, HEAD_DIM), lambda bi, g, *_: (bi, g, 0, 0)),
            scratch_shapes=[pltpu.VMEM((2, n_tok, nk, HEAD_DIM), F32), pltpu.VMEM((2, n_tok, nk, HEAD_DIM), F32),
                            pltpu.SemaphoreType.DMA((2,))],
        ),
        out_shape=jax.ShapeDtypeStruct((b, ATT_HEADS, tq, HEAD_DIM), F32),
        compiler_params=_cparams(("arbitrary", "arbitrary")),
        name=name,
    )(idx.reshape(-1), page_table.reshape(-1), q_h, slopes.reshape(KV_HEADS, GQA, 1, 1), cache, new_kv)


def _win_kernel(q_ref, k_ref, v_ref, slope_ref, oc_ref, os_ref, gl_ref, o_ref, *, tq, nk, k_valid_from):
    qi = pl.program_id(2)
    r = GQA * tq
    r0 = pl.multiple_of(qi * tq, tq)
    q = (q_ref[0].reshape(r, HEAD_DIM) * (HEAD_DIM ** -0.5)).astype(BF16)
    s = _dot_nt(q, k_ref[0, 0, pl.ds(r0, nk), :])
    qrow = r0 + WINDOW + lax.broadcasted_iota(jnp.int32, (tq, 1), 0)
    krow = r0 + lax.broadcasted_iota(jnp.int32, (1, nk), 1)
    di = qrow - krow
    ok = (di >= 0) & (di < WINDOW) & (krow >= k_valid_from)
    s3 = s.reshape(GQA, tq, nk) - slope_ref[0] * di.astype(F32)[None]
    s3 = jnp.where(ok[None], s3, NEG_INF)
    m = jnp.max(s3, axis=-1, keepdims=True)
    e = jnp.exp(s3 - m)
    p = e / jnp.sum(e, axis=-1, keepdims=True)
    ow = _dot(p.reshape(r, nk).astype(BF16), v_ref[0, 0, pl.ds(r0, nk), :]).reshape(GQA, tq, HEAD_DIM)
    gates = 1.0 / (1.0 + jnp.exp(-gl_ref[0]))
    o_ref[0] = gates[:, :, 0:1] * oc_ref[0] + gates[:, :, 1:2] * os_ref[0] + gates[:, :, 2:3] * ow


def win_attention_merge(q_h, kw, vw, slopes, o_c, o_s, gate_logits, *, tq, k_valid_from, name):
    b, _, t, _ = q_h.shape
    nk = tq + WINDOW
    lk = kw.shape[2]
    assert lk >= t + WINDOW
    head_spec = lambda w: pl.BlockSpec((1, GQA, tq, w), lambda bi, g, qi: (bi, g, qi, 0))
    kv_spec = pl.BlockSpec((1, 1, lk, HEAD_DIM), lambda bi, g, qi: (bi, g, 0, 0))
    return pl.pallas_call(
        functools.partial(_win_kernel, tq=tq, nk=nk, k_valid_from=k_valid_from),
        grid=(b, KV_HEADS, t // tq),
        in_specs=[head_spec(HEAD_DIM), kv_spec, kv_spec,
                  pl.BlockSpec((1, GQA, 1, 1), lambda bi, g, qi: (g, 0, 0, 0)),
                  head_spec(HEAD_DIM), head_spec(HEAD_DIM), head_spec(N_BRANCH)],
        out_specs=head_spec(HEAD_DIM),
        out_shape=jax.ShapeDtypeStruct((b, ATT_HEADS, t, HEAD_DIM), F32),
        compiler_params=_cparams(("parallel", "parallel", "parallel")),
        name=name,
    )(q_h, kw, vw, slopes.reshape(KV_HEADS, GQA, 1, 1), o_c, o_s, gate_logits)


def _heads(x2d, b, t, n_heads):
    return x2d.reshape(b, t, n_heads, HEAD_DIM).transpose(0, 2, 1, 3)


def _kv_stack(qkv, b, t, col):
    return qkv[:, col:col + 2 * KV_DIM].reshape(b, t, 2, KV_HEADS, HEAD_DIM)


def alibi_slopes():
    return 2.0 ** (-8.0 * jnp.arange(1, ATT_HEADS + 1, dtype=jnp.float32) / ATT_HEADS)


def nsa_prompt_layer(h2d, b, t, norm_g, w_in, cmp_pos, cmp_w1, cmp_w2, w_out, slopes):
    tm = _row_tile(h2d.shape[0], 512)
    qkv = fused_matmul(h2d, w_in, gain=norm_g, tm=tm, tn=ATT_IN_PAD // 3, name="att_in_p")
    q_h = _heads(qkv[:, :D_MODEL], b, t, ATT_HEADS)
    gl_h = qkv[:, D_MODEL + 6 * KV_DIM:D_MODEL + 6 * KV_DIM + ATT_HEADS * N_BRANCH]
    gl_h = gl_h.reshape(b, t, ATT_HEADS, N_BRANCH).transpose(0, 2, 1, 3)

    n_pages = t // PAGE_SIZE
    page_ids = (jnp.arange(b, dtype=jnp.int32)[:, None] * n_pages + jnp.arange(n_pages, dtype=jnp.int32)[None, :])
    uf, us = cmp_project(qkv, page_ids, D_MODEL // (2 * KV_DIM), cmp_w1, "cmp_u_p")
    kvc = cmp_finish(uf, us, cmp_pos, cmp_w1, cmp_w2, "cmp_out_p")

    n_sel = t // SLC_BLOCK
    tq = 256
    o_c, sel_bias = cmp_attention(q_h, kvc, slopes, tq=tq, n_sel=n_sel, q_base=0, want_idx=False, name="cmp_attn_p")

    q_g = (q_h * (HEAD_DIM ** -0.5)).astype(BF16).reshape(b, KV_HEADS, GQA, t, HEAD_DIM)
    qa = jnp.concatenate([q_g, jnp.broadcast_to(sel_bias[:, :, None], (b, KV_HEADS, GQA, t, sel_bias.shape[-1])),
                          jnp.zeros((b, KV_HEADS, GQA, t, SEL_FEAT - HEAD_DIM - sel_bias.shape[-1]), BF16)], axis=-1)
    ks = _heads(qkv[:, D_MODEL + 2 * KV_DIM:D_MODEL + 3 * KV_DIM], b, t, KV_HEADS).astype(BF16)
    vs = _heads(qkv[:, D_MODEL + 3 * KV_DIM:D_MODEL + 4 * KV_DIM], b, t, KV_HEADS).astype(BF16)
    onehot = jax.nn.one_hot(jnp.arange(t) // SLC_BLOCK, sel_bias.shape[-1], dtype=BF16)
    ka = jnp.concatenate([ks, jnp.broadcast_to(onehot, (b, KV_HEADS) + onehot.shape),
                          jnp.zeros((b, KV_HEADS, t, SEL_FEAT - HEAD_DIM - onehot.shape[-1]), BF16)], axis=-1)
    o_s = sel_attention_dense(qa, ka, vs, slopes, tq=tq, tk=512, name="sel_attn_p")

    front = ((0, 0), (0, 0), (WINDOW, 0), (0, 0))
    kw = jnp.pad(_heads(qkv[:, D_MODEL + 4 * KV_DIM:D_MODEL + 5 * KV_DIM], b, t, KV_HEADS).astype(BF16), front)
    vw = jnp.pad(_heads(qkv[:, D_MODEL + 5 * KV_DIM:D_MODEL + 6 * KV_DIM], b, t, KV_HEADS).astype(BF16), front)
    o = win_attention_merge(q_h, kw, vw, slopes, o_c, o_s, gl_h, tq=tq, k_valid_from=WINDOW, name="win_attn_p")
    o2d = o.transpose(0, 2, 1, 3).reshape(b * t, D_MODEL)
    h_new = fused_matmul(o2d, w_out, res=h2d, tm=tm, tn=512, name="att_out_p")
    win_rows = min(WINDOW, t)
    return (h_new, _kv_stack(qkv, b, t, D_MODEL), _kv_stack(qkv, b, t, D_MODEL + 2 * KV_DIM),
            _kv_stack(qkv, b, t, D_MODEL + 4 * KV_DIM)[:, t - win_rows:])


def nsa_sample_layer(h2d, b, t, cache_cmp, cache_slc, win_buf, page_table, norm_g, w_in, cmp_pos, cmp_w1, cmp_w2,
                     w_out, slopes):
    n_pages = page_table.shape[1]
    past_len = n_pages * PAGE_SIZE
    tq = 16
    assert t <= tq and t <= 2 * CMP_STRIDE - 1
    qkv = fused_matmul(h2d, w_in, gain=norm_g, tm=h2d.shape[0], tn=ATT_IN_PAD // 3, name="att_in_s")
    tpad = ((0, 0), (0, 0), (0, tq - t), (0, 0))
    q_h = jnp.pad(_heads(qkv[:, :D_MODEL], b, t, ATT_HEADS), tpad)
    gl_h = qkv[:, D_MODEL + 6 * KV_DIM:D_MODEL + 6 * KV_DIM + ATT_HEADS * N_BRANCH]
    gl_h = jnp.pad(gl_h.reshape(b, t, ATT_HEADS, N_BRANCH).transpose(0, 2, 1, 3), tpad)
    new_cmp = _kv_stack(qkv, b, t, D_MODEL)
    new_slc = _kv_stack(qkv, b, t, D_MODEL + 2 * KV_DIM)
    new_win = _kv_stack(qkv, b, t, D_MODEL + 4 * KV_DIM)

    cache2d = cache_cmp.reshape(-1, 2 * KV_DIM)
    uf, us = cmp_project(cache2d, page_table, 0, cmp_w1, "cmp_u_s")
    kvc = cmp_finish(uf, us, cmp_pos, cmp_w1, cmp_w2, "cmp_out_s")
    l_all = past_len + t
    n_sel = -(-l_all // SLC_BLOCK)
    o_c, idx = cmp_attention(q_h, kvc, slopes, tq=tq, n_sel=n_sel, q_base=past_len, want_idx=True, name="cmp_attn_s")

    new_blk = jnp.pad(new_slc.reshape(b, t, 2 * KV_HEADS, HEAD_DIM), ((0, 0), (0, SLC_BLOCK - t), (0, 0), (0, 0)))
    o_s = sel_attention_gather(q_h, idx[..., :t], page_table, cache_slc.reshape(-1, PAGE_SIZE, 2 * KV_HEADS, HEAD_DIM),
                               new_blk, slopes, n_tok=t, past_len=past_len, name="sel_attn_s")

    w_rows = win_buf.shape[1]
    win_all = jnp.concatenate([win_buf.astype(F32), new_win], axis=1)
    lk = tq + WINDOW
    shift = WINDOW - w_rows
    kvw = jnp.pad(win_all, ((0, 0), (shift, lk - shift - w_rows - t), (0, 0), (0, 0), (0, 0))).astype(BF16)
    kw = kvw[:, :, 0].transpose(0, 2, 1, 3)
    vw = kvw[:, :, 1].transpose(0, 2, 1, 3)
    o = win_attention_merge(q_h, kw, vw, slopes, o_c, o_s, gl_h, tq=tq, k_valid_from=shift, name="win_attn_s")
    o2d = o[:, :, :t].transpose(0, 2, 1, 3).reshape(b * t, D_MODEL)
    h_new = fused_matmul(o2d, w_out, res=h2d, tm=h2d.shape[0], tn=512, name="att_out_s")
    return h_new, new_cmp, new_slc, win_all[:, -w_rows:]


def ssm_layer(h2d, b, t, conv_state, ssm_state, norm_g, w_in, conv_prm, w_out, tag):
    t_pad = -(-t // SSD_CHUNK) * SSD_CHUNK
    tm = _row_tile(h2d.shape[0], 512)
    zx = fused_matmul(h2d, w_in, gain=norm_g, tm=tm, tn=1280, name="ssm_in_" + tag)
    xbc = zx[:, D_INNER:D_INNER + CONV_DIM].reshape(b, t, CONV_DIM)
    keep = CONV_WIDTH - 1
    conv_out = xbc[:, t - keep:] if t >= keep else jnp.concatenate([conv_state.astype(F32), xbc], axis=1)[:, -keep:]
    if t_pad != t:
        zx = jnp.pad(zx.reshape(b, t, SSM_IN_PAD), ((0, 0), (0, t_pad - t), (0, 0))).reshape(b * t_pad, SSM_IN_PAD)
    y, s_new = ssd_mixer(zx, conv_state, ssm_state, conv_prm, batch=b, t_pad=t_pad, t_valid=t, name="ssd_" + tag)
    if t_pad != t:
        y = y.reshape(b, t_pad, D_INNER)[:, :t].reshape(b * t, D_INNER)
    h_new = fused_matmul(y, w_out, res=h2d, tm=tm, tn=512, name="ssm_out_" + tag)
    return h_new, conv_out, s_new


def kernel(x_prompt, x_sample, state_ssm, state_conv, cache_cmp_kv, cache_slc_kv, state_win_kv, page_table, ssm_norm, ssm_w_in, ssm_conv_w, ssm_conv_b, ssm_dt_bias, ssm_a_log, ssm_d, ssm_gate_norm, ssm_w_out, att_norm, att_w_in, att_cmp_pos, att_cmp_w1, att_cmp_w2, att_w_out, mlp_norm, mlp_w_up, mlp_w_down, final_norm):
    bp, tp, _ = x_prompt.shape
    bs, ts, _ = x_sample.shape
    depth = mlp_norm.shape[0]
    slopes = alibi_slopes()
    hp = x_prompt.reshape(bp * tp, D_MODEL)
    hs = x_sample.reshape(bs * ts, D_MODEL)
    ssm_p, conv_p, ssm_s, conv_s = [], [], [], []
    cmp_p, slc_p, win_p, cmp_s, slc_s, win_s = [], [], [], [], [], []
    for i in range(depth):
        j = i // 2
        if i % 2 == 0:
            w_in = jnp.pad(ssm_w_in[j], ((0, 0), (0, SSM_IN_PAD - ssm_w_in.shape[-1]))).astype(BF16)
            w_out = ssm_w_out[j].astype(BF16)
            prm = (ssm_conv_w[j], ssm_conv_b[j], ssm_dt_bias[j], ssm_a_log[j], ssm_d[j], ssm_gate_norm[j])
            zc = jnp.zeros((bp, CONV_WIDTH - 1, CONV_DIM), F32)
            zs = jnp.zeros((bp, SSM_HEADS, SSM_HEAD_DIM, SSM_STATE), F32)
            hp, cp, sp = ssm_layer(hp, bp, tp, zc, zs, ssm_norm[j], w_in, prm, w_out, "p")
            hs, cs, ss = ssm_layer(hs, bs, ts, state_conv[j], state_ssm[j], ssm_norm[j], w_in, prm, w_out, "s")
            conv_p.append(cp); ssm_p.append(sp); conv_s.append(cs); ssm_s.append(ss)
        else:
            w_in = jnp.pad(att_w_in[j], ((0, 0), (0, ATT_IN_PAD - att_w_in.shape[-1]))).astype(BF16)
            w_out = att_w_out[j].astype(BF16)
            prm = (att_norm[j], w_in, att_cmp_pos[j], att_cmp_w1[j], att_cmp_w2[j], w_out, slopes)
            hp, kp, lp, wp = nsa_prompt_layer(hp, bp, tp, *prm)
            hs, kq, lq, wq = nsa_sample_layer(hs, bs, ts, cache_cmp_kv[j], cache_slc_kv[j], state_win_kv[j],
                                              page_table, *prm)
            cmp_p.append(kp); slc_p.append(lp); win_p.append(wp)
            cmp_s.append(kq); slc_s.append(lq); win_s.append(wq)
        fin = final_norm if i == depth - 1 else None
        w_up = mlp_w_up[i].astype(BF16)
        w_down = mlp_w_down[i].astype(BF16)
        hp = mlp_block(hp, mlp_norm[i], w_up, w_down, fin, "mlp%d_p" % i)
        hs = mlp_block(hs, mlp_norm[i], w_up, w_down, fin, "mlp%d_s" % i)
    return (hp.reshape(bp, tp, D_MODEL), hs.reshape(bs, ts, D_MODEL),
            jnp.stack(ssm_p), jnp.stack(conv_p), jnp.stack(cmp_p), jnp.stack(slc_p), jnp.stack(win_p),
            jnp.stack(ssm_s), jnp.stack(conv_s), jnp.stack(cmp_s), jnp.stack(slc_s), jnp.stack(win_s))
```

```python
import functools

import numpy as np
import jax
import jax.numpy as jnp
from jax import lax
from jax.experimental import pallas as pl
from jax.experimental.pallas import tpu as pltpu

F32 = jnp.float32
BF16 = jnp.bfloat16

D_MODEL = 1024
D_INNER = 2048
SSM_HEADS = 32
SSM_HEAD_DIM = 64
SSM_GROUPS = 8
SSM_STATE = 128
CONV_WIDTH = 4
CONV_DIM = 4096
SSM_IN_PAD = 6400
SSD_CHUNK = 128
ATT_HEADS = 16
HEAD_DIM = 64
KV_HEADS = 4
GQA = ATT_HEADS // KV_HEADS
KV_DIM = KV_HEADS * HEAD_DIM
CMP_STRIDE = 16
CMP_HIDDEN = 128
SLC_BLOCK = 64
SLC_SHIFT = 6
SLC_TOP_N = 16
WINDOW = 512
N_BRANCH = 3
ATT_IN_PAD = 2688
PAGE_SIZE = 128
NORM_EPS = 1e-5
NEG_INF = -1e30
LANES = 128
VMEM_LIMIT = 56 * 1024 * 1024


def _cparams(sem):
    return pltpu.CompilerParams(dimension_semantics=sem, vmem_limit_bytes=VMEM_LIMIT)


def _dot(a, b):
    return jnp.dot(a, b, preferred_element_type=F32)


def _dot_nt(a, b):
    return lax.dot_general(a, b, (((1,), (1,)), ((), ())), preferred_element_type=F32)


def _split_dot(a, onehot, left=False, terms=3):
    acc = None
    rem = a
    for _ in range(terms):
        hi = rem.astype(BF16)
        part = _dot(onehot, hi) if left else _dot(hi, onehot)
        acc = part if acc is None else acc + part
        rem = rem - hi.astype(F32)
    return acc


def _silu(x):
    return x / (1.0 + jnp.exp(-x))


def _mm_kernel(*refs, has_norm, act, has_res, has_fnorm):
    it = iter(refs)
    x_ref = next(it)
    g_ref = next(it) if has_norm else None
    w_ref = next(it)
    res_ref = next(it) if has_res else None
    fg_ref = next(it) if has_fnorm else None
    o_ref = next(it)
    xn_ref = next(it) if has_norm else None
    if has_norm:
        @pl.when(pl.program_id(1) == 0)
        def _():
            x = x_ref[...].astype(F32)
            ms = jnp.mean(x * x, axis=-1, keepdims=True)
            xn_ref[...] = (x * lax.rsqrt(ms + NORM_EPS) * g_ref[...]).astype(BF16)
        a = xn_ref[...]
    else:
        a = x_ref[...].astype(BF16)
    y = _dot(a, w_ref[...])
    if act == "relu2":
        y = jnp.square(jnp.maximum(y, 0.0))
    if has_res:
        y = y + res_ref[...]
    if has_fnorm:
        ms = jnp.mean(y * y, axis=-1, keepdims=True)
        y = y * lax.rsqrt(ms + NORM_EPS) * fg_ref[...]
    o_ref[...] = y.astype(o_ref.dtype)


def fused_matmul(x, w, *, gain=None, res=None, fgain=None, act=None, out_dtype=F32, tm, tn, name):
    m, k = x.shape
    n = w.shape[1]
    assert m % tm == 0 and n % tn == 0 and w.shape[0] == k
    assert fgain is None or tn == n
    in_specs = [pl.BlockSpec((tm, k), lambda i, j: (i, 0))]
    args = [x]
    if gain is not None:
        in_specs.append(pl.BlockSpec((1, k), lambda i, j: (0, 0)))
        args.append(gain.reshape(1, k).astype(F32))
    in_specs.append(pl.BlockSpec((k, tn), lambda i, j: (0, j)))
    args.append(w)
    if res is not None:
        in_specs.append(pl.BlockSpec((tm, tn), lambda i, j: (i, j)))
        args.append(res)
    if fgain is not None:
        in_specs.append(pl.BlockSpec((1, n), lambda i, j: (0, 0)))
        args.append(fgain.reshape(1, n).astype(F32))
    scratch = [pltpu.VMEM((tm, k), BF16)] if gain is not None else []
    kern = functools.partial(_mm_kernel, has_norm=gain is not None, act=act, has_res=res is not None,
                             has_fnorm=fgain is not None)
    return pl.pallas_call(
        kern,
        grid=(m // tm, n // tn),
        in_specs=in_specs,
        out_specs=pl.BlockSpec((tm, tn), lambda i, j: (i, j)),
        out_shape=jax.ShapeDtypeStruct((m, n), out_dtype),
        scratch_shapes=scratch,
        compiler_params=_cparams(("parallel", "arbitrary")),
        name=name,
    )(*args)


def _row_tile(m, pref):
    for t in (pref, 1024, 512, 256, 128, 64, 32, 16, 8):
        if t <= pref and m % t == 0:
            return t
    raise ValueError(m)


def mlp_block(h, norm_g, w_up, w_down, final_g, name):
    tm = _row_tile(h.shape[0], 1024)
    a = fused_matmul(h, w_up, gain=norm_g, act="relu2", out_dtype=BF16, tm=tm, tn=1024, name=name + "_up")
    if final_g is None:
        return fused_matmul(a, w_down, res=h, tm=tm, tn=512, name=name + "_down")
    tm = _row_tile(h.shape[0], 512)
    return fused_matmul(a, w_down, res=h, fgain=final_g, tm=tm, tn=D_MODEL, name=name + "_down")


def _ssd_kernel(z_ref, x_ref, bc_ref, dt_ref, cw_ref, cb_ref, dtb_ref, alog_ref, dsk_ref, gn_ref,
                c0_ref, s0_ref, tri_ref, ex_ref,
                y_ref, sout_ref,
                ext_ref, st_ref, xs_ref, b_ref, c_ref, *, t_valid, n_chunks):
    c = pl.program_id(1)
    L = SSD_CHUNK

    @pl.when(c == 0)
    def _():
        ext_ref[0:8, :] = c0_ref[0]
        st_ref[...] = s0_ref[0].T

    @pl.when(c > 0)
    def _():
        ext_ref[0:8, :] = ext_ref[L:L + 8, :]

    ext_ref[8:L + 8, 0:D_INNER] = x_ref[...]
    ext_ref[8:L + 8, D_INNER:CONV_DIM] = bc_ref[...]

    cblk = 512
    for i in range(CONV_DIM // cblk):
        sl = slice(i * cblk, (i + 1) * cblk)
        acc = cb_ref[:, sl]
        for k in range(CONV_WIDTH):
            acc = acc + cw_ref[k:k + 1, sl] * ext_ref[5 + k:5 + k + L, sl]
        v = _silu(acc)
        if i < 4:
            xs_ref[:, sl] = v
        elif i < 6:
            b_ref[:, (i - 4) * cblk:(i - 3) * cblk] = v
        else:
            c_ref[:, (i - 6) * cblk:(i - 5) * cblk] = v

    dt_raw = dt_ref[...] + dtb_ref[...]
    dt = jnp.maximum(dt_raw, 0.0) + jnp.log1p(jnp.exp(-jnp.abs(dt_raw)))
    if t_valid % L:
        row = c * L + lax.broadcasted_iota(jnp.int32, (L, LANES), 0)
        dt = jnp.where(row < t_valid, dt, 0.0)
    da = dt * (-jnp.exp(alog_ref[...]))
    acs = _split_dot(da, tri_ref[...], left=True)
    acs_t = acs.T
    expd = _split_dot(jnp.concatenate([acs, dt], axis=0), ex_ref[...])
    a_exp = expd[0:L]
    dt_exp = expd[L:2 * L]

    li = lax.broadcasted_iota(jnp.int32, (L, L), 0)
    si = lax.broadcasted_iota(jnp.int32, (L, L), 1)
    causal = si <= li
    first_half = si < SSM_HEAD_DIM
    gw = D_INNER // SSM_GROUPS
    for g in range(SSM_GROUPS):
        gs = slice(g * gw, (g + 1) * gw)
        bg = b_ref[:, g * SSM_STATE:(g + 1) * SSM_STATE]
        cg = c_ref[:, g * SSM_STATE:(g + 1) * SSM_STATE].astype(BF16)
        cbm = _dot_nt(cg, bg.astype(BF16))
        ag = a_exp[:, gs]
        xg = xs_ref[:, gs]
        xdt = xg * dt_exp[:, gs]
        a_last = ag[L - 1:L, :]
        y_off = _dot(cg, st_ref[:, gs].astype(BF16)) * jnp.exp(ag)
        ys = []
        for p in range(2):
            h0 = 4 * g + 2 * p
            xp = xdt[:, p * LANES:(p + 1) * LANES].astype(BF16)
            outs = []
            for h in (h0, h0 + 1):
                seg = acs[:, h:h + 1] - acs_t[h:h + 1, :]
                lm = jnp.exp(jnp.where(causal, seg, NEG_INF))
                outs.append(_dot((cbm * lm).astype(BF16), xp))
            ys.append(jnp.where(first_half, outs[0], outs[1]))
        y = jnp.concatenate(ys, axis=1) + y_off
        xd = (xdt * jnp.exp(a_last - ag)).astype(BF16)
        st_ref[:, gs] = st_ref[:, gs] * jnp.exp(a_last) + _dot(bg.T.astype(BF16), xd)
        zg = z_ref[:, gs]
        yy = (y + dsk_ref[:, gs] * xg) * _silu(zg)
        ms = jnp.mean(yy * yy, axis=-1, keepdims=True)
        y_ref[:, gs] = (yy * lax.rsqrt(ms + NORM_EPS) * gn_ref[:, gs]).astype(y_ref.dtype)

    @pl.when(c == n_chunks - 1)
    def _():
        sout_ref[0] = st_ref[...].T


def ssd_mixer(zx, conv0, st0, prm, *, batch, t_pad, t_valid, name):
    n_chunks = t_pad // SSD_CHUNK
    L = SSD_CHUNK
    conv_w, conv_b, dt_bias, a_log, d_skip, gate_norm = prm
    lane_pad = lambda v: jnp.pad(v.astype(F32), (0, LANES - SSM_HEADS)).reshape(1, LANES)
    tri = jnp.asarray(np.tril(np.ones((L, L), np.float32)), BF16)
    ex_np = np.zeros((LANES, D_INNER), np.float32)
    ex_np[np.arange(D_INNER) // SSM_HEAD_DIM, np.arange(D_INNER)] = 1.0
    ex = jnp.asarray(ex_np, BF16)
    c0 = jnp.pad(conv0.astype(F32), ((0, 0), (8 - (CONV_WIDTH - 1), 0), (0, 0)))
    row = lambda b, c: (b * n_chunks + c)
    const = lambda shape: pl.BlockSpec(shape, lambda b, c: (0,) * len(shape))
    kern = functools.partial(_ssd_kernel, t_valid=t_valid, n_chunks=n_chunks)
    y, s_out = pl.pallas_call(
        kern,
        grid=(batch, n_chunks),
        in_specs=[
            pl.BlockSpec((L, D_INNER), lambda b, c: (row(b, c), 0)),
            pl.BlockSpec((L, D_INNER), lambda b, c: (row(b, c), 1)),
            pl.BlockSpec((L, D_INNER), lambda b, c: (row(b, c), 2)),
            pl.BlockSpec((L, LANES), lambda b, c: (row(b, c), (D_INNER + CONV_DIM) // LANES)),
            const((CONV_WIDTH, CONV_DIM)), const((1, CONV_DIM)), const((1, LANES)), const((1, LANES)),
            const((1, D_INNER)), const((1, D_INNER)),
            pl.BlockSpec((1, 8, CONV_DIM), lambda b, c: (b, 0, 0)),
            pl.BlockSpec((1, D_INNER, SSM_STATE), lambda b, c: (b, 0, 0)),
            const((L, L)), const((LANES, D_INNER)),
        ],
        out_specs=[
            pl.BlockSpec((L, D_INNER), lambda b, c: (row(b, c), 0)),
            pl.BlockSpec((1, D_INNER, SSM_STATE), lambda b, c: (b, 0, 0)),
        ],
        out_shape=[
            jax.ShapeDtypeStruct((batch * t_pad, D_INNER), BF16),
            jax.ShapeDtypeStruct((batch, D_INNER, SSM_STATE), F32),
        ],
        scratch_shapes=[
            pltpu.VMEM((L + 8, CONV_DIM), F32),
            pltpu.VMEM((SSM_STATE, D_INNER), F32),
            pltpu.VMEM((L, D_INNER), F32),
            pltpu.VMEM((L, SSM_GROUPS * SSM_STATE), F32),
            pltpu.VMEM((L, SSM_GROUPS * SSM_STATE), F32),
        ],
        compiler_params=_cparams(("parallel", "arbitrary")),
        name=name,
    )(zx, zx, zx, zx,
      conv_w.astype(F32).T, conv_b.astype(F32).reshape(1, CONV_DIM), lane_pad(dt_bias), lane_pad(a_log),
      jnp.repeat(d_skip.astype(F32), SSM_HEAD_DIM).reshape(1, D_INNER), gate_norm.astype(F32).reshape(1, D_INNER),
      c0, st0.astype(F32).reshape(batch, D_INNER, SSM_STATE), tri, ex)
    return y, s_out.reshape(batch, SSM_HEADS, SSM_HEAD_DIM, SSM_STATE)


CMP_PAGES = 16


def _cmp_u_kernel(pt_ref, *refs, feature_major):
    del pt_ref
    pages = refs[:CMP_PAGES]
    perm_ref, w_ref, uf_ref, us_ref, y_ref = refs[CMP_PAGES:]
    regroup = _dot_nt if feature_major else _dot
    for p, page in enumerate(pages):
        yp = regroup(perm_ref[...], page[...].astype(BF16))
        for j in range(CMP_STRIDE):
            y_ref[j, p * 8:(p + 1) * 8, :] = yp[j * 8:(j + 1) * 8, :]
    nq = 4
    accs = [None] * nq
    for j in range(CMP_STRIDE):
        xj = y_ref[j].astype(BF16)
        for q in range(nq):
            part = _dot(xj[:, q * LANES:(q + 1) * LANES], w_ref[q // 2, j])
            accs[q] = part if accs[q] is None else accs[q] + part
    for q in range(nq):
        for gl in range(2):
            base = gl * 2 * CMP_HIDDEN
            uf_ref[0, 2 * q + gl] = accs[q][:, base:base + CMP_HIDDEN]
            us_ref[0, 2 * q + gl] = accs[q][:, base + CMP_HIDDEN:base + 2 * CMP_HIDDEN]


def cmp_project(src2d, page_ids, col_block, w1, name, feature_major=False):
    batch, n_pages = page_ids.shape
    assert n_pages % CMP_PAGES == 0
    n_sub = n_pages * (PAGE_SIZE // CMP_STRIDE)
    rows = CMP_PAGES * (PAGE_SIZE // CMP_STRIDE)
    w1r = w1.astype(F32).reshape(2, 2 * CMP_STRIDE, HEAD_DIM, CMP_HIDDEN)
    wj = jnp.concatenate([w1r[:, :CMP_STRIDE], w1r[:, CMP_STRIDE:]], axis=-1)
    zero = jnp.zeros_like(wj)
    wpad = jnp.concatenate([jnp.concatenate([wj, zero], axis=-1), jnp.concatenate([zero, wj], axis=-1)], axis=-2)
    wpad = wpad.astype(BF16)
    perm = np.zeros((PAGE_SIZE, PAGE_SIZE), np.float32)
    for j in range(CMP_STRIDE):
        for s in range(PAGE_SIZE // CMP_STRIDE):
            perm[j * 8 + s, CMP_STRIDE * s + j] = 1.0
    perm = jnp.asarray(perm, BF16)

    def page_spec(p):
        shape = (4 * LANES, PAGE_SIZE) if feature_major else (PAGE_SIZE, 4 * LANES)
        return pl.BlockSpec(shape, lambda b, c, pt: (pt[b, c * CMP_PAGES + p], col_block))

    u_shape = jax.ShapeDtypeStruct((batch, 2 * KV_HEADS, n_sub, CMP_HIDDEN), F32)
    u_spec = pl.BlockSpec((1, 2 * KV_HEADS, rows, CMP_HIDDEN), lambda b, c, pt: (b, 0, c, 0))
    return pl.pallas_call(
        functools.partial(_cmp_u_kernel, feature_major=feature_major),
        grid_spec=pltpu.PrefetchScalarGridSpec(
            num_scalar_prefetch=1,
            grid=(batch, n_pages // CMP_PAGES),
            in_specs=[page_spec(p) for p in range(CMP_PAGES)] + [
                pl.BlockSpec((PAGE_SIZE, PAGE_SIZE), lambda b, c, pt: (0, 0)),
                pl.BlockSpec((2, CMP_STRIDE, LANES, 4 * LANES), lambda b, c, pt: (0, 0, 0, 0)),
            ],
            out_specs=[u_spec, u_spec],
            scratch_shapes=[pltpu.VMEM((CMP_STRIDE, rows, 4 * LANES), F32)],
        ),
        out_shape=[u_shape, u_shape],
        compiler_params=_cparams(("parallel", "arbitrary")),
        name=name,
    )(page_ids, *([src2d] * CMP_PAGES), perm, wpad)


def _cmp_out_kernel(uf_ref, us_ref, pos_ref, w1_ref, w2_ref, o_ref, sh_ref, *, n_sub):
    sh_ref[0:n_sub, :] = us_ref[0, 0]
    sh_ref[n_sub:n_sub + 8, :] = jnp.zeros((8, CMP_HIDDEN), F32)
    pos_term = _dot(pos_ref[0].astype(BF16), w1_ref[0])[0:1, :]
    hid = jax.nn.gelu(uf_ref[0, 0] + sh_ref[pl.ds(1, n_sub), :] + pos_term)
    out = _dot(hid.astype(BF16), w2_ref[0])
    rows = lax.broadcasted_iota(jnp.int32, out.shape, 0)
    o_ref[0, 0] = jnp.where(rows < n_sub - 1, out, 0.0).astype(o_ref.dtype)


def cmp_finish(uf, us, pos_emb, w1, w2, name):
    batch, nkvg, n_sub, _ = uf.shape
    flat = 2 * CMP_STRIDE * HEAD_DIM
    pos = jnp.pad(pos_emb.astype(F32).reshape(2, 1, flat), ((0, 0), (0, 7), (0, 0)))
    u_spec = pl.BlockSpec((1, 1, n_sub, CMP_HIDDEN), lambda b, k: (b, k, 0, 0))
    return pl.pallas_call(
        functools.partial(_cmp_out_kernel, n_sub=n_sub),
        grid=(batch, nkvg),
        in_specs=[u_spec, u_spec,
                  pl.BlockSpec((1, 8, flat), lambda b, k: (k // KV_HEADS, 0, 0)),
                  pl.BlockSpec((1, flat, CMP_HIDDEN), lambda b, k: (k // KV_HEADS, 0, 0)),
                  pl.BlockSpec((1, CMP_HIDDEN, HEAD_DIM), lambda b, k: (k // KV_HEADS, 0, 0))],
        out_specs=pl.BlockSpec((1, 1, n_sub, HEAD_DIM), lambda b, k: (b, k, 0, 0)),
        out_shape=jax.ShapeDtypeStruct((batch, nkvg, n_sub, HEAD_DIM), BF16),
        scratch_shapes=[pltpu.VMEM((n_sub + 8, CMP_HIDDEN), F32)],
        compiler_params=_cparams(("parallel", "parallel")),
        name=name,
    )(uf, us, pos, w1.astype(BF16), w2.astype(BF16))


def _cmp_attn_kernel(q_ref, kc_ref, vc_ref, slope_ref, wsel_ref, o_ref, sel_ref, *rest, tq, ncp, nblk_pad, n_sel, q_base,
                     want_idx):
    qi = pl.program_id(2)
    use_ref = None if want_idx else rest[0]
    tpos = q_base + qi * tq + lax.broadcasted_iota(jnp.int32, (tq, 1), 0)
    c_end = CMP_STRIDE * lax.broadcasted_iota(jnp.int32, (1, ncp), 1) + (2 * CMP_STRIDE - 1)
    d = (tpos - c_end).astype(F32)
    ok = d >= 0.0
    any_ok = (tpos >= 2 * CMP_STRIDE - 1).astype(F32)
    kc = kc_ref[0, 0]
    vc = vc_ref[0, 0]
    imp = None
    for h in range(GQA):
        q = (q_ref[0, h] * (HEAD_DIM ** -0.5)).astype(BF16)
        s = jnp.where(ok, _dot_nt(q, kc) - slope_ref[0, h] * d, NEG_INF)
        e = jnp.exp(s - jnp.max(s, axis=-1, keepdims=True))
        p = e * (any_ok / jnp.sum(e, axis=-1, keepdims=True))
        o_ref[0, h] = _dot(p.astype(BF16), vc)
        imp = p if imp is None else imp + p
    p_slc = _split_dot(imp, wsel_ref[...], terms=2)
    tq_l = max(tq, LANES)
    if tq < LANES:
        p_slc = jnp.concatenate([p_slc, jnp.zeros((tq_l - tq, nblk_pad), F32)], axis=0)
    sc = p_slc.T
    blk = lax.broadcasted_iota(jnp.int32, (nblk_pad, tq_l), 0)
    tp = q_base + qi * tq + lax.broadcasted_iota(jnp.int32, (nblk_pad, tq_l), 1)
    cur = jnp.right_shift(tp, SLC_SHIFT)
    forced = (blk == 0) | (blk == cur) | (blk == cur - 1)
    reach = blk * SLC_BLOCK <= tp
    sc = jnp.where(forced, 1e30, jnp.where(reach, sc, -1.0))
    sc = jnp.where(blk < n_sel, sc, -3.0)
    chosen = jnp.zeros((nblk_pad, tq_l), F32)
    idx_rows = []
    for _ in range(min(SLC_TOP_N, n_sel)):
        mx = jnp.max(sc, axis=0, keepdims=True)
        idx = jnp.min(jnp.where(sc == mx, blk, nblk_pad), axis=0, keepdims=True)
        hit = blk == idx
        chosen = jnp.where(hit, 1.0, chosen)
        sc = jnp.where(hit, -2.0, sc)
        idx_rows.append(idx)
    if want_idx:
        sel_ref[0, 0] = jnp.concatenate(idx_rows, axis=0)
    else:
        chosen_t = chosen.T[0:tq]
        sel_ref[0, 0] = ((chosen_t - 1.0) * 1e30).astype(sel_ref.dtype)
        used = jnp.max(chosen_t, axis=0, keepdims=True)
        use_ref[0, 0, 0] = jnp.broadcast_to(used, (8, nblk_pad))


def cmp_attention(q_h, kvc, slopes, *, tq, n_sel, q_base, want_idx, name):
    b, _, t, _ = q_h.shape
    ncp = kvc.shape[2]
    nblk_pad = -(-n_sel // LANES) * LANES
    c = np.arange(ncp)[:, None]
    j = np.arange(nblk_pad)[None, :]
    wsel = ((c // 4 == j).astype(np.float32) + ((c + 1) // 4 == j).astype(np.float32)) * (c < ncp - 1)
    tq_l = max(tq, LANES)
    if want_idx:
        sel_shape = jax.ShapeDtypeStruct((b, KV_HEADS, SLC_TOP_N, (t // tq) * tq_l), jnp.int32)
        sel_spec = pl.BlockSpec((1, 1, SLC_TOP_N, tq_l), lambda bi, g, qi: (bi, g, 0, qi))
    else:
        sel_shape = jax.ShapeDtypeStruct((b, KV_HEADS, t, nblk_pad), BF16)
        sel_spec = pl.BlockSpec((1, 1, tq, nblk_pad), lambda bi, g, qi: (bi, g, qi, 0))
    kern = functools.partial(_cmp_attn_kernel, tq=tq, ncp=ncp, nblk_pad=nblk_pad, n_sel=n_sel, q_base=q_base,
                             want_idx=want_idx)
    out_specs = [pl.BlockSpec((1, GQA, tq, HEAD_DIM), lambda bi, g, qi: (bi, g, qi, 0)), sel_spec]
    out_shape = [jax.ShapeDtypeStruct((b, ATT_HEADS, t, HEAD_DIM), F32), sel_shape]
    if not want_idx:
        out_specs.append(pl.BlockSpec((1, 1, 1, 8, nblk_pad), lambda bi, g, qi: (bi, g, qi, 0, 0)))
        out_shape.append(jax.ShapeDtypeStruct((b, KV_HEADS, t // tq, 8, nblk_pad), F32))
    return pl.pallas_call(
        kern,
        grid=(b, KV_HEADS, t // tq),
        in_specs=[
            pl.BlockSpec((1, GQA, tq, HEAD_DIM), lambda bi, g, qi: (bi, g, qi, 0)),
            pl.BlockSpec((1, 1, ncp, HEAD_DIM), lambda bi, g, qi: (bi, g, 0, 0)),
            pl.BlockSpec((1, 1, ncp, HEAD_DIM), lambda bi, g, qi: (bi, KV_HEADS + g, 0, 0)),
            pl.BlockSpec((1, GQA, 1, 1), lambda bi, g, qi: (g, 0, 0, 0)),
            pl.BlockSpec((ncp, nblk_pad), lambda bi, g, qi: (0, 0)),
        ],
        out_specs=out_specs,
        out_shape=out_shape,
        compiler_params=_cparams(("parallel", "parallel", "parallel")),
        name=name,
    )(q_h, kvc, kvc, slopes.reshape(KV_HEADS, GQA, 1, 1), jnp.asarray(wsel, BF16))


SEL_FEAT = 256


def _sel_kernel(use_ref, qa_ref, ka_ref, v_ref, slope_ref, o_ref, m_ref, acc_ref, *, tq):
    bi = pl.program_id(0)
    g = pl.program_id(1)
    qi = pl.program_id(2)
    n_qt = pl.num_programs(2)
    m_ref[...] = jnp.full(m_ref.shape, NEG_INF, F32)
    acc_ref[...] = jnp.zeros(acc_ref.shape, F32)
    rows = lax.broadcasted_iota(jnp.int32, (tq, tq), 0)
    cols = lax.broadcasted_iota(jnp.int32, (tq, tq), 1)
    lane = lax.broadcasted_iota(jnp.int32, (1, tq), 1)

    def chunk(j, diagonal):
        k0 = pl.multiple_of(j * tq, tq)
        ka = ka_ref[0, 0, pl.ds(k0, tq), :]
        va = v_ref[0, 0, pl.ds(k0, tq), :]
        rel = (lane + (j - qi) * tq).astype(F32)
        for h in range(GQA):
            s = _dot_nt(qa_ref[0, 0, h], ka) + slope_ref[0, h] * rel
            if diagonal:
                s = jnp.where(cols <= rows, s, NEG_INF)
            m_prev = m_ref[h]
            m_new = jnp.maximum(m_prev, jnp.max(s, axis=-1, keepdims=True))
            p = jnp.exp(s - jnp.concatenate([m_new] * (tq // LANES), axis=1))
            acc_ref[h] = jnp.exp(m_prev - m_new) * acc_ref[h] + _dot(p.astype(BF16), va)
            m_ref[h] = m_new

    def body(j, carry):
        @pl.when(use_ref[((bi * KV_HEADS + g) * n_qt + qi) * n_qt + j] > 0)
        def _():
            chunk(j, False)
        return carry

    lax.fori_loop(0, qi, body, 0)
    chunk(qi, True)
    for h in range(GQA):
        acc = acc_ref[h]
        o_ref[0, h] = acc[:, 0:HEAD_DIM] / acc[:, HEAD_DIM:HEAD_DIM + 1]


def sel_attention_dense(chunk_used, qa, ka, va, slopes, *, tq, name):
    b, _, _, t, _ = qa.shape
    return pl.pallas_call(
        functools.partial(_sel_kernel, tq=tq),
        grid_spec=pltpu.PrefetchScalarGridSpec(
            num_scalar_prefetch=1,
            grid=(b, KV_HEADS, t // tq),
            in_specs=[
                pl.BlockSpec((1, 1, GQA, tq, SEL_FEAT), lambda bi, g, qi, *_: (bi, g, 0, qi, 0)),
                pl.BlockSpec((1, 1, t, SEL_FEAT), lambda bi, g, qi, *_: (bi, g, 0, 0)),
                pl.BlockSpec((1, 1, t, LANES), lambda bi, g, qi, *_: (bi, g, 0, 0)),
                pl.BlockSpec((1, GQA, 1, 1), lambda bi, g, qi, *_: (g, 0, 0, 0)),
            ],
            out_specs=pl.BlockSpec((1, GQA, tq, HEAD_DIM), lambda bi, g, qi, *_: (bi, g, qi, 0)),
            scratch_shapes=[pltpu.VMEM((GQA, tq, LANES), F32), pltpu.VMEM((GQA, tq, LANES), F32)],
        ),
        out_shape=jax.ShapeDtypeStruct((b, ATT_HEADS, t, HEAD_DIM), F32),
        compiler_params=_cparams(("parallel", "parallel", "arbitrary")),
        name=name,
    )(chunk_used.reshape(-1), qa, ka, va, slopes.reshape(KV_HEADS, GQA, 1, 1))


def _sel_gather_kernel(idx_ref, pt_ref, q_ref, slope_ref, cache_ref, new_ref, o_ref, kbuf, vbuf, sem, *,
                       n_tok, tq, past_len, n_steps):
    b = pl.program_id(0)
    g = pl.program_id(1)
    step = b * KV_HEADS + g
    slot = step % 2
    new_blk = past_len // SLC_BLOCK
    halves = PAGE_SIZE // SLC_BLOCK

    def block_copy(bb, gg, sl, t, n, kv, blk, newest):
        buf = vbuf if kv else kbuf
        dst = buf.at[sl, t, :, pl.ds(n * PAGE_SIZE, PAGE_SIZE)]
        col = kv * KV_HEADS + gg
        if newest:
            src = new_ref.at[bb, col]
        else:
            page = pt_ref[bb * (past_len // PAGE_SIZE) + jnp.minimum(blk, new_blk - 1) // halves]
            src = cache_ref.at[page, col]
        return pltpu.make_async_copy(src, dst, sem.at[sl])

    def for_each_block(bb, gg, fn):
        for t in range(n_tok):
            for n in range(SLC_TOP_N):
                blk = idx_ref[((bb * KV_HEADS + gg) * SLC_TOP_N + n) * n_tok + t]
                for kv in range(2):
                    fn(t, n, kv, blk)

    def start(bb, gg, sl):
        def one(t, n, kv, blk):
            @pl.when(blk >= new_blk)
            def _():
                block_copy(bb, gg, sl, t, n, kv, blk, True).start()

            @pl.when(blk < new_blk)
            def _():
                block_copy(bb, gg, sl, t, n, kv, blk, False).start()
        for_each_block(bb, gg, one)

    @pl.when(step == 0)
    def _():
        start(b, g, slot)

    @pl.when(step + 1 < n_steps)
    def _():
        nxt = step + 1
        start(nxt // KV_HEADS, nxt % KV_HEADS, 1 - slot)

    for_each_block(b, g, lambda t, n, kv, blk: block_copy(b, g, slot, t, n, kv, blk, False).wait())

    r = GQA * tq
    q = (q_ref[0].reshape(r, HEAD_DIM) * (HEAD_DIM ** -0.5)).astype(BF16)
    nk = SLC_TOP_N * PAGE_SIZE
    lane = lax.broadcasted_iota(jnp.int32, (1, nk), 1)
    trow = lax.broadcasted_iota(jnp.int32, (tq, 1), 0)
    tpos = past_len + trow
    acc = jnp.zeros((GQA, tq, HEAD_DIM), F32)
    in_page = jnp.bitwise_and(lane, PAGE_SIZE - 1)
    half_of_lane = jnp.right_shift(in_page, SLC_SHIFT)
    slot_of_lane = jnp.right_shift(lane, SLC_SHIFT + 1)
    for t in range(n_tok):
        kpos = in_page
        chosen_half = jnp.zeros((1, nk), jnp.int32)
        for n in range(SLC_TOP_N):
            blk = idx_ref[((b * KV_HEADS + g) * SLC_TOP_N + n) * n_tok + t]
            here = slot_of_lane == n
            kpos = jnp.where(here, kpos + (blk // halves) * PAGE_SIZE, kpos)
            chosen_half = jnp.where(here, blk % halves, chosen_half)
        d = (tpos - kpos).astype(F32)
        ok = (d >= 0.0) & (half_of_lane == chosen_half)
        s = _dot(q, kbuf[slot, t].astype(BF16))
        s3 = s.reshape(GQA, tq, nk) - slope_ref[0] * d[None]
        s3 = jnp.where(ok[None], s3, NEG_INF)
        m = jnp.max(s3, axis=-1, keepdims=True)
        e = jnp.exp(s3 - m)
        p = e / jnp.sum(e, axis=-1, keepdims=True)
        o = _dot_nt(p.reshape(r, nk).astype(BF16), vbuf[slot, t].astype(BF16)).reshape(GQA, tq, HEAD_DIM)
        acc = jnp.where((trow == t)[None], o, acc)
    o_ref[0] = acc


def sel_attention_gather(q_h, idx, page_table, cache, new_kv, slopes, *, n_tok, past_len, name):
    b, _, tq, _ = q_h.shape
    n_steps = b * KV_HEADS
    nk = SLC_TOP_N * PAGE_SIZE
    kern = functools.partial(_sel_gather_kernel, n_tok=n_tok, tq=tq, past_len=past_len, n_steps=n_steps)
    return pl.pallas_call(
        kern,
        grid_spec=pltpu.PrefetchScalarGridSpec(
            num_scalar_prefetch=2,
            grid=(b, KV_HEADS),
            in_specs=[
                pl.BlockSpec((1, GQA, tq, HEAD_DIM), lambda bi, g, *_: (bi, g, 0, 0)),
                pl.BlockSpec((1, GQA, 1, 1), lambda bi, g, *_: (g, 0, 0, 0)),
                pl.BlockSpec(memory_space=pl.ANY),
                pl.BlockSpec(memory_space=pl.ANY),
            ],
            out_specs=pl.BlockSpec((1, GQA, tq, HEAD_DIM), lambda bi, g, *_: (bi, g, 0, 0)),
            scratch_shapes=[pltpu.VMEM((2, n_tok, HEAD_DIM, nk), F32), pltpu.VMEM((2, n_tok, HEAD_DIM, nk), F32),
                            pltpu.SemaphoreType.DMA((2,))],
        ),
        out_shape=jax.ShapeDtypeStruct((b, ATT_HEADS, tq, HEAD_DIM), F32),
        compiler_params=_cparams(("arbitrary", "arbitrary")),
        name=name,
    )(idx.reshape(-1), page_table.reshape(-1), q_h, slopes.reshape(KV_HEADS, GQA, 1, 1), cache, new_kv)


def _win_kernel(q_ref, k_ref, v_ref, slope_ref, oc_ref, os_ref, gl_ref, o_ref, *, tq, nk, k_valid_from):
    qi = pl.program_id(2)
    r0 = pl.multiple_of(qi * tq, tq)
    kw = k_ref[0, 0, pl.ds(r0, nk), :]
    vw = v_ref[0, 0, pl.ds(r0, nk), :]
    qrow = r0 + WINDOW + lax.broadcasted_iota(jnp.int32, (tq, 1), 0)
    krow = r0 + lax.broadcasted_iota(jnp.int32, (1, nk), 1)
    di = qrow - krow
    ok = (di >= 0) & (di < WINDOW) & (krow >= k_valid_from)
    dist = di.astype(F32)
    for h in range(GQA):
        q = (q_ref[0, h] * (HEAD_DIM ** -0.5)).astype(BF16)
        s = jnp.where(ok, _dot_nt(q, kw) - slope_ref[0, h] * dist, NEG_INF)
        e = jnp.exp(s - jnp.max(s, axis=-1, keepdims=True))
        p = e * (1.0 / jnp.sum(e, axis=-1, keepdims=True))
        ow = _dot(p.astype(BF16), vw)
        gates = 1.0 / (1.0 + jnp.exp(-gl_ref[0, h]))
        o_ref[0, h] = gates[:, 0:1] * oc_ref[0, h] + gates[:, 1:2] * os_ref[0, h] + gates[:, 2:3] * ow


def win_attention_merge(q_h, kw, vw, slopes, o_c, o_s, gate_logits, *, tq, k_valid_from, name):
    b, _, t, _ = q_h.shape
    nk = tq + WINDOW
    lk = kw.shape[2]
    assert lk >= t + WINDOW
    head_spec = lambda w: pl.BlockSpec((1, GQA, tq, w), lambda bi, g, qi: (bi, g, qi, 0))
    kv_spec = pl.BlockSpec((1, 1, lk, HEAD_DIM), lambda bi, g, qi: (bi, g, 0, 0))
    return pl.pallas_call(
        functools.partial(_win_kernel, tq=tq, nk=nk, k_valid_from=k_valid_from),
        grid=(b, KV_HEADS, t // tq),
        in_specs=[head_spec(HEAD_DIM), kv_spec, kv_spec,
                  pl.BlockSpec((1, GQA, 1, 1), lambda bi, g, qi: (g, 0, 0, 0)),
                  head_spec(HEAD_DIM), head_spec(HEAD_DIM), head_spec(N_BRANCH)],
        out_specs=head_spec(HEAD_DIM),
        out_shape=jax.ShapeDtypeStruct((b, ATT_HEADS, t, HEAD_DIM), F32),
        compiler_params=_cparams(("parallel", "parallel", "parallel")),
        name=name,
    )(q_h, kw, vw, slopes.reshape(KV_HEADS, GQA, 1, 1), o_c, o_s, gate_logits)


def _heads(x2d, b, t, n_heads):
    return x2d.reshape(b, t, n_heads, HEAD_DIM).transpose(0, 2, 1, 3)


def _kv_stack(qkv, b, t, col):
    return qkv[:, col:col + 2 * KV_DIM].reshape(b, t, 2, KV_HEADS, HEAD_DIM)


def alibi_slopes():
    return 2.0 ** (-8.0 * jnp.arange(1, ATT_HEADS + 1, dtype=jnp.float32) / ATT_HEADS)


def nsa_prompt_layer(h2d, b, t, norm_g, w_in, cmp_pos, cmp_w1, cmp_w2, w_out, slopes):
    tm = _row_tile(h2d.shape[0], 1024)
    qkv = fused_matmul(h2d, w_in, gain=norm_g, tm=tm, tn=ATT_IN_PAD // 3, name="att_in_p")
    q_h = _heads(qkv[:, :D_MODEL], b, t, ATT_HEADS)
    gl_h = qkv[:, D_MODEL + 6 * KV_DIM:D_MODEL + 6 * KV_DIM + ATT_HEADS * N_BRANCH]
    gl_h = gl_h.reshape(b, t, ATT_HEADS, N_BRANCH).transpose(0, 2, 1, 3)

    n_pages = t // PAGE_SIZE
    page_ids = (jnp.arange(b, dtype=jnp.int32)[:, None] * n_pages + jnp.arange(n_pages, dtype=jnp.int32)[None, :])
    uf, us = cmp_project(qkv, page_ids, D_MODEL // (2 * KV_DIM), cmp_w1, "cmp_u_p")
    kvc = cmp_finish(uf, us, cmp_pos, cmp_w1, cmp_w2, "cmp_out_p")

    n_sel = t // SLC_BLOCK
    tq = 256
    o_c, sel_bias, blk_used = cmp_attention(q_h, kvc, slopes, tq=tq, n_sel=n_sel, q_base=0, want_idx=False,
                                            name="cmp_attn_p")
    n_qt = t // tq
    chunk_used = blk_used[:, :, :, 0, :n_sel].reshape(b, KV_HEADS, n_qt, n_qt, tq // SLC_BLOCK).max(axis=-1)
    chunk_used = (chunk_used > 0.0).astype(jnp.int32)

    q_g = (q_h * (HEAD_DIM ** -0.5)).astype(BF16).reshape(b, KV_HEADS, GQA, t, HEAD_DIM)
    qa = jnp.concatenate([q_g, jnp.broadcast_to(sel_bias[:, :, None], (b, KV_HEADS, GQA, t, sel_bias.shape[-1])),
                          jnp.zeros((b, KV_HEADS, GQA, t, SEL_FEAT - HEAD_DIM - sel_bias.shape[-1]), BF16)], axis=-1)
    ks = _heads(qkv[:, D_MODEL + 2 * KV_DIM:D_MODEL + 3 * KV_DIM], b, t, KV_HEADS).astype(BF16)
    vs = _heads(qkv[:, D_MODEL + 3 * KV_DIM:D_MODEL + 4 * KV_DIM], b, t, KV_HEADS).astype(BF16)
    onehot = jax.nn.one_hot(jnp.arange(t) // SLC_BLOCK, sel_bias.shape[-1], dtype=BF16)
    ka = jnp.concatenate([ks, jnp.broadcast_to(onehot, (b, KV_HEADS) + onehot.shape),
                          jnp.zeros((b, KV_HEADS, t, SEL_FEAT - HEAD_DIM - onehot.shape[-1]), BF16)], axis=-1)
    va = jnp.concatenate([vs, jnp.ones((b, KV_HEADS, t, 1), BF16),
                          jnp.zeros((b, KV_HEADS, t, LANES - HEAD_DIM - 1), BF16)], axis=-1)
    o_s = sel_attention_dense(chunk_used, qa, ka, va, slopes, tq=tq, name="sel_attn_p")

    front = ((0, 0), (0, 0), (WINDOW, 0), (0, 0))
    kw = jnp.pad(_heads(qkv[:, D_MODEL + 4 * KV_DIM:D_MODEL + 5 * KV_DIM], b, t, KV_HEADS).astype(BF16), front)
    vw = jnp.pad(_heads(qkv[:, D_MODEL + 5 * KV_DIM:D_MODEL + 6 * KV_DIM], b, t, KV_HEADS).astype(BF16), front)
    o = win_attention_merge(q_h, kw, vw, slopes, o_c, o_s, gl_h, tq=tq, k_valid_from=WINDOW, name="win_attn_p")
    o2d = o.transpose(0, 2, 1, 3).reshape(b * t, D_MODEL)
    h_new = fused_matmul(o2d, w_out, res=h2d, tm=tm, tn=512, name="att_out_p")
    win_rows = min(WINDOW, t)
    return (h_new, _kv_stack(qkv, b, t, D_MODEL), _kv_stack(qkv, b, t, D_MODEL + 2 * KV_DIM),
            _kv_stack(qkv, b, t, D_MODEL + 4 * KV_DIM)[:, t - win_rows:])


def nsa_sample_layer(h2d, b, t, cache_cmp, cache_slc, win_buf, page_table, norm_g, w_in, cmp_pos, cmp_w1, cmp_w2,
                     w_out, slopes):
    n_pages = page_table.shape[1]
    past_len = n_pages * PAGE_SIZE
    tq = 16
    assert t <= tq and t <= 2 * CMP_STRIDE - 1
    qkv = fused_matmul(h2d, w_in, gain=norm_g, tm=h2d.shape[0], tn=ATT_IN_PAD // 3, name="att_in_s")
    tpad = ((0, 0), (0, 0), (0, tq - t), (0, 0))
    q_h = jnp.pad(_heads(qkv[:, :D_MODEL], b, t, ATT_HEADS), tpad)
    gl_h = qkv[:, D_MODEL + 6 * KV_DIM:D_MODEL + 6 * KV_DIM + ATT_HEADS * N_BRANCH]
    gl_h = jnp.pad(gl_h.reshape(b, t, ATT_HEADS, N_BRANCH).transpose(0, 2, 1, 3), tpad)
    new_cmp = _kv_stack(qkv, b, t, D_MODEL)
    new_slc = _kv_stack(qkv, b, t, D_MODEL + 2 * KV_DIM)
    new_win = _kv_stack(qkv, b, t, D_MODEL + 4 * KV_DIM)

    cmp_fm = cache_cmp.transpose(0, 2, 3, 4, 1).reshape(-1, PAGE_SIZE)
    slc_fm = cache_slc.transpose(0, 2, 3, 4, 1).reshape(-1, 2 * KV_HEADS, HEAD_DIM, PAGE_SIZE)
    uf, us = cmp_project(cmp_fm, page_table, 0, cmp_w1, "cmp_u_s", feature_major=True)
    kvc = cmp_finish(uf, us, cmp_pos, cmp_w1, cmp_w2, "cmp_out_s")
    l_all = past_len + t
    n_sel = -(-l_all // SLC_BLOCK)
    o_c, idx = cmp_attention(q_h, kvc, slopes, tq=tq, n_sel=n_sel, q_base=past_len, want_idx=True, name="cmp_attn_s")

    new_page = jnp.pad(new_slc.reshape(b, t, 2 * KV_HEADS, HEAD_DIM), ((0, 0), (0, PAGE_SIZE - t), (0, 0), (0, 0)))
    o_s = sel_attention_gather(q_h, idx[..., :t], page_table, slc_fm, new_page.transpose(0, 2, 3, 1), slopes,
                               n_tok=t, past_len=past_len, name="sel_attn_s")

    w_rows = win_buf.shape[1]
    win_all = jnp.concatenate([win_buf.astype(F32), new_win], axis=1)
    lk = tq + WINDOW
    shift = WINDOW - w_rows
    kvw = jnp.pad(win_all, ((0, 0), (shift, lk - shift - w_rows - t), (0, 0), (0, 0), (0, 0))).astype(BF16)
    kw = kvw[:, :, 0].transpose(0, 2, 1, 3)
    vw = kvw[:, :, 1].transpose(0, 2, 1, 3)
    o = win_attention_merge(q_h, kw, vw, slopes, o_c, o_s, gl_h, tq=tq, k_valid_from=shift, name="win_attn_s")
    o2d = o[:, :, :t].transpose(0, 2, 1, 3).reshape(b * t, D_MODEL)
    h_new = fused_matmul(o2d, w_out, res=h2d, tm=h2d.shape[0], tn=512, name="att_out_s")
    return h_new, new_cmp, new_slc, win_all[:, -w_rows:]


def ssm_layer(h2d, b, t, conv_state, ssm_state, norm_g, w_in, conv_prm, w_out, tag):
    t_pad = -(-t // SSD_CHUNK) * SSD_CHUNK
    tm = _row_tile(h2d.shape[0], 1024)
    zx = fused_matmul(h2d, w_in, gain=norm_g, tm=tm, tn=1280, name="ssm_in_" + tag)
    xbc = zx[:, D_INNER:D_INNER + CONV_DIM].reshape(b, t, CONV_DIM)
    keep = CONV_WIDTH - 1
    conv_out = xbc[:, t - keep:] if t >= keep else jnp.concatenate([conv_state.astype(F32), xbc], axis=1)[:, -keep:]
    if t_pad != t:
        zx = jnp.pad(zx.reshape(b, t, SSM_IN_PAD), ((0, 0), (0, t_pad - t), (0, 0))).reshape(b * t_pad, SSM_IN_PAD)
    y, s_new = ssd_mixer(zx, conv_state, ssm_state, conv_prm, batch=b, t_pad=t_pad, t_valid=t, name="ssd_" + tag)
    if t_pad != t:
        y = y.reshape(b, t_pad, D_INNER)[:, :t].reshape(b * t, D_INNER)
    h_new = fused_matmul(y, w_out, res=h2d, tm=tm, tn=512, name="ssm_out_" + tag)
    return h_new, conv_out, s_new


def kernel(x_prompt, x_sample, state_ssm, state_conv, cache_cmp_kv, cache_slc_kv, state_win_kv, page_table, ssm_norm, ssm_w_in, ssm_conv_w, ssm_conv_b, ssm_dt_bias, ssm_a_log, ssm_d, ssm_gate_norm, ssm_w_out, att_norm, att_w_in, att_cmp_pos, att_cmp_w1, att_cmp_w2, att_w_out, mlp_norm, mlp_w_up, mlp_w_down, final_norm):
    bp, tp, _ = x_prompt.shape
    bs, ts, _ = x_sample.shape
    depth = mlp_norm.shape[0]
    slopes = alibi_slopes()
    hp = x_prompt.reshape(bp * tp, D_MODEL)
    hs = x_sample.reshape(bs * ts, D_MODEL)
    ssm_p, conv_p, ssm_s, conv_s = [], [], [], []
    cmp_p, slc_p, win_p, cmp_s, slc_s, win_s = [], [], [], [], [], []
    for i in range(depth):
        j = i // 2
        if i % 2 == 0:
            w_in = jnp.pad(ssm_w_in[j], ((0, 0), (0, SSM_IN_PAD - ssm_w_in.shape[-1]))).astype(BF16)
            w_out = ssm_w_out[j].astype(BF16)
            prm = (ssm_conv_w[j], ssm_conv_b[j], ssm_dt_bias[j], ssm_a_log[j], ssm_d[j], ssm_gate_norm[j])
            zc = jnp.zeros((bp, CONV_WIDTH - 1, CONV_DIM), F32)
            zs = jnp.zeros((bp, SSM_HEADS, SSM_HEAD_DIM, SSM_STATE), F32)
            hp, cp, sp = ssm_layer(hp, bp, tp, zc, zs, ssm_norm[j], w_in, prm, w_out, "p")
            hs, cs, ss = ssm_layer(hs, bs, ts, state_conv[j], state_ssm[j], ssm_norm[j], w_in, prm, w_out, "s")
            conv_p.append(cp); ssm_p.append(sp); conv_s.append(cs); ssm_s.append(ss)
        else:
            w_in = jnp.pad(att_w_in[j], ((0, 0), (0, ATT_IN_PAD - att_w_in.shape[-1]))).astype(BF16)
            w_out = att_w_out[j].astype(BF16)
            prm = (att_norm[j], w_in, att_cmp_pos[j], att_cmp_w1[j], att_cmp_w2[j], w_out, slopes)
            hp, kp, lp, wp = nsa_prompt_layer(hp, bp, tp, *prm)
            hs, kq, lq, wq = nsa_sample_layer(hs, bs, ts, cache_cmp_kv[j], cache_slc_kv[j], state_win_kv[j],
                                              page_table, *prm)
            cmp_p.append(kp); slc_p.append(lp); win_p.append(wp)
            cmp_s.append(kq); slc_s.append(lq); win_s.append(wq)
        fin = final_norm if i == depth - 1 else None
        w_up = mlp_w_up[i].astype(BF16)
        w_down = mlp_w_down[i].astype(BF16)
        hp = mlp_block(hp, mlp_norm[i], w_up, w_down, fin, "mlp%d_p" % i)
        hs = mlp_block(hs, mlp_norm[i], w_up, w_down, fin, "mlp%d_s" % i)
    return (hp.reshape(bp, tp, D_MODEL), hs.reshape(bs, ts, D_MODEL),
            jnp.stack(ssm_p), jnp.stack(conv_p), jnp.stack(cmp_p), jnp.stack(slc_p), jnp.stack(win_p),
            jnp.stack(ssm_s), jnp.stack(conv_s), jnp.stack(cmp_s), jnp.stack(slc_s), jnp.stack(win_s))
```

```python
import functools

import numpy as np
import jax
import jax.numpy as jnp
from jax import lax
from jax.experimental import pallas as pl
from jax.experimental.pallas import tpu as pltpu

F32 = jnp.float32
BF16 = jnp.bfloat16

D_MODEL = 1024
D_INNER = 2048
SSM_HEADS = 32
SSM_HEAD_DIM = 64
SSM_GROUPS = 8
SSM_STATE = 128
CONV_WIDTH = 4
CONV_DIM = 4096
SSM_IN_PAD = 6400
SSD_CHUNK = 128
ATT_HEADS = 16
HEAD_DIM = 64
KV_HEADS = 4
GQA = ATT_HEADS // KV_HEADS
KV_DIM = KV_HEADS * HEAD_DIM
CMP_STRIDE = 16
CMP_HIDDEN = 128
SLC_BLOCK = 64
SLC_SHIFT = 6
SLC_TOP_N = 16
WINDOW = 512
N_BRANCH = 3
ATT_IN_PAD = 2688
PAGE_SIZE = 128
NORM_EPS = 1e-5
NEG_INF = -1e30
LANES = 128
VMEM_LIMIT = 56 * 1024 * 1024


def _cparams(sem):
    return pltpu.CompilerParams(dimension_semantics=sem, vmem_limit_bytes=VMEM_LIMIT)


def _dot(a, b):
    return jnp.dot(a, b, preferred_element_type=F32)


def _dot_nt(a, b):
    return lax.dot_general(a, b, (((1,), (1,)), ((), ())), preferred_element_type=F32)


def _split_dot(a, onehot, left=False, terms=3):
    acc = None
    rem = a
    for _ in range(terms):
        hi = rem.astype(BF16)
        part = _dot(onehot, hi) if left else _dot(hi, onehot)
        acc = part if acc is None else acc + part
        rem = rem - hi.astype(F32)
    return acc


def _silu(x):
    return x / (1.0 + jnp.exp(-x))


def _mm_kernel(*refs, has_norm, act, has_res, has_fnorm):
    it = iter(refs)
    x_ref = next(it)
    g_ref = next(it) if has_norm else None
    w_ref = next(it)
    res_ref = next(it) if has_res else None
    fg_ref = next(it) if has_fnorm else None
    o_ref = next(it)
    xn_ref = next(it) if has_norm else None
    if has_norm:
        @pl.when(pl.program_id(1) == 0)
        def _():
            x = x_ref[...].astype(F32)
            ms = jnp.mean(x * x, axis=-1, keepdims=True)
            xn_ref[...] = (x * lax.rsqrt(ms + NORM_EPS) * g_ref[...]).astype(BF16)
        a = xn_ref[...]
    else:
        a = x_ref[...].astype(BF16)
    y = _dot(a, w_ref[...])
    if act == "relu2":
        y = jnp.square(jnp.maximum(y, 0.0))
    if has_res:
        y = y + res_ref[...]
    if has_fnorm:
        ms = jnp.mean(y * y, axis=-1, keepdims=True)
        y = y * lax.rsqrt(ms + NORM_EPS) * fg_ref[...]
    o_ref[...] = y.astype(o_ref.dtype)


def fused_matmul(x, w, *, gain=None, res=None, fgain=None, act=None, out_dtype=F32, tm, tn, name):
    m, k = x.shape
    n = w.shape[1]
    assert m % tm == 0 and n % tn == 0 and w.shape[0] == k
    assert fgain is None or tn == n
    in_specs = [pl.BlockSpec((tm, k), lambda i, j: (i, 0))]
    args = [x]
    if gain is not None:
        in_specs.append(pl.BlockSpec((1, k), lambda i, j: (0, 0)))
        args.append(gain.reshape(1, k).astype(F32))
    in_specs.append(pl.BlockSpec((k, tn), lambda i, j: (0, j)))
    args.append(w)
    if res is not None:
        in_specs.append(pl.BlockSpec((tm, tn), lambda i, j: (i, j)))
        args.append(res)
    if fgain is not None:
        in_specs.append(pl.BlockSpec((1, n), lambda i, j: (0, 0)))
        args.append(fgain.reshape(1, n).astype(F32))
    scratch = [pltpu.VMEM((tm, k), BF16)] if gain is not None else []
    kern = functools.partial(_mm_kernel, has_norm=gain is not None, act=act, has_res=res is not None,
                             has_fnorm=fgain is not None)
    return pl.pallas_call(
        kern,
        grid=(m // tm, n // tn),
        in_specs=in_specs,
        out_specs=pl.BlockSpec((tm, tn), lambda i, j: (i, j)),
        out_shape=jax.ShapeDtypeStruct((m, n), out_dtype),
        scratch_shapes=scratch,
        compiler_params=_cparams(("parallel", "arbitrary")),
        name=name,
    )(*args)


def _row_tile(m, pref):
    for t in (pref, 1024, 512, 256, 128, 64, 32, 16, 8):
        if t <= pref and m % t == 0:
            return t
    raise ValueError(m)


def mlp_block(h, norm_g, w_up, w_down, final_g, name):
    tm = _row_tile(h.shape[0], 1024)
    a = fused_matmul(h, w_up, gain=norm_g, act="relu2", out_dtype=BF16, tm=tm, tn=1024, name=name + "_up")
    if final_g is None:
        return fused_matmul(a, w_down, res=h, tm=tm, tn=512, name=name + "_down")
    tm = _row_tile(h.shape[0], 512)
    return fused_matmul(a, w_down, res=h, fgain=final_g, tm=tm, tn=D_MODEL, name=name + "_down")


def _ssd_kernel(z_ref, x_ref, bc_ref, dt_ref, cw_ref, cb_ref, dtb_ref, alog_ref, dsk_ref, gn_ref,
                c0_ref, s0_ref, tri_ref, ex_ref,
                y_ref, sout_ref,
                ext_ref, st_ref, xs_ref, b_ref, c_ref, *, t_valid, n_chunks):
    c = pl.program_id(1)
    L = SSD_CHUNK

    @pl.when(c == 0)
    def _():
        ext_ref[0:8, :] = c0_ref[0]
        st_ref[...] = s0_ref[0].T

    @pl.when(c > 0)
    def _():
        ext_ref[0:8, :] = ext_ref[L:L + 8, :]

    ext_ref[8:L + 8, 0:D_INNER] = x_ref[...]
    ext_ref[8:L + 8, D_INNER:CONV_DIM] = bc_ref[...]

    cblk = 512
    for i in range(CONV_DIM // cblk):
        sl = slice(i * cblk, (i + 1) * cblk)
        acc = cb_ref[:, sl]
        for k in range(CONV_WIDTH):
            acc = acc + cw_ref[k:k + 1, sl] * ext_ref[5 + k:5 + k + L, sl]
        v = _silu(acc)
        if i < 4:
            xs_ref[:, sl] = v
        elif i < 6:
            b_ref[:, (i - 4) * cblk:(i - 3) * cblk] = v
        else:
            c_ref[:, (i - 6) * cblk:(i - 5) * cblk] = v

    dt_raw = dt_ref[...] + dtb_ref[...]
    dt = jnp.maximum(dt_raw, 0.0) + jnp.log1p(jnp.exp(-jnp.abs(dt_raw)))
    if t_valid % L:
        row = c * L + lax.broadcasted_iota(jnp.int32, (L, LANES), 0)
        dt = jnp.where(row < t_valid, dt, 0.0)
    da = dt * (-jnp.exp(alog_ref[...]))
    acs = _split_dot(da, tri_ref[...], left=True)
    acs_t = acs.T
    expd = _split_dot(jnp.concatenate([acs, dt], axis=0), ex_ref[...])
    a_exp = expd[0:L]
    dt_exp = expd[L:2 * L]

    li = lax.broadcasted_iota(jnp.int32, (L, L), 0)
    si = lax.broadcasted_iota(jnp.int32, (L, L), 1)
    causal = si <= li
    first_half = si < SSM_HEAD_DIM
    gw = D_INNER // SSM_GROUPS
    for g in range(SSM_GROUPS):
        gs = slice(g * gw, (g + 1) * gw)
        bg = b_ref[:, g * SSM_STATE:(g + 1) * SSM_STATE]
        cg = c_ref[:, g * SSM_STATE:(g + 1) * SSM_STATE].astype(BF16)
        cbm = _dot_nt(cg, bg.astype(BF16))
        ag = a_exp[:, gs]
        xg = xs_ref[:, gs]
        xdt = xg * dt_exp[:, gs]
        a_last = ag[L - 1:L, :]
        y_off = _dot(cg, st_ref[:, gs].astype(BF16)) * jnp.exp(ag)
        ys = []
        for p in range(2):
            h0 = 4 * g + 2 * p
            xp = xdt[:, p * LANES:(p + 1) * LANES].astype(BF16)
            outs = []
            for h in (h0, h0 + 1):
                seg = acs[:, h:h + 1] - acs_t[h:h + 1, :]
                lm = jnp.exp(jnp.where(causal, seg, NEG_INF))
                outs.append(_dot((cbm * lm).astype(BF16), xp))
            ys.append(jnp.where(first_half, outs[0], outs[1]))
        y = jnp.concatenate(ys, axis=1) + y_off
        xd = (xdt * jnp.exp(a_last - ag)).astype(BF16)
        st_ref[:, gs] = st_ref[:, gs] * jnp.exp(a_last) + _dot(bg.T.astype(BF16), xd)
        zg = z_ref[:, gs]
        yy = (y + dsk_ref[:, gs] * xg) * _silu(zg)
        ms = jnp.mean(yy * yy, axis=-1, keepdims=True)
        y_ref[:, gs] = (yy * lax.rsqrt(ms + NORM_EPS) * gn_ref[:, gs]).astype(y_ref.dtype)

    @pl.when(c == n_chunks - 1)
    def _():
        sout_ref[0] = st_ref[...].T


def ssd_mixer(zx, conv0, st0, prm, *, batch, t_pad, t_valid, name):
    n_chunks = t_pad // SSD_CHUNK
    L = SSD_CHUNK
    conv_w, conv_b, dt_bias, a_log, d_skip, gate_norm = prm
    lane_pad = lambda v: jnp.pad(v.astype(F32), (0, LANES - SSM_HEADS)).reshape(1, LANES)
    tri = jnp.asarray(np.tril(np.ones((L, L), np.float32)), BF16)
    ex_np = np.zeros((LANES, D_INNER), np.float32)
    ex_np[np.arange(D_INNER) // SSM_HEAD_DIM, np.arange(D_INNER)] = 1.0
    ex = jnp.asarray(ex_np, BF16)
    c0 = jnp.pad(conv0.astype(F32), ((0, 0), (8 - (CONV_WIDTH - 1), 0), (0, 0)))
    row = lambda b, c: (b * n_chunks + c)
    const = lambda shape: pl.BlockSpec(shape, lambda b, c: (0,) * len(shape))
    kern = functools.partial(_ssd_kernel, t_valid=t_valid, n_chunks=n_chunks)
    y, s_out = pl.pallas_call(
        kern,
        grid=(batch, n_chunks),
        in_specs=[
            pl.BlockSpec((L, D_INNER), lambda b, c: (row(b, c), 0)),
            pl.BlockSpec((L, D_INNER), lambda b, c: (row(b, c), 1)),
            pl.BlockSpec((L, D_INNER), lambda b, c: (row(b, c), 2)),
            pl.BlockSpec((L, LANES), lambda b, c: (row(b, c), (D_INNER + CONV_DIM) // LANES)),
            const((CONV_WIDTH, CONV_DIM)), const((1, CONV_DIM)), const((1, LANES)), const((1, LANES)),
            const((1, D_INNER)), const((1, D_INNER)),
            pl.BlockSpec((1, 8, CONV_DIM), lambda b, c: (b, 0, 0)),
            pl.BlockSpec((1, D_INNER, SSM_STATE), lambda b, c: (b, 0, 0)),
            const((L, L)), const((LANES, D_INNER)),
        ],
        out_specs=[
            pl.BlockSpec((L, D_INNER), lambda b, c: (row(b, c), 0)),
            pl.BlockSpec((1, D_INNER, SSM_STATE), lambda b, c: (b, 0, 0)),
        ],
        out_shape=[
            jax.ShapeDtypeStruct((batch * t_pad, D_INNER), BF16),
            jax.ShapeDtypeStruct((batch, D_INNER, SSM_STATE), F32),
        ],
        scratch_shapes=[
            pltpu.VMEM((L + 8, CONV_DIM), F32),
            pltpu.VMEM((SSM_STATE, D_INNER), F32),
            pltpu.VMEM((L, D_INNER), F32),
            pltpu.VMEM((L, SSM_GROUPS * SSM_STATE), F32),
            pltpu.VMEM((L, SSM_GROUPS * SSM_STATE), F32),
        ],
        compiler_params=_cparams(("parallel", "arbitrary")),
        name=name,
    )(zx, zx, zx, zx,
      conv_w.astype(F32).T, conv_b.astype(F32).reshape(1, CONV_DIM), lane_pad(dt_bias), lane_pad(a_log),
      jnp.repeat(d_skip.astype(F32), SSM_HEAD_DIM).reshape(1, D_INNER), gate_norm.astype(F32).reshape(1, D_INNER),
      c0, st0.astype(F32).reshape(batch, D_INNER, SSM_STATE), tri, ex)
    return y, s_out.reshape(batch, SSM_HEADS, SSM_HEAD_DIM, SSM_STATE)


CMP_PAGES = 16


def _cmp_u_kernel(pt_ref, *refs, feature_major):
    del pt_ref
    pages = refs[:CMP_PAGES]
    perm_ref, w_ref, uf_ref, us_ref, y_ref = refs[CMP_PAGES:]
    regroup = _dot_nt if feature_major else _dot
    for p, page in enumerate(pages):
        yp = regroup(perm_ref[...], page[...].astype(BF16))
        for j in range(CMP_STRIDE):
            y_ref[j, p * 8:(p + 1) * 8, :] = yp[j * 8:(j + 1) * 8, :]
    rows = y_ref.shape[1]
    low = lax.broadcasted_iota(jnp.int32, (rows, LANES), 1) < HEAD_DIM
    accs = [None] * (2 * KV_HEADS)
    for a in range(CMP_STRIDE // 4):
        for q in range(4):
            blocks = [y_ref[4 * a + jj, :, q * LANES:(q + 1) * LANES] for jj in range(4)]
            swapped = [pltpu.roll(x, HEAD_DIM, axis=1) for x in blocks]
            for gl in range(2):
                if gl == 0:
                    pairs = [jnp.where(low, blocks[0], swapped[1]), jnp.where(low, blocks[2], swapped[3])]
                else:
                    pairs = [jnp.where(low, swapped[0], blocks[1]), jnp.where(low, swapped[2], blocks[3])]
                part = _dot(jnp.concatenate(pairs, axis=1).astype(BF16), w_ref[q // 2, a])
                k = 2 * q + gl
                accs[k] = part if accs[k] is None else accs[k] + part
    for k in range(2 * KV_HEADS):
        uf_ref[0, k] = accs[k][:, 0:CMP_HIDDEN]
        us_ref[0, k] = accs[k][:, CMP_HIDDEN:2 * CMP_HIDDEN]


def cmp_project(src2d, page_ids, col_block, w1, name, feature_major=False):
    batch, n_pages = page_ids.shape
    assert n_pages % CMP_PAGES == 0
    n_sub = n_pages * (PAGE_SIZE // CMP_STRIDE)
    rows = CMP_PAGES * (PAGE_SIZE // CMP_STRIDE)
    w1r = w1.astype(F32).reshape(2, 2 * CMP_STRIDE, HEAD_DIM, CMP_HIDDEN)
    wj = jnp.concatenate([w1r[:, :CMP_STRIDE], w1r[:, CMP_STRIDE:]], axis=-1)
    wpad = wj.reshape(2, CMP_STRIDE // 4, 4 * HEAD_DIM, 2 * CMP_HIDDEN).astype(BF16)
    perm = np.zeros((PAGE_SIZE, PAGE_SIZE), np.float32)
    for j in range(CMP_STRIDE):
        for s in range(PAGE_SIZE // CMP_STRIDE):
            perm[j * 8 + s, CMP_STRIDE * s + j] = 1.0
    perm = jnp.asarray(perm, BF16)

    def page_spec(p):
        shape = (4 * LANES, PAGE_SIZE) if feature_major else (PAGE_SIZE, 4 * LANES)
        return pl.BlockSpec(shape, lambda b, c, pt: (pt[b, c * CMP_PAGES + p], col_block))

    u_shape = jax.ShapeDtypeStruct((batch, 2 * KV_HEADS, n_sub, CMP_HIDDEN), F32)
    u_spec = pl.BlockSpec((1, 2 * KV_HEADS, rows, CMP_HIDDEN), lambda b, c, pt: (b, 0, c, 0))
    return pl.pallas_call(
        functools.partial(_cmp_u_kernel, feature_major=feature_major),
        grid_spec=pltpu.PrefetchScalarGridSpec(
            num_scalar_prefetch=1,
            grid=(batch, n_pages // CMP_PAGES),
            in_specs=[page_spec(p) for p in range(CMP_PAGES)] + [
                pl.BlockSpec((PAGE_SIZE, PAGE_SIZE), lambda b, c, pt: (0, 0)),
                pl.BlockSpec((2, CMP_STRIDE // 4, 4 * HEAD_DIM, 2 * CMP_HIDDEN), lambda b, c, pt: (0, 0, 0, 0)),
            ],
            out_specs=[u_spec, u_spec],
            scratch_shapes=[pltpu.VMEM((CMP_STRIDE, rows, 4 * LANES), F32)],
        ),
        out_shape=[u_shape, u_shape],
        compiler_params=_cparams(("parallel", "arbitrary")),
        name=name,
    )(page_ids, *([src2d] * CMP_PAGES), perm, wpad)


def _cmp_out_kernel(uf_ref, us_ref, pos_ref, w1_ref, w2_ref, o_ref, sh_ref, *, n_sub):
    sh_ref[0:n_sub, :] = us_ref[0, 0]
    sh_ref[n_sub:n_sub + 8, :] = jnp.zeros((8, CMP_HIDDEN), F32)
    pos_term = _dot(pos_ref[0].astype(BF16), w1_ref[0])[0:1, :]
    hid = jax.nn.gelu(uf_ref[0, 0] + sh_ref[pl.ds(1, n_sub), :] + pos_term)
    out = _dot(hid.astype(BF16), w2_ref[0])
    rows = lax.broadcasted_iota(jnp.int32, out.shape, 0)
    o_ref[0, 0] = jnp.where(rows < n_sub - 1, out, 0.0).astype(o_ref.dtype)


def cmp_finish(uf, us, pos_emb, w1, w2, name):
    batch, nkvg, n_sub, _ = uf.shape
    flat = 2 * CMP_STRIDE * HEAD_DIM
    pos = jnp.pad(pos_emb.astype(F32).reshape(2, 1, flat), ((0, 0), (0, 7), (0, 0)))
    u_spec = pl.BlockSpec((1, 1, n_sub, CMP_HIDDEN), lambda b, k: (b, k, 0, 0))
    return pl.pallas_call(
        functools.partial(_cmp_out_kernel, n_sub=n_sub),
        grid=(batch, nkvg),
        in_specs=[u_spec, u_spec,
                  pl.BlockSpec((1, 8, flat), lambda b, k: (k // KV_HEADS, 0, 0)),
                  pl.BlockSpec((1, flat, CMP_HIDDEN), lambda b, k: (k // KV_HEADS, 0, 0)),
                  pl.BlockSpec((1, CMP_HIDDEN, HEAD_DIM), lambda b, k: (k // KV_HEADS, 0, 0))],
        out_specs=pl.BlockSpec((1, 1, n_sub, HEAD_DIM), lambda b, k: (b, k, 0, 0)),
        out_shape=jax.ShapeDtypeStruct((batch, nkvg, n_sub, HEAD_DIM), BF16),
        scratch_shapes=[pltpu.VMEM((n_sub + 8, CMP_HIDDEN), F32)],
        compiler_params=_cparams(("parallel", "parallel")),
        name=name,
    )(uf, us, pos, w1.astype(BF16), w2.astype(BF16))


def _cmp_attn_kernel(q_ref, kc_ref, vc_ref, wsel_ref, o_ref, sel_ref, *rest, tq, ncp, nblk_pad, n_sel, q_base,
                     want_idx):
    qi = pl.program_id(2)
    use_ref = None if want_idx else rest[0]
    tpos = q_base + qi * tq + lax.broadcasted_iota(jnp.int32, (tq, 1), 0)
    c_end = CMP_STRIDE * lax.broadcasted_iota(jnp.int32, (1, ncp), 1) + (2 * CMP_STRIDE - 1)
    ok = c_end <= tpos
    any_ok = (tpos >= 2 * CMP_STRIDE - 1).astype(F32)
    kc = kc_ref[0, 0]
    vw = jnp.concatenate([vc_ref[0, 0], wsel_ref[...]], axis=1)
    p_slc = None
    for h in range(GQA):
        s = jnp.where(ok, _dot_nt(q_ref[0, h], kc), NEG_INF)
        e = jnp.exp(s - jnp.max(s, axis=-1, keepdims=True))
        r = _dot(e.astype(BF16), vw)
        inv = any_ok / r[:, HEAD_DIM:HEAD_DIM + 1]
        o_ref[0, h] = r[:, 0:HEAD_DIM] * inv
        part = r[:, LANES:] * inv
        p_slc = part if p_slc is None else p_slc + part
    tq_l = max(tq, LANES)
    if tq < LANES:
        p_slc = jnp.concatenate([p_slc, jnp.zeros((tq_l - tq, nblk_pad), F32)], axis=0)
    sc = p_slc.T
    blk = lax.broadcasted_iota(jnp.int32, (nblk_pad, tq_l), 0)
    tp = q_base + qi * tq + lax.broadcasted_iota(jnp.int32, (nblk_pad, tq_l), 1)
    cur = jnp.right_shift(tp, SLC_SHIFT)
    forced = (blk == 0) | (blk == cur) | (blk == cur - 1)
    reach = blk * SLC_BLOCK <= tp
    sc = jnp.where(forced, 1e30, jnp.where(reach, sc, -1.0))
    sc = jnp.where(blk < n_sel, sc, -3.0)
    removed = -2.0
    idx_rows = []
    for _ in range(min(SLC_TOP_N, n_sel)):
        mx = jnp.max(sc, axis=0, keepdims=True)
        idx = jnp.min(jnp.where(sc == mx, blk, nblk_pad), axis=0, keepdims=True)
        sc = jnp.where(blk == idx, removed, sc)
        idx_rows.append(idx)
    if want_idx:
        sel_ref[0, 0] = jnp.concatenate(idx_rows, axis=0)
    else:
        chosen_t = (sc == removed).astype(F32).T[0:tq]
        sel_ref[0, 0] = ((chosen_t - 1.0) * 1e30).astype(sel_ref.dtype)
        used = jnp.max(chosen_t, axis=0, keepdims=True)
        use_ref[0, 0, 0] = jnp.broadcast_to(used, (8, nblk_pad))


N_ALIBI = 6


def _slope_feats(slopes):
    s1 = slopes.astype(BF16)
    r1 = slopes - s1.astype(F32)
    s2 = r1.astype(BF16)
    s3 = (r1 - s2.astype(F32)).astype(BF16)
    return jnp.stack([s1, s2, s3, s1, s2, s3], axis=-1)


def _pos_feats(pos):
    lo = jnp.bitwise_and(pos, 255)
    hi = pos - lo
    return jnp.stack([lo, lo, lo, hi, hi, hi], axis=-1).astype(BF16)


def _with_feats(x, feats, width):
    lead = x.shape[:-1]
    f = jnp.broadcast_to(feats, lead + feats.shape[-1:]).astype(BF16)
    pad = jnp.zeros(lead + (width - x.shape[-1] - f.shape[-1],), BF16)
    return jnp.concatenate([x.astype(BF16), f, pad], axis=-1)


def _with_ones(v):
    return _with_feats(v, jnp.ones((1,), BF16), LANES)


def cmp_attention(qa, kvc, *, tq, n_sel, q_base, want_idx, name):
    b, _, t, _ = qa.shape
    ncp = kvc.shape[2]
    nblk_pad = -(-n_sel // LANES) * LANES
    c_end = CMP_STRIDE * jnp.arange(ncp, dtype=jnp.int32) + (2 * CMP_STRIDE - 1)
    kca = _with_feats(kvc[:, :KV_HEADS], _pos_feats(c_end), LANES)
    vca = _with_ones(kvc[:, KV_HEADS:])
    c = np.arange(ncp)[:, None]
    j = np.arange(nblk_pad)[None, :]
    wsel = ((c // 4 == j).astype(np.float32) + ((c + 1) // 4 == j).astype(np.float32)) * (c < ncp - 1)
    tq_l = max(tq, LANES)
    if want_idx:
        sel_shape = jax.ShapeDtypeStruct((b, KV_HEADS, SLC_TOP_N, (t // tq) * tq_l), jnp.int32)
        sel_spec = pl.BlockSpec((1, 1, SLC_TOP_N, tq_l), lambda bi, g, qi: (bi, g, 0, qi))
    else:
        sel_shape = jax.ShapeDtypeStruct((b, KV_HEADS, t, nblk_pad), BF16)
        sel_spec = pl.BlockSpec((1, 1, tq, nblk_pad), lambda bi, g, qi: (bi, g, qi, 0))
    kern = functools.partial(_cmp_attn_kernel, tq=tq, ncp=ncp, nblk_pad=nblk_pad, n_sel=n_sel, q_base=q_base,
                             want_idx=want_idx)
    out_specs = [pl.BlockSpec((1, GQA, tq, HEAD_DIM), lambda bi, g, qi: (bi, g, qi, 0)), sel_spec]
    out_shape = [jax.ShapeDtypeStruct((b, ATT_HEADS, t, HEAD_DIM), F32), sel_shape]
    if not want_idx:
        out_specs.append(pl.BlockSpec((1, 1, 1, 8, nblk_pad), lambda bi, g, qi: (bi, g, qi, 0, 0)))
        out_shape.append(jax.ShapeDtypeStruct((b, KV_HEADS, t // tq, 8, nblk_pad), F32))
    return pl.pallas_call(
        kern,
        grid=(b, KV_HEADS, t // tq),
        in_specs=[
            pl.BlockSpec((1, GQA, tq, LANES), lambda bi, g, qi: (bi, g, qi, 0)),
            pl.BlockSpec((1, 1, ncp, LANES), lambda bi, g, qi: (bi, g, 0, 0)),
            pl.BlockSpec((1, 1, ncp, LANES), lambda bi, g, qi: (bi, g, 0, 0)),
            pl.BlockSpec((ncp, nblk_pad), lambda bi, g, qi: (0, 0)),
        ],
        out_specs=out_specs,
        out_shape=out_shape,
        compiler_params=_cparams(("parallel", "parallel", "parallel")),
        name=name,
    )(qa, kca, vca, jnp.asarray(wsel, BF16))


SEL_FEAT = 256


def _sel_kernel(use_ref, qa_ref, ka_ref, v_ref, o_ref, m_ref, acc_ref, *, tq, tk):
    bi = pl.program_id(0)
    g = pl.program_id(1)
    qi = pl.program_id(2)
    n_qt = pl.num_programs(2)
    per = tq // tk
    n_kc = n_qt * per
    m_ref[...] = jnp.full(m_ref.shape, NEG_INF, F32)
    acc_ref[...] = jnp.zeros(acc_ref.shape, F32)
    rows = lax.broadcasted_iota(jnp.int32, (tq, tk), 0)
    cols = lax.broadcasted_iota(jnp.int32, (tq, tk), 1)

    def chunk(j, diag_offset):
        k0 = pl.multiple_of(j * tk, tk)
        ka = ka_ref[0, 0, pl.ds(k0, tk), :]
        va = v_ref[0, 0, pl.ds(k0, tk), :]
        for h in range(GQA):
            s = _dot_nt(qa_ref[0, 0, h], ka)
            if diag_offset is not None:
                s = jnp.where(cols + diag_offset <= rows, s, NEG_INF)
            m_prev = m_ref[h]
            m_new = jnp.maximum(m_prev, jnp.max(s, axis=-1, keepdims=True))
            p = jnp.exp(s - jnp.concatenate([m_new] * (tk // LANES), axis=1))
            acc_ref[h] = jnp.exp(m_prev - m_new) * acc_ref[h] + _dot(p.astype(BF16), va)
            m_ref[h] = m_new

    def body(j, carry):
        @pl.when(use_ref[((bi * KV_HEADS + g) * n_qt + qi) * n_kc + j] > 0)
        def _():
            chunk(j, None)
        return carry

    lax.fori_loop(0, qi * per, body, 0)
    for c in range(per):
        chunk(qi * per + c, c * tk)
    for h in range(GQA):
        acc = acc_ref[h]
        o_ref[0, h] = acc[:, 0:HEAD_DIM] / acc[:, HEAD_DIM:HEAD_DIM + 1]


def sel_attention_dense(chunk_used, qa, ka, va, *, tq, tk, name):
    b, _, _, t, _ = qa.shape
    return pl.pallas_call(
        functools.partial(_sel_kernel, tq=tq, tk=tk),
        grid_spec=pltpu.PrefetchScalarGridSpec(
            num_scalar_prefetch=1,
            grid=(b, KV_HEADS, t // tq),
            in_specs=[
                pl.BlockSpec((1, 1, GQA, tq, SEL_FEAT), lambda bi, g, qi, *_: (bi, g, 0, qi, 0)),
                pl.BlockSpec((1, 1, t, SEL_FEAT), lambda bi, g, qi, *_: (bi, g, 0, 0)),
                pl.BlockSpec((1, 1, t, LANES), lambda bi, g, qi, *_: (bi, g, 0, 0)),
            ],
            out_specs=pl.BlockSpec((1, GQA, tq, HEAD_DIM), lambda bi, g, qi, *_: (bi, g, qi, 0)),
            scratch_shapes=[pltpu.VMEM((GQA, tq, LANES), F32), pltpu.VMEM((GQA, tq, LANES), F32)],
        ),
        out_shape=jax.ShapeDtypeStruct((b, ATT_HEADS, t, HEAD_DIM), F32),
        compiler_params=_cparams(("parallel", "parallel", "arbitrary")),
        name=name,
    )(chunk_used.reshape(-1), qa, ka, va)


def _sel_gather_kernel(idx_ref, pt_ref, q_ref, slope_ref, cache_ref, new_ref, o_ref, kbuf, vbuf, sem, *,
                       n_tok, tq, past_len, n_steps):
    b = pl.program_id(0)
    g = pl.program_id(1)
    step = b * KV_HEADS + g
    slot = step % 2
    new_blk = past_len // SLC_BLOCK
    halves = PAGE_SIZE // SLC_BLOCK

    def block_copy(bb, gg, sl, t, n, kv, blk, newest):
        buf = vbuf if kv else kbuf
        dst = buf.at[sl, t, :, pl.ds(n * PAGE_SIZE, PAGE_SIZE)]
        col = kv * KV_HEADS + gg
        if newest:
            src = new_ref.at[bb, col]
        else:
            page = pt_ref[bb * (past_len // PAGE_SIZE) + jnp.minimum(blk, new_blk - 1) // halves]
            src = cache_ref.at[page, col]
        return pltpu.make_async_copy(src, dst, sem.at[sl])

    def for_each_block(bb, gg, fn):
        for t in range(n_tok):
            for n in range(SLC_TOP_N):
                blk = idx_ref[((bb * KV_HEADS + gg) * SLC_TOP_N + n) * n_tok + t]
                for kv in range(2):
                    fn(t, n, kv, blk)

    def start(bb, gg, sl):
        def one(t, n, kv, blk):
            @pl.when(blk >= new_blk)
            def _():
                block_copy(bb, gg, sl, t, n, kv, blk, True).start()

            @pl.when(blk < new_blk)
            def _():
                block_copy(bb, gg, sl, t, n, kv, blk, False).start()
        for_each_block(bb, gg, one)

    @pl.when(step == 0)
    def _():
        start(b, g, slot)

    @pl.when(step + 1 < n_steps)
    def _():
        nxt = step + 1
        start(nxt // KV_HEADS, nxt % KV_HEADS, 1 - slot)

    for_each_block(b, g, lambda t, n, kv, blk: block_copy(b, g, slot, t, n, kv, blk, False).wait())

    r = GQA * tq
    q = (q_ref[0].reshape(r, HEAD_DIM) * (HEAD_DIM ** -0.5)).astype(BF16)
    nk = SLC_TOP_N * PAGE_SIZE
    lane = lax.broadcasted_iota(jnp.int32, (1, nk), 1)
    trow = lax.broadcasted_iota(jnp.int32, (tq, 1), 0)
    tpos = past_len + trow
    acc = jnp.zeros((GQA, tq, HEAD_DIM), F32)
    in_page = jnp.bitwise_and(lane, PAGE_SIZE - 1)
    half_of_lane = jnp.right_shift(in_page, SLC_SHIFT)
    slot_of_lane = jnp.right_shift(lane, SLC_SHIFT + 1)
    for t in range(n_tok):
        kpos = in_page
        chosen_half = jnp.zeros((1, nk), jnp.int32)
        for n in range(SLC_TOP_N):
            blk = idx_ref[((b * KV_HEADS + g) * SLC_TOP_N + n) * n_tok + t]
            here = slot_of_lane == n
            kpos = jnp.where(here, kpos + (blk // halves) * PAGE_SIZE, kpos)
            chosen_half = jnp.where(here, blk % halves, chosen_half)
        d = (tpos - kpos).astype(F32)
        ok = (d >= 0.0) & (half_of_lane == chosen_half)
        s = _dot(q, kbuf[slot, t].astype(BF16))
        s3 = s.reshape(GQA, tq, nk) - slope_ref[0] * d[None]
        s3 = jnp.where(ok[None], s3, NEG_INF)
        m = jnp.max(s3, axis=-1, keepdims=True)
        e = jnp.exp(s3 - m)
        p = e / jnp.sum(e, axis=-1, keepdims=True)
        o = _dot_nt(p.reshape(r, nk).astype(BF16), vbuf[slot, t].astype(BF16)).reshape(GQA, tq, HEAD_DIM)
        acc = jnp.where((trow == t)[None], o, acc)
    o_ref[0] = acc


def sel_attention_gather(q_h, idx, page_table, cache, new_kv, slopes, *, n_tok, past_len, name):
    b, _, tq, _ = q_h.shape
    n_steps = b * KV_HEADS
    nk = SLC_TOP_N * PAGE_SIZE
    kern = functools.partial(_sel_gather_kernel, n_tok=n_tok, tq=tq, past_len=past_len, n_steps=n_steps)
    return pl.pallas_call(
        kern,
        grid_spec=pltpu.PrefetchScalarGridSpec(
            num_scalar_prefetch=2,
            grid=(b, KV_HEADS),
            in_specs=[
                pl.BlockSpec((1, GQA, tq, HEAD_DIM), lambda bi, g, *_: (bi, g, 0, 0)),
                pl.BlockSpec((1, GQA, 1, 1), lambda bi, g, *_: (g, 0, 0, 0)),
                pl.BlockSpec(memory_space=pl.ANY),
                pl.BlockSpec(memory_space=pl.ANY),
            ],
            out_specs=pl.BlockSpec((1, GQA, tq, HEAD_DIM), lambda bi, g, *_: (bi, g, 0, 0)),
            scratch_shapes=[pltpu.VMEM((2, n_tok, HEAD_DIM, nk), F32), pltpu.VMEM((2, n_tok, HEAD_DIM, nk), F32),
                            pltpu.SemaphoreType.DMA((2,))],
        ),
        out_shape=jax.ShapeDtypeStruct((b, ATT_HEADS, tq, HEAD_DIM), F32),
        compiler_params=_cparams(("arbitrary", "arbitrary")),
        name=name,
    )(idx.reshape(-1), page_table.reshape(-1), q_h, slopes.reshape(KV_HEADS, GQA, 1, 1), cache, new_kv)


def _win_kernel(q_ref, k_ref, v_ref, oc_ref, os_ref, gl_ref, o_ref, *, tq, nk, k_valid_from):
    qi = pl.program_id(2)
    r0 = pl.multiple_of(qi * tq, tq)
    kw = k_ref[0, 0, pl.ds(r0, nk), :]
    vw = v_ref[0, 0, pl.ds(r0, nk), :]
    qrow = r0 + WINDOW + lax.broadcasted_iota(jnp.int32, (tq, 1), 0)
    krow = r0 + lax.broadcasted_iota(jnp.int32, (1, nk), 1)
    di = qrow - krow
    ok = (di >= 0) & (di < WINDOW) & (krow >= k_valid_from)
    for h in range(GQA):
        s = jnp.where(ok, _dot_nt(q_ref[0, h], kw), NEG_INF)
        e = jnp.exp(s - jnp.max(s, axis=-1, keepdims=True))
        r = _dot(e.astype(BF16), vw)
        ow = r[:, 0:HEAD_DIM] / r[:, HEAD_DIM:HEAD_DIM + 1]
        gates = 1.0 / (1.0 + jnp.exp(-gl_ref[0, h]))
        o_ref[0, h] = gates[:, 0:1] * oc_ref[0, h] + gates[:, 1:2] * os_ref[0, h] + gates[:, 2:3] * ow


def win_attention_merge(qa, kw, vw, o_c, o_s, gate_logits, *, tq, k_valid_from, name):
    b, _, t, _ = qa.shape
    nk = tq + WINDOW
    lk = kw.shape[2]
    assert lk >= t + WINDOW
    kwa = _with_feats(kw, _pos_feats(jnp.arange(lk, dtype=jnp.int32)), LANES)
    vwa = _with_ones(vw)
    head_spec = lambda w: pl.BlockSpec((1, GQA, tq, w), lambda bi, g, qi: (bi, g, qi, 0))
    kv_spec = pl.BlockSpec((1, 1, lk, LANES), lambda bi, g, qi: (bi, g, 0, 0))
    return pl.pallas_call(
        functools.partial(_win_kernel, tq=tq, nk=nk, k_valid_from=k_valid_from),
        grid=(b, KV_HEADS, t // tq),
        in_specs=[head_spec(LANES), kv_spec, kv_spec,
                  head_spec(HEAD_DIM), head_spec(HEAD_DIM), head_spec(N_BRANCH)],
        out_specs=head_spec(HEAD_DIM),
        out_shape=jax.ShapeDtypeStruct((b, ATT_HEADS, t, HEAD_DIM), F32),
        compiler_params=_cparams(("parallel", "parallel", "parallel")),
        name=name,
    )(qa, kwa, vwa, o_c, o_s, gate_logits)


def _heads(x2d, b, t, n_heads):
    return x2d.reshape(b, t, n_heads, HEAD_DIM).transpose(0, 2, 1, 3)


def _kv_stack(qkv, b, t, col):
    return qkv[:, col:col + 2 * KV_DIM].reshape(b, t, 2, KV_HEADS, HEAD_DIM)


def alibi_slopes():
    return 2.0 ** (-8.0 * jnp.arange(1, ATT_HEADS + 1, dtype=jnp.float32) / ATT_HEADS)


def nsa_prompt_layer(h2d, b, t, norm_g, w_in, cmp_pos, cmp_w1, cmp_w2, w_out, slopes):
    tm = _row_tile(h2d.shape[0], 1024)
    qkv = fused_matmul(h2d, w_in, gain=norm_g, tm=tm, tn=ATT_IN_PAD // 3, name="att_in_p")
    q_h = _heads(qkv[:, :D_MODEL], b, t, ATT_HEADS)
    gl_h = qkv[:, D_MODEL + 6 * KV_DIM:D_MODEL + 6 * KV_DIM + ATT_HEADS * N_BRANCH]
    gl_h = gl_h.reshape(b, t, ATT_HEADS, N_BRANCH).transpose(0, 2, 1, 3)

    n_pages = t // PAGE_SIZE
    page_ids = (jnp.arange(b, dtype=jnp.int32)[:, None] * n_pages + jnp.arange(n_pages, dtype=jnp.int32)[None, :])
    uf, us = cmp_project(qkv, page_ids, D_MODEL // (2 * KV_DIM), cmp_w1, "cmp_u_p")
    kvc = cmp_finish(uf, us, cmp_pos, cmp_w1, cmp_w2, "cmp_out_p")

    n_sel = t // SLC_BLOCK
    tq = 256
    q_s = (q_h * (HEAD_DIM ** -0.5)).astype(BF16)
    sfeat = _slope_feats(slopes)[None, :, None, :]
    qa = _with_feats(q_s, sfeat, LANES)
    o_c, sel_bias, blk_used = cmp_attention(qa, kvc, tq=tq, n_sel=n_sel, q_base=0, want_idx=False, name="cmp_attn_p")
    tq_sel, tk_sel = min(512, t), 256
    used = blk_used[:, :, :, 0, :n_sel].reshape(b, KV_HEADS, t // tq_sel, tq_sel // tq, t // tk_sel, tk_sel // SLC_BLOCK)
    chunk_used = (used.max(axis=(3, 5)) > 0.0).astype(jnp.int32)

    nb = sel_bias.shape[-1]
    bias_h = jnp.broadcast_to(sel_bias[:, :, None], (b, KV_HEADS, GQA, t, nb)).reshape(b, ATT_HEADS, t, nb)
    qa_sel = _with_feats(jnp.concatenate([q_s, bias_h], axis=-1), sfeat, SEL_FEAT)
    qa_sel = qa_sel.reshape(b, KV_HEADS, GQA, t, SEL_FEAT)
    ks = _heads(qkv[:, D_MODEL + 2 * KV_DIM:D_MODEL + 3 * KV_DIM], b, t, KV_HEADS).astype(BF16)
    vs = _heads(qkv[:, D_MODEL + 3 * KV_DIM:D_MODEL + 4 * KV_DIM], b, t, KV_HEADS).astype(BF16)
    kpos = jnp.arange(t, dtype=jnp.int32)
    onehot = jax.nn.one_hot(kpos // SLC_BLOCK, nb, dtype=BF16)
    ka = _with_feats(jnp.concatenate([ks, jnp.broadcast_to(onehot, (b, KV_HEADS) + onehot.shape)], axis=-1),
                     _pos_feats(kpos), SEL_FEAT)
    o_s = sel_attention_dense(chunk_used, qa_sel, ka, _with_ones(vs), tq=tq_sel, tk=tk_sel, name="sel_attn_p")

    front = ((0, 0), (0, 0), (WINDOW, 0), (0, 0))
    kw = jnp.pad(_heads(qkv[:, D_MODEL + 4 * KV_DIM:D_MODEL + 5 * KV_DIM], b, t, KV_HEADS).astype(BF16), front)
    vw = jnp.pad(_heads(qkv[:, D_MODEL + 5 * KV_DIM:D_MODEL + 6 * KV_DIM], b, t, KV_HEADS).astype(BF16), front)
    o = win_attention_merge(qa, kw, vw, o_c, o_s, gl_h, tq=tq, k_valid_from=WINDOW, name="win_attn_p")
    o2d = o.transpose(0, 2, 1, 3).reshape(b * t, D_MODEL)
    h_new = fused_matmul(o2d, w_out, res=h2d, tm=tm, tn=512, name="att_out_p")
    win_rows = min(WINDOW, t)
    return (h_new, _kv_stack(qkv, b, t, D_MODEL), _kv_stack(qkv, b, t, D_MODEL + 2 * KV_DIM),
            _kv_stack(qkv, b, t, D_MODEL + 4 * KV_DIM)[:, t - win_rows:])


def nsa_sample_layer(h2d, b, t, cache_cmp, cache_slc, win_buf, page_table, norm_g, w_in, cmp_pos, cmp_w1, cmp_w2,
                     w_out, slopes):
    n_pages = page_table.shape[1]
    past_len = n_pages * PAGE_SIZE
    tq = 16
    assert t <= tq and t <= 2 * CMP_STRIDE - 1
    qkv = fused_matmul(h2d, w_in, gain=norm_g, tm=h2d.shape[0], tn=ATT_IN_PAD // 3, name="att_in_s")
    tpad = ((0, 0), (0, 0), (0, tq - t), (0, 0))
    q_h = jnp.pad(_heads(qkv[:, :D_MODEL], b, t, ATT_HEADS), tpad)
    gl_h = qkv[:, D_MODEL + 6 * KV_DIM:D_MODEL + 6 * KV_DIM + ATT_HEADS * N_BRANCH]
    gl_h = jnp.pad(gl_h.reshape(b, t, ATT_HEADS, N_BRANCH).transpose(0, 2, 1, 3), tpad)
    new_cmp = _kv_stack(qkv, b, t, D_MODEL)
    new_slc = _kv_stack(qkv, b, t, D_MODEL + 2 * KV_DIM)
    new_win = _kv_stack(qkv, b, t, D_MODEL + 4 * KV_DIM)

    cmp_fm = cache_cmp.transpose(0, 2, 3, 4, 1).reshape(-1, PAGE_SIZE)
    slc_fm = cache_slc.transpose(0, 2, 3, 4, 1).reshape(-1, 2 * KV_HEADS, HEAD_DIM, PAGE_SIZE)
    uf, us = cmp_project(cmp_fm, page_table, 0, cmp_w1, "cmp_u_s", feature_major=True)
    kvc = cmp_finish(uf, us, cmp_pos, cmp_w1, cmp_w2, "cmp_out_s")
    l_all = past_len + t
    n_sel = -(-l_all // SLC_BLOCK)
    qa = _with_feats((q_h * (HEAD_DIM ** -0.5)).astype(BF16), _slope_feats(slopes)[None, :, None, :], LANES)
    o_c, idx = cmp_attention(qa, kvc, tq=tq, n_sel=n_sel, q_base=past_len, want_idx=True, name="cmp_attn_s")

    new_page = jnp.pad(new_slc.reshape(b, t, 2 * KV_HEADS, HEAD_DIM), ((0, 0), (0, PAGE_SIZE - t), (0, 0), (0, 0)))
    o_s = sel_attention_gather(q_h, idx[..., :t], page_table, slc_fm, new_page.transpose(0, 2, 3, 1), slopes,
                               n_tok=t, past_len=past_len, name="sel_attn_s")

    w_rows = win_buf.shape[1]
    win_all = jnp.concatenate([win_buf.astype(F32), new_win], axis=1)
    lk = tq + WINDOW
    shift = WINDOW - w_rows
    kvw = jnp.pad(win_all, ((0, 0), (shift, lk - shift - w_rows - t), (0, 0), (0, 0), (0, 0))).astype(BF16)
    kw = kvw[:, :, 0].transpose(0, 2, 1, 3)
    vw = kvw[:, :, 1].transpose(0, 2, 1, 3)
    o = win_attention_merge(qa, kw, vw, o_c, o_s, gl_h, tq=tq, k_valid_from=shift, name="win_attn_s")
    o2d = o[:, :, :t].transpose(0, 2, 1, 3).reshape(b * t, D_MODEL)
    h_new = fused_matmul(o2d, w_out, res=h2d, tm=h2d.shape[0], tn=512, name="att_out_s")
    return h_new, new_cmp, new_slc, win_all[:, -w_rows:]


def ssm_layer(h2d, b, t, conv_state, ssm_state, norm_g, w_in, conv_prm, w_out, tag):
    t_pad = -(-t // SSD_CHUNK) * SSD_CHUNK
    tm = _row_tile(h2d.shape[0], 1024)
    zx = fused_matmul(h2d, w_in, gain=norm_g, tm=tm, tn=1280, name="ssm_in_" + tag)
    keep = CONV_WIDTH - 1
    tail = zx.reshape(b, t, SSM_IN_PAD)[:, max(t - keep, 0):, D_INNER:D_INNER + CONV_DIM]
    conv_out = tail if t >= keep else jnp.concatenate([conv_state.astype(F32), tail], axis=1)[:, -keep:]
    if t_pad != t:
        zx = jnp.pad(zx.reshape(b, t, SSM_IN_PAD), ((0, 0), (0, t_pad - t), (0, 0))).reshape(b * t_pad, SSM_IN_PAD)
    y, s_new = ssd_mixer(zx, conv_state, ssm_state, conv_prm, batch=b, t_pad=t_pad, t_valid=t, name="ssd_" + tag)
    if t_pad != t:
        y = y.reshape(b, t_pad, D_INNER)[:, :t].reshape(b * t, D_INNER)
    h_new = fused_matmul(y, w_out, res=h2d, tm=tm, tn=512, name="ssm_out_" + tag)
    return h_new, conv_out, s_new


def kernel(x_prompt, x_sample, state_ssm, state_conv, cache_cmp_kv, cache_slc_kv, state_win_kv, page_table, ssm_norm, ssm_w_in, ssm_conv_w, ssm_conv_b, ssm_dt_bias, ssm_a_log, ssm_d, ssm_gate_norm, ssm_w_out, att_norm, att_w_in, att_cmp_pos, att_cmp_w1, att_cmp_w2, att_w_out, mlp_norm, mlp_w_up, mlp_w_down, final_norm):
    bp, tp, _ = x_prompt.shape
    bs, ts, _ = x_sample.shape
    depth = mlp_norm.shape[0]
    slopes = alibi_slopes()
    hp = x_prompt.reshape(bp * tp, D_MODEL)
    hs = x_sample.reshape(bs * ts, D_MODEL)
    ssm_p, conv_p, ssm_s, conv_s = [], [], [], []
    cmp_p, slc_p, win_p, cmp_s, slc_s, win_s = [], [], [], [], [], []
    for i in range(depth):
        j = i // 2
        if i % 2 == 0:
            w_in = jnp.pad(ssm_w_in[j], ((0, 0), (0, SSM_IN_PAD - ssm_w_in.shape[-1]))).astype(BF16)
            w_out = ssm_w_out[j].astype(BF16)
            prm = (ssm_conv_w[j], ssm_conv_b[j], ssm_dt_bias[j], ssm_a_log[j], ssm_d[j], ssm_gate_norm[j])
            zc = jnp.zeros((bp, CONV_WIDTH - 1, CONV_DIM), F32)
            zs = jnp.zeros((bp, SSM_HEADS, SSM_HEAD_DIM, SSM_STATE), F32)
            hp, cp, sp = ssm_layer(hp, bp, tp, zc, zs, ssm_norm[j], w_in, prm, w_out, "p")
            hs, cs, ss = ssm_layer(hs, bs, ts, state_conv[j], state_ssm[j], ssm_norm[j], w_in, prm, w_out, "s")
            conv_p.append(cp); ssm_p.append(sp); conv_s.append(cs); ssm_s.append(ss)
        else:
            w_in = jnp.pad(att_w_in[j], ((0, 0), (0, ATT_IN_PAD - att_w_in.shape[-1]))).astype(BF16)
            w_out = att_w_out[j].astype(BF16)
            prm = (att_norm[j], w_in, att_cmp_pos[j], att_cmp_w1[j], att_cmp_w2[j], w_out, slopes)
            hp, kp, lp, wp = nsa_prompt_layer(hp, bp, tp, *prm)
            hs, kq, lq, wq = nsa_sample_layer(hs, bs, ts, cache_cmp_kv[j], cache_slc_kv[j], state_win_kv[j],
                                              page_table, *prm)
            cmp_p.append(kp); slc_p.append(lp); win_p.append(wp)
            cmp_s.append(kq); slc_s.append(lq); win_s.append(wq)
        fin = final_norm if i == depth - 1 else None
        w_up = mlp_w_up[i].astype(BF16)
        w_down = mlp_w_down[i].astype(BF16)
        hp = mlp_block(hp, mlp_norm[i], w_up, w_down, fin, "mlp%d_p" % i)
        hs = mlp_block(hs, mlp_norm[i], w_up, w_down, fin, "mlp%d_s" % i)
    return (hp.reshape(bp, tp, D_MODEL), hs.reshape(bs, ts, D_MODEL),
            jnp.stack(ssm_p), jnp.stack(conv_p), jnp.stack(cmp_p), jnp.stack(slc_p), jnp.stack(win_p),
            jnp.stack(ssm_s), jnp.stack(conv_s), jnp.stack(cmp_s), jnp.stack(slc_s), jnp.stack(win_s))
```

```python
import functools

import numpy as np
import jax
import jax.numpy as jnp
from jax import lax
from jax.experimental import pallas as pl
from jax.experimental.pallas import tpu as pltpu

F32 = jnp.float32
BF16 = jnp.bfloat16

D_MODEL = 1024
D_INNER = 2048
SSM_HEADS = 32
SSM_HEAD_DIM = 64
SSM_GROUPS = 8
SSM_STATE = 128
CONV_WIDTH = 4
CONV_DIM = 4096
SSM_IN_PAD = 6400
SSD_CHUNK = 128
ATT_HEADS = 16
HEAD_DIM = 64
KV_HEADS = 4
GQA = ATT_HEADS // KV_HEADS
KV_DIM = KV_HEADS * HEAD_DIM
CMP_STRIDE = 16
CMP_HIDDEN = 128
SLC_BLOCK = 64
SLC_SHIFT = 6
SLC_TOP_N = 16
WINDOW = 512
N_BRANCH = 3
ATT_IN_PAD = 2688
PAGE_SIZE = 128
NORM_EPS = 1e-5
NEG_INF = -1e30
LANES = 128
VMEM_LIMIT = 56 * 1024 * 1024


def _cparams(sem):
    return pltpu.CompilerParams(dimension_semantics=sem, vmem_limit_bytes=VMEM_LIMIT)


def _dot(a, b):
    return jnp.dot(a, b, preferred_element_type=F32)


def _dot_nt(a, b):
    return lax.dot_general(a, b, (((1,), (1,)), ((), ())), preferred_element_type=F32)


def _split_dot(a, onehot, left=False, terms=3):
    acc = None
    rem = a
    for _ in range(terms):
        hi = rem.astype(BF16)
        part = _dot(onehot, hi) if left else _dot(hi, onehot)
        acc = part if acc is None else acc + part
        rem = rem - hi.astype(F32)
    return acc


def _silu(x):
    return x * (0.5 * jnp.tanh(0.5 * x) + 0.5)


def _mm_kernel(*refs, has_norm, act, has_res, has_fnorm):
    it = iter(refs)
    x_ref = next(it)
    g_ref = next(it) if has_norm else None
    w_ref = next(it)
    res_ref = next(it) if has_res else None
    fg_ref = next(it) if has_fnorm else None
    o_ref = next(it)
    xn_ref = next(it) if has_norm else None
    if has_norm:
        @pl.when(pl.program_id(1) == 0)
        def _():
            x = x_ref[...].astype(F32)
            ms = jnp.mean(x * x, axis=-1, keepdims=True)
            xn_ref[...] = (x * lax.rsqrt(ms + NORM_EPS) * g_ref[...]).astype(BF16)
        a = xn_ref[...]
    else:
        a = x_ref[...].astype(BF16)
    y = _dot(a, w_ref[...])
    if act == "relu2":
        y = jnp.square(jnp.maximum(y, 0.0))
    if has_res:
        y = y + res_ref[...]
    if has_fnorm:
        ms = jnp.mean(y * y, axis=-1, keepdims=True)
        y = y * lax.rsqrt(ms + NORM_EPS) * fg_ref[...]
    o_ref[...] = y.astype(o_ref.dtype)


def fused_matmul(x, w, *, gain=None, res=None, fgain=None, act=None, out_dtype=F32, tm, tn, name):
    m, k = x.shape
    n = w.shape[1]
    assert m % tm == 0 and n % tn == 0 and w.shape[0] == k
    assert fgain is None or tn == n
    in_specs = [pl.BlockSpec((tm, k), lambda i, j: (i, 0))]
    args = [x]
    if gain is not None:
        in_specs.append(pl.BlockSpec((1, k), lambda i, j: (0, 0)))
        args.append(gain.reshape(1, k).astype(F32))
    in_specs.append(pl.BlockSpec((k, tn), lambda i, j: (0, j)))
    args.append(w)
    if res is not None:
        in_specs.append(pl.BlockSpec((tm, tn), lambda i, j: (i, j)))
        args.append(res)
    if fgain is not None:
        in_specs.append(pl.BlockSpec((1, n), lambda i, j: (0, 0)))
        args.append(fgain.reshape(1, n).astype(F32))
    scratch = [pltpu.VMEM((tm, k), BF16)] if gain is not None else []
    kern = functools.partial(_mm_kernel, has_norm=gain is not None, act=act, has_res=res is not None,
                             has_fnorm=fgain is not None)
    return pl.pallas_call(
        kern,
        grid=(m // tm, n // tn),
        in_specs=in_specs,
        out_specs=pl.BlockSpec((tm, tn), lambda i, j: (i, j)),
        out_shape=jax.ShapeDtypeStruct((m, n), out_dtype),
        scratch_shapes=scratch,
        compiler_params=_cparams(("parallel", "arbitrary")),
        name=name,
    )(*args)


def _row_tile(m, pref):
    for t in (pref, 1024, 512, 256, 128, 64, 32, 16, 8):
        if t <= pref and m % t == 0:
            return t
    raise ValueError(m)


def mlp_block(h, norm_g, w_up, w_down, final_g, name):
    tm = _row_tile(h.shape[0], 1024)
    a = fused_matmul(h, w_up, gain=norm_g, act="relu2", out_dtype=BF16, tm=tm, tn=1024, name=name + "_up")
    if final_g is None:
        return fused_matmul(a, w_down, res=h, tm=tm, tn=512, name=name + "_down")
    tm = _row_tile(h.shape[0], 512)
    return fused_matmul(a, w_down, res=h, fgain=final_g, tm=tm, tn=D_MODEL, name=name + "_down")


def _ssd_kernel(z_ref, x_ref, bc_ref, dt_ref, cw_ref, cb_ref, dtb_ref, alog_ref, dsk_ref, gn_ref,
                c0_ref, s0_ref, tri_ref, ex_ref,
                y_ref, sout_ref,
                ext_ref, st_ref, xs_ref, b_ref, c_ref, *, t_valid, n_chunks):
    c = pl.program_id(1)
    L = SSD_CHUNK

    @pl.when(c == 0)
    def _():
        ext_ref[0:8, :] = c0_ref[0]
        st_ref[...] = s0_ref[0].T

    @pl.when(c > 0)
    def _():
        ext_ref[0:8, :] = ext_ref[L:L + 8, :]

    ext_ref[8:L + 8, 0:D_INNER] = x_ref[...]
    ext_ref[8:L + 8, D_INNER:CONV_DIM] = bc_ref[...]

    cblk = 512
    for i in range(CONV_DIM // cblk):
        sl = slice(i * cblk, (i + 1) * cblk)
        acc = cb_ref[:, sl]
        for k in range(CONV_WIDTH):
            acc = acc + cw_ref[k:k + 1, sl] * ext_ref[5 + k:5 + k + L, sl]
        v = _silu(acc)
        if i < 4:
            xs_ref[:, sl] = v
        elif i < 6:
            b_ref[:, (i - 4) * cblk:(i - 3) * cblk] = v
        else:
            c_ref[:, (i - 6) * cblk:(i - 5) * cblk] = v

    dt_raw = dt_ref[...] + dtb_ref[...]
    dt = jnp.maximum(dt_raw, 0.0) + jnp.log1p(jnp.exp(-jnp.abs(dt_raw)))
    if t_valid % L:
        row = c * L + lax.broadcasted_iota(jnp.int32, (L, LANES), 0)
        dt = jnp.where(row < t_valid, dt, 0.0)
    da = dt * (-jnp.exp(alog_ref[...]))
    acs = _split_dot(da, tri_ref[...], left=True)
    acs_t = acs.T
    expd = _split_dot(jnp.concatenate([acs, dt], axis=0), ex_ref[...])
    a_exp = expd[0:L]
    dt_exp = expd[L:2 * L]

    li = lax.broadcasted_iota(jnp.int32, (L, L), 0)
    si = lax.broadcasted_iota(jnp.int32, (L, L), 1)
    causal = si <= li
    first_half = si < SSM_HEAD_DIM
    gw = D_INNER // SSM_GROUPS
    for g in range(SSM_GROUPS):
        gs = slice(g * gw, (g + 1) * gw)
        bg = b_ref[:, g * SSM_STATE:(g + 1) * SSM_STATE]
        cg = c_ref[:, g * SSM_STATE:(g + 1) * SSM_STATE].astype(BF16)
        cbm = _dot_nt(cg, bg.astype(BF16))
        ag = a_exp[:, gs]
        xg = xs_ref[:, gs]
        xdt = xg * dt_exp[:, gs]
        a_last = ag[L - 1:L, :]
        y_off = _dot(cg, st_ref[:, gs].astype(BF16)) * jnp.exp(ag)
        ys = []
        for p in range(2):
            h0 = 4 * g + 2 * p
            xp = xdt[:, p * LANES:(p + 1) * LANES].astype(BF16)
            outs = []
            for h in (h0, h0 + 1):
                seg = acs[:, h:h + 1] - acs_t[h:h + 1, :]
                lm = jnp.exp(jnp.where(causal, seg, NEG_INF))
                outs.append(_dot((cbm * lm).astype(BF16), xp))
            ys.append(jnp.where(first_half, outs[0], outs[1]))
        y = jnp.concatenate(ys, axis=1) + y_off
        xd = (xdt * jnp.exp(a_last - ag)).astype(BF16)
        st_ref[:, gs] = st_ref[:, gs] * jnp.exp(a_last) + _dot(bg.T.astype(BF16), xd)
        zg = z_ref[:, gs]
        yy = (y + dsk_ref[:, gs] * xg) * _silu(zg)
        ms = jnp.mean(yy * yy, axis=-1, keepdims=True)
        y_ref[:, gs] = (yy * lax.rsqrt(ms + NORM_EPS) * gn_ref[:, gs]).astype(y_ref.dtype)

    @pl.when(c == n_chunks - 1)
    def _():
        sout_ref[0] = st_ref[...].T


def ssd_mixer(zx, conv0, st0, prm, *, batch, t_pad, t_valid, name):
    n_chunks = t_pad // SSD_CHUNK
    L = SSD_CHUNK
    conv_w, conv_b, dt_bias, a_log, d_skip, gate_norm = prm
    lane_pad = lambda v: jnp.pad(v.astype(F32), (0, LANES - SSM_HEADS)).reshape(1, LANES)
    tri = jnp.asarray(np.tril(np.ones((L, L), np.float32)), BF16)
    ex_np = np.zeros((LANES, D_INNER), np.float32)
    ex_np[np.arange(D_INNER) // SSM_HEAD_DIM, np.arange(D_INNER)] = 1.0
    ex = jnp.asarray(ex_np, BF16)
    c0 = jnp.pad(conv0.astype(F32), ((0, 0), (8 - (CONV_WIDTH - 1), 0), (0, 0)))
    row = lambda b, c: (b * n_chunks + c)
    const = lambda shape: pl.BlockSpec(shape, lambda b, c: (0,) * len(shape))
    kern = functools.partial(_ssd_kernel, t_valid=t_valid, n_chunks=n_chunks)
    y, s_out = pl.pallas_call(
        kern,
        grid=(batch, n_chunks),
        in_specs=[
            pl.BlockSpec((L, D_INNER), lambda b, c: (row(b, c), 0)),
            pl.BlockSpec((L, D_INNER), lambda b, c: (row(b, c), 1)),
            pl.BlockSpec((L, D_INNER), lambda b, c: (row(b, c), 2)),
            pl.BlockSpec((L, LANES), lambda b, c: (row(b, c), (D_INNER + CONV_DIM) // LANES)),
            const((CONV_WIDTH, CONV_DIM)), const((1, CONV_DIM)), const((1, LANES)), const((1, LANES)),
            const((1, D_INNER)), const((1, D_INNER)),
            pl.BlockSpec((1, 8, CONV_DIM), lambda b, c: (b, 0, 0)),
            pl.BlockSpec((1, D_INNER, SSM_STATE), lambda b, c: (b, 0, 0)),
            const((L, L)), const((LANES, D_INNER)),
        ],
        out_specs=[
            pl.BlockSpec((L, D_INNER), lambda b, c: (row(b, c), 0)),
            pl.BlockSpec((1, D_INNER, SSM_STATE), lambda b, c: (b, 0, 0)),
        ],
        out_shape=[
            jax.ShapeDtypeStruct((batch * t_pad, D_INNER), BF16),
            jax.ShapeDtypeStruct((batch, D_INNER, SSM_STATE), F32),
        ],
        scratch_shapes=[
            pltpu.VMEM((L + 8, CONV_DIM), F32),
            pltpu.VMEM((SSM_STATE, D_INNER), F32),
            pltpu.VMEM((L, D_INNER), F32),
            pltpu.VMEM((L, SSM_GROUPS * SSM_STATE), F32),
            pltpu.VMEM((L, SSM_GROUPS * SSM_STATE), F32),
        ],
        compiler_params=_cparams(("parallel", "arbitrary")),
        name=name,
    )(zx, zx, zx, zx,
      conv_w.astype(F32).T, conv_b.astype(F32).reshape(1, CONV_DIM), lane_pad(dt_bias), lane_pad(a_log),
      jnp.repeat(d_skip.astype(F32), SSM_HEAD_DIM).reshape(1, D_INNER), gate_norm.astype(F32).reshape(1, D_INNER),
      c0, st0.astype(F32).reshape(batch, D_INNER, SSM_STATE), tri, ex)
    return y, s_out.reshape(batch, SSM_HEADS, SSM_HEAD_DIM, SSM_STATE)


CMP_PAGES = 16


def _cmp_u_kernel(pt_ref, *refs, feature_major):
    del pt_ref
    pages = refs[:CMP_PAGES]
    perm_ref, w_ref, uf_ref, us_ref, y_ref = refs[CMP_PAGES:]
    regroup = _dot_nt if feature_major else _dot
    for p, page in enumerate(pages):
        yp = regroup(perm_ref[...], page[...].astype(BF16))
        for j in range(CMP_STRIDE):
            y_ref[j, p * 8:(p + 1) * 8, :] = yp[j * 8:(j + 1) * 8, :]
    rows = y_ref.shape[1]
    low = lax.broadcasted_iota(jnp.int32, (rows, LANES), 1) < HEAD_DIM
    accs = [None] * (2 * KV_HEADS)
    for a in range(CMP_STRIDE // 4):
        for q in range(4):
            blocks = [y_ref[4 * a + jj, :, q * LANES:(q + 1) * LANES] for jj in range(4)]
            swapped = [pltpu.roll(x, HEAD_DIM, axis=1) for x in blocks]
            for gl in range(2):
                if gl == 0:
                    pairs = [jnp.where(low, blocks[0], swapped[1]), jnp.where(low, blocks[2], swapped[3])]
                else:
                    pairs = [jnp.where(low, swapped[0], blocks[1]), jnp.where(low, swapped[2], blocks[3])]
                part = _dot(jnp.concatenate(pairs, axis=1).astype(BF16), w_ref[q // 2, a])
                k = 2 * q + gl
                accs[k] = part if accs[k] is None else accs[k] + part
    for k in range(2 * KV_HEADS):
        uf_ref[0, k] = accs[k][:, 0:CMP_HIDDEN]
        us_ref[0, k] = accs[k][:, CMP_HIDDEN:2 * CMP_HIDDEN]


def cmp_project(src2d, page_ids, col_block, w1, name, feature_major=False):
    batch, n_pages = page_ids.shape
    assert n_pages % CMP_PAGES == 0
    n_sub = n_pages * (PAGE_SIZE // CMP_STRIDE)
    rows = CMP_PAGES * (PAGE_SIZE // CMP_STRIDE)
    w1r = w1.astype(F32).reshape(2, 2 * CMP_STRIDE, HEAD_DIM, CMP_HIDDEN)
    wj = jnp.concatenate([w1r[:, :CMP_STRIDE], w1r[:, CMP_STRIDE:]], axis=-1)
    wpad = wj.reshape(2, CMP_STRIDE // 4, 4 * HEAD_DIM, 2 * CMP_HIDDEN).astype(BF16)
    perm = np.zeros((PAGE_SIZE, PAGE_SIZE), np.float32)
    for j in range(CMP_STRIDE):
        for s in range(PAGE_SIZE // CMP_STRIDE):
            perm[j * 8 + s, CMP_STRIDE * s + j] = 1.0
    perm = jnp.asarray(perm, BF16)

    def page_spec(p):
        shape = (4 * LANES, PAGE_SIZE) if feature_major else (PAGE_SIZE, 4 * LANES)
        return pl.BlockSpec(shape, lambda b, c, pt: (pt[b, c * CMP_PAGES + p], col_block))

    u_shape = jax.ShapeDtypeStruct((batch, 2 * KV_HEADS, n_sub, CMP_HIDDEN), F32)
    u_spec = pl.BlockSpec((1, 2 * KV_HEADS, rows, CMP_HIDDEN), lambda b, c, pt: (b, 0, c, 0))
    return pl.pallas_call(
        functools.partial(_cmp_u_kernel, feature_major=feature_major),
        grid_spec=pltpu.PrefetchScalarGridSpec(
            num_scalar_prefetch=1,
            grid=(batch, n_pages // CMP_PAGES),
            in_specs=[page_spec(p) for p in range(CMP_PAGES)] + [
                pl.BlockSpec((PAGE_SIZE, PAGE_SIZE), lambda b, c, pt: (0, 0)),
                pl.BlockSpec((2, CMP_STRIDE // 4, 4 * HEAD_DIM, 2 * CMP_HIDDEN), lambda b, c, pt: (0, 0, 0, 0)),
            ],
            out_specs=[u_spec, u_spec],
            scratch_shapes=[pltpu.VMEM((CMP_STRIDE, rows, 4 * LANES), F32)],
        ),
        out_shape=[u_shape, u_shape],
        compiler_params=_cparams(("parallel", "arbitrary")),
        name=name,
    )(page_ids, *([src2d] * CMP_PAGES), perm, wpad)


def _cmp_out_kernel(uf_ref, us_ref, pos_ref, w1_ref, w2_ref, o_ref, sh_ref, *, n_sub):
    sh_ref[0:n_sub, :] = us_ref[0, 0]
    sh_ref[n_sub:n_sub + 8, :] = jnp.zeros((8, CMP_HIDDEN), F32)
    pos_term = _dot(pos_ref[0].astype(BF16), w1_ref[0])[0:1, :]
    hid = jax.nn.gelu(uf_ref[0, 0] + sh_ref[pl.ds(1, n_sub), :] + pos_term)
    out = _dot(hid.astype(BF16), w2_ref[0])
    rows = lax.broadcasted_iota(jnp.int32, out.shape, 0)
    o_ref[0, 0] = jnp.where(rows < n_sub - 1, out, 0.0).astype(o_ref.dtype)


def cmp_finish(uf, us, pos_emb, w1, w2, name):
    batch, nkvg, n_sub, _ = uf.shape
    flat = 2 * CMP_STRIDE * HEAD_DIM
    pos = jnp.pad(pos_emb.astype(F32).reshape(2, 1, flat), ((0, 0), (0, 7), (0, 0)))
    u_spec = pl.BlockSpec((1, 1, n_sub, CMP_HIDDEN), lambda b, k: (b, k, 0, 0))
    return pl.pallas_call(
        functools.partial(_cmp_out_kernel, n_sub=n_sub),
        grid=(batch, nkvg),
        in_specs=[u_spec, u_spec,
                  pl.BlockSpec((1, 8, flat), lambda b, k: (k // KV_HEADS, 0, 0)),
                  pl.BlockSpec((1, flat, CMP_HIDDEN), lambda b, k: (k // KV_HEADS, 0, 0)),
                  pl.BlockSpec((1, CMP_HIDDEN, HEAD_DIM), lambda b, k: (k // KV_HEADS, 0, 0))],
        out_specs=pl.BlockSpec((1, 1, n_sub, HEAD_DIM), lambda b, k: (b, k, 0, 0)),
        out_shape=jax.ShapeDtypeStruct((batch, nkvg, n_sub, HEAD_DIM), BF16),
        scratch_shapes=[pltpu.VMEM((n_sub + 8, CMP_HIDDEN), F32)],
        compiler_params=_cparams(("parallel", "parallel")),
        name=name,
    )(uf, us, pos, w1.astype(BF16), w2.astype(BF16))


def _cmp_attn_kernel(q_ref, kc_ref, vc_ref, wsel_ref, o_ref, sel_ref, *rest, tq, ncp, nblk_pad, n_sel, q_base,
                     want_idx):
    qi = pl.program_id(2)
    use_ref = None if want_idx else rest[0]
    tpos = q_base + qi * tq + lax.broadcasted_iota(jnp.int32, (tq, 1), 0)
    c_end = CMP_STRIDE * lax.broadcasted_iota(jnp.int32, (1, ncp), 1) + (2 * CMP_STRIDE - 1)
    ok = c_end <= tpos
    any_ok = (tpos >= 2 * CMP_STRIDE - 1).astype(F32)
    kc = kc_ref[0, 0]
    vw = jnp.concatenate([vc_ref[0, 0], wsel_ref[...]], axis=1)
    p_slc = None
    for h in range(GQA):
        s = jnp.where(ok, _dot_nt(q_ref[0, h], kc), NEG_INF)
        e = jnp.exp2(s - jnp.max(s, axis=-1, keepdims=True))
        r = _dot(e.astype(BF16), vw)
        inv = any_ok / r[:, HEAD_DIM:HEAD_DIM + 1]
        o_ref[0, h] = r[:, 0:HEAD_DIM] * inv
        part = r[:, LANES:] * inv
        p_slc = part if p_slc is None else p_slc + part
    tq_l = max(tq, LANES)
    if tq < LANES:
        p_slc = jnp.concatenate([p_slc, jnp.zeros((tq_l - tq, nblk_pad), F32)], axis=0)
    sc = p_slc.T
    blk = lax.broadcasted_iota(jnp.int32, (nblk_pad, tq_l), 0)
    tp = q_base + qi * tq + lax.broadcasted_iota(jnp.int32, (nblk_pad, tq_l), 1)
    cur = jnp.right_shift(tp, SLC_SHIFT)
    forced = (blk == 0) | (blk == cur) | (blk == cur - 1)
    reach = blk * SLC_BLOCK <= tp
    sc = jnp.where(forced, 1e30, jnp.where(reach, sc, -1.0))
    sc = jnp.where(blk < n_sel, sc, -3.0)
    removed = -2.0
    idx_rows = []
    for _ in range(min(SLC_TOP_N, n_sel)):
        mx = jnp.max(sc, axis=0, keepdims=True)
        idx = jnp.min(jnp.where(sc == mx, blk, nblk_pad), axis=0, keepdims=True)
        sc = jnp.where(blk == idx, removed, sc)
        idx_rows.append(idx)
    if want_idx:
        sel_ref[0, 0] = jnp.concatenate(idx_rows, axis=0)
    else:
        chosen_t = (sc == removed).astype(F32).T[0:tq]
        sel_ref[0, 0] = ((chosen_t - 1.0) * 1e30).astype(sel_ref.dtype)
        used = jnp.max(chosen_t, axis=0, keepdims=True)
        use_ref[0, 0, 0] = jnp.broadcast_to(used, (8, nblk_pad))


LOG2E = 1.4426950408889634


def _query_feats(q_h, slopes):
    sl = slopes * LOG2E
    s1 = sl.astype(BF16)
    r1 = sl - s1.astype(F32)
    s2 = r1.astype(BF16)
    s3 = (r1 - s2.astype(F32)).astype(BF16)
    feats = jnp.stack([s1, s2, s3, s1, s2, s3, jnp.ones_like(s1)], axis=-1)
    return _with_feats((q_h * (LOG2E * HEAD_DIM ** -0.5)).astype(BF16), feats[None, :, None, :], LANES)


def _pos_feats(pos, invalid=None):
    lo = jnp.bitwise_and(pos, 255)
    hi = pos - lo
    inv = jnp.zeros_like(pos, F32) if invalid is None else jnp.where(invalid, NEG_INF, 0.0)
    return jnp.concatenate([jnp.stack([lo, lo, lo, hi, hi, hi], axis=-1).astype(BF16), inv.astype(BF16)[..., None]],
                           axis=-1)


def _with_feats(x, feats, width):
    lead = x.shape[:-1]
    f = jnp.broadcast_to(feats, lead + feats.shape[-1:]).astype(BF16)
    pad = jnp.zeros(lead + (width - x.shape[-1] - f.shape[-1],), BF16)
    return jnp.concatenate([x.astype(BF16), f, pad], axis=-1)


def _with_ones(v):
    return _with_feats(v, jnp.ones((1,), BF16), LANES)


def cmp_attention(qa, kvc, *, tq, n_sel, q_base, want_idx, name):
    b, _, t, _ = qa.shape
    ncp = kvc.shape[2]
    nblk_pad = -(-n_sel // LANES) * LANES
    c_end = CMP_STRIDE * jnp.arange(ncp, dtype=jnp.int32) + (2 * CMP_STRIDE - 1)
    kca = _with_feats(kvc[:, :KV_HEADS], _pos_feats(c_end), LANES)
    vca = _with_ones(kvc[:, KV_HEADS:])
    c = np.arange(ncp)[:, None]
    j = np.arange(nblk_pad)[None, :]
    wsel = ((c // 4 == j).astype(np.float32) + ((c + 1) // 4 == j).astype(np.float32)) * (c < ncp - 1)
    tq_l = max(tq, LANES)
    if want_idx:
        sel_shape = jax.ShapeDtypeStruct((b, KV_HEADS, SLC_TOP_N, (t // tq) * tq_l), jnp.int32)
        sel_spec = pl.BlockSpec((1, 1, SLC_TOP_N, tq_l), lambda bi, g, qi: (bi, g, 0, qi))
    else:
        sel_shape = jax.ShapeDtypeStruct((b, KV_HEADS, t, nblk_pad), BF16)
        sel_spec = pl.BlockSpec((1, 1, tq, nblk_pad), lambda bi, g, qi: (bi, g, qi, 0))
    kern = functools.partial(_cmp_attn_kernel, tq=tq, ncp=ncp, nblk_pad=nblk_pad, n_sel=n_sel, q_base=q_base,
                             want_idx=want_idx)
    out_specs = [pl.BlockSpec((1, GQA, tq, HEAD_DIM), lambda bi, g, qi: (bi, g, qi, 0)), sel_spec]
    out_shape = [jax.ShapeDtypeStruct((b, ATT_HEADS, t, HEAD_DIM), F32), sel_shape]
    if not want_idx:
        out_specs.append(pl.BlockSpec((1, 1, 1, 8, nblk_pad), lambda bi, g, qi: (bi, g, qi, 0, 0)))
        out_shape.append(jax.ShapeDtypeStruct((b, KV_HEADS, t // tq, 8, nblk_pad), F32))
    return pl.pallas_call(
        kern,
        grid=(b, KV_HEADS, t // tq),
        in_specs=[
            pl.BlockSpec((1, GQA, tq, LANES), lambda bi, g, qi: (bi, g, qi, 0)),
            pl.BlockSpec((1, 1, ncp, LANES), lambda bi, g, qi: (bi, g, 0, 0)),
            pl.BlockSpec((1, 1, ncp, LANES), lambda bi, g, qi: (bi, g, 0, 0)),
            pl.BlockSpec((ncp, nblk_pad), lambda bi, g, qi: (0, 0)),
        ],
        out_specs=out_specs,
        out_shape=out_shape,
        compiler_params=_cparams(("parallel", "parallel", "parallel")),
        name=name,
    )(qa, kca, vca, jnp.asarray(wsel, BF16))


SEL_FEAT = 256


def _sel_kernel(use_ref, qa_ref, bias_ref, ka_ref, v_ref, o_ref, m_ref, acc_ref, *, tq, tk):
    bi = pl.program_id(0)
    g = pl.program_id(1)
    qi = pl.program_id(2)
    n_qt = pl.num_programs(2)
    per = tq // tk
    n_kc = n_qt * per
    m_ref[...] = jnp.full(m_ref.shape, NEG_INF, F32)
    acc_ref[...] = jnp.zeros(acc_ref.shape, F32)
    rows = lax.broadcasted_iota(jnp.int32, (tq, tk), 0)
    cols = lax.broadcasted_iota(jnp.int32, (tq, tk), 1)

    def chunk(j, diag_offset):
        k0 = pl.multiple_of(j * tk, tk)
        ka = ka_ref[0, 0, pl.ds(k0, tk), :]
        va = v_ref[0, 0, pl.ds(k0, tk), :]
        bias = bias_ref[0, 0]
        for h in range(GQA):
            qf = jnp.concatenate([qa_ref[0, h], bias], axis=1)
            s = _dot_nt(qf, ka)
            if diag_offset is not None:
                s = jnp.where(cols + diag_offset <= rows, s, NEG_INF)
            m_prev = m_ref[h]
            m_new = jnp.maximum(m_prev, jnp.max(s, axis=-1, keepdims=True))
            p = jnp.exp2(s - jnp.concatenate([m_new] * (tk // LANES), axis=1))
            acc_ref[h] = jnp.exp2(m_prev - m_new) * acc_ref[h] + _dot(p.astype(BF16), va)
            m_ref[h] = m_new

    def body(j, carry):
        @pl.when(use_ref[((bi * KV_HEADS + g) * n_qt + qi) * n_kc + j] > 0)
        def _():
            chunk(j, None)
        return carry

    lax.fori_loop(0, qi * per, body, 0)
    for c in range(per):
        chunk(qi * per + c, c * tk)
    for h in range(GQA):
        acc = acc_ref[h]
        o_ref[0, h] = acc[:, 0:HEAD_DIM] / acc[:, HEAD_DIM:HEAD_DIM + 1]


def sel_attention_dense(chunk_used, qa, bias, ka, va, *, tq, tk, name):
    b, _, t, _ = qa.shape
    return pl.pallas_call(
        functools.partial(_sel_kernel, tq=tq, tk=tk),
        grid_spec=pltpu.PrefetchScalarGridSpec(
            num_scalar_prefetch=1,
            grid=(b, KV_HEADS, t // tq),
            in_specs=[
                pl.BlockSpec((1, GQA, tq, LANES), lambda bi, g, qi, *_: (bi, g, qi, 0)),
                pl.BlockSpec((1, 1, tq, LANES), lambda bi, g, qi, *_: (bi, g, qi, 0)),
                pl.BlockSpec((1, 1, t, SEL_FEAT), lambda bi, g, qi, *_: (bi, g, 0, 0)),
                pl.BlockSpec((1, 1, t, LANES), lambda bi, g, qi, *_: (bi, g, 0, 0)),
            ],
            out_specs=pl.BlockSpec((1, GQA, tq, HEAD_DIM), lambda bi, g, qi, *_: (bi, g, qi, 0)),
            scratch_shapes=[pltpu.VMEM((GQA, tq, LANES), F32), pltpu.VMEM((GQA, tq, LANES), F32)],
        ),
        out_shape=jax.ShapeDtypeStruct((b, ATT_HEADS, t, HEAD_DIM), F32),
        compiler_params=_cparams(("parallel", "parallel", "arbitrary")),
        name=name,
    )(chunk_used.reshape(-1), qa, bias, ka, va)


def _sel_gather_kernel(idx_ref, pt_ref, q_ref, slope_ref, cache_ref, new_ref, o_ref, kvbuf, sem, *,
                       n_tok, tq, past_len, n_steps):
    b = pl.program_id(0)
    g = pl.program_id(1)
    step = b * KV_HEADS + g
    slot = step % 2
    new_blk = past_len // SLC_BLOCK
    halves = PAGE_SIZE // SLC_BLOCK

    def block_copy(bb, gg, sl, t, n, blk, newest):
        dst = kvbuf.at[sl, t, :, :, pl.ds(n * PAGE_SIZE, PAGE_SIZE)]
        if newest:
            src = new_ref.at[bb, :, gg]
        else:
            page = pt_ref[bb * (past_len // PAGE_SIZE) + jnp.minimum(blk, new_blk - 1) // halves]
            src = cache_ref.at[page, :, gg]
        return pltpu.make_async_copy(src, dst, sem.at[sl])

    def for_each_block(bb, gg, fn):
        for t in range(n_tok):
            for n in range(SLC_TOP_N):
                fn(t, n, idx_ref[((bb * KV_HEADS + gg) * SLC_TOP_N + n) * n_tok + t])

    def start(bb, gg, sl):
        def one(t, n, blk):
            @pl.when(blk >= new_blk)
            def _():
                block_copy(bb, gg, sl, t, n, blk, True).start()

            @pl.when(blk < new_blk)
            def _():
                block_copy(bb, gg, sl, t, n, blk, False).start()
        for_each_block(bb, gg, one)

    @pl.when(step == 0)
    def _():
        start(b, g, slot)

    @pl.when(step + 1 < n_steps)
    def _():
        nxt = step + 1
        start(nxt // KV_HEADS, nxt % KV_HEADS, 1 - slot)

    for_each_block(b, g, lambda t, n, blk: block_copy(b, g, slot, t, n, blk, False).wait())

    r = GQA * tq
    q = (q_ref[0].reshape(r, HEAD_DIM) * (HEAD_DIM ** -0.5)).astype(BF16)
    nk = SLC_TOP_N * PAGE_SIZE
    lane = lax.broadcasted_iota(jnp.int32, (1, nk), 1)
    trow = lax.broadcasted_iota(jnp.int32, (tq, 1), 0)
    tpos = past_len + trow
    acc = jnp.zeros((GQA, tq, HEAD_DIM), F32)
    in_page = jnp.bitwise_and(lane, PAGE_SIZE - 1)
    half_of_lane = jnp.right_shift(in_page, SLC_SHIFT)
    slot_of_lane = jnp.right_shift(lane, SLC_SHIFT + 1)
    for t in range(n_tok):
        kpos = in_page
        chosen_half = jnp.zeros((1, nk), jnp.int32)
        for n in range(SLC_TOP_N):
            blk = idx_ref[((b * KV_HEADS + g) * SLC_TOP_N + n) * n_tok + t]
            here = slot_of_lane == n
            kpos = jnp.where(here, kpos + (blk // halves) * PAGE_SIZE, kpos)
            chosen_half = jnp.where(here, blk % halves, chosen_half)
        d = (tpos - kpos).astype(F32)
        ok = (d >= 0.0) & (half_of_lane == chosen_half)
        s = _dot(q, kvbuf[slot, t, 0].astype(BF16))
        s3 = s.reshape(GQA, tq, nk) - slope_ref[0] * d[None]
        s3 = jnp.where(ok[None], s3, NEG_INF)
        m = jnp.max(s3, axis=-1, keepdims=True)
        e = jnp.exp(s3 - m)
        p = e / jnp.sum(e, axis=-1, keepdims=True)
        o = _dot_nt(p.reshape(r, nk).astype(BF16), kvbuf[slot, t, 1].astype(BF16)).reshape(GQA, tq, HEAD_DIM)
        acc = jnp.where((trow == t)[None], o, acc)
    o_ref[0] = acc


def sel_attention_gather(q_h, idx, page_table, cache, new_kv, slopes, *, n_tok, past_len, name):
    b, _, tq, _ = q_h.shape
    n_steps = b * KV_HEADS
    nk = SLC_TOP_N * PAGE_SIZE
    kern = functools.partial(_sel_gather_kernel, n_tok=n_tok, tq=tq, past_len=past_len, n_steps=n_steps)
    return pl.pallas_call(
        kern,
        grid_spec=pltpu.PrefetchScalarGridSpec(
            num_scalar_prefetch=2,
            grid=(b, KV_HEADS),
            in_specs=[
                pl.BlockSpec((1, GQA, tq, HEAD_DIM), lambda bi, g, *_: (bi, g, 0, 0)),
                pl.BlockSpec((1, GQA, 1, 1), lambda bi, g, *_: (g, 0, 0, 0)),
                pl.BlockSpec(memory_space=pl.ANY),
                pl.BlockSpec(memory_space=pl.ANY),
            ],
            out_specs=pl.BlockSpec((1, GQA, tq, HEAD_DIM), lambda bi, g, *_: (bi, g, 0, 0)),
            scratch_shapes=[pltpu.VMEM((2, n_tok, 2, HEAD_DIM, nk), F32), pltpu.SemaphoreType.DMA((2,))],
        ),
        out_shape=jax.ShapeDtypeStruct((b, ATT_HEADS, tq, HEAD_DIM), F32),
        compiler_params=_cparams(("arbitrary", "arbitrary")),
        name=name,
    )(idx.reshape(-1), page_table.reshape(-1), q_h, slopes.reshape(KV_HEADS, GQA, 1, 1), cache, new_kv)


def _win_kernel(q_ref, k_ref, v_ref, oc_ref, os_ref, gl_ref, o_ref, *, tq, nk):
    qi = pl.program_id(2)
    r0 = pl.multiple_of(qi * tq, tq)
    kw = k_ref[0, 0, pl.ds(r0, nk), :]
    vw = v_ref[0, 0, pl.ds(r0, nk), :]
    blockwise = tq % LANES == 0 and WINDOW % tq == 0
    if blockwise:
        rows = lax.broadcasted_iota(jnp.int32, (tq, tq), 0)
        cols = lax.broadcasted_iota(jnp.int32, (tq, tq), 1)
        old_ok = cols > rows
        new_ok = cols <= rows
    else:
        di = (WINDOW + lax.broadcasted_iota(jnp.int32, (tq, 1), 0)) - lax.broadcasted_iota(jnp.int32, (1, nk), 1)
        ok = (di >= 0) & (di < WINDOW)
    for h in range(GQA):
        s = _dot_nt(q_ref[0, h], kw)
        if blockwise:
            s = jnp.concatenate([jnp.where(old_ok, s[:, 0:tq], NEG_INF), s[:, tq:nk - tq],
                                 jnp.where(new_ok, s[:, nk - tq:nk], NEG_INF)], axis=1)
        else:
            s = jnp.where(ok, s, NEG_INF)
        e = jnp.exp2(s - jnp.max(s, axis=-1, keepdims=True))
        r = _dot(e.astype(BF16), vw)
        ow = r[:, 0:HEAD_DIM] / r[:, HEAD_DIM:HEAD_DIM + 1]
        gates = 1.0 / (1.0 + jnp.exp(-gl_ref[0, h]))
        o_ref[0, h] = gates[:, 0:1] * oc_ref[0, h] + gates[:, 1:2] * os_ref[0, h] + gates[:, 2:3] * ow


def win_attention_merge(qa, kw, vw, o_c, o_s, gate_logits, *, tq, k_valid_from, name):
    b, _, t, _ = qa.shape
    nk = tq + WINDOW
    lk = kw.shape[2]
    assert lk >= t + WINDOW
    krow = jnp.arange(lk, dtype=jnp.int32)
    kwa = _with_feats(kw, _pos_feats(krow, invalid=krow < k_valid_from), LANES)
    vwa = _with_ones(vw)
    head_spec = lambda w: pl.BlockSpec((1, GQA, tq, w), lambda bi, g, qi: (bi, g, qi, 0))
    kv_spec = pl.BlockSpec((1, 1, lk, LANES), lambda bi, g, qi: (bi, g, 0, 0))
    return pl.pallas_call(
        functools.partial(_win_kernel, tq=tq, nk=nk),
        grid=(b, KV_HEADS, t // tq),
        in_specs=[head_spec(LANES), kv_spec, kv_spec,
                  head_spec(HEAD_DIM), head_spec(HEAD_DIM), head_spec(N_BRANCH)],
        out_specs=head_spec(HEAD_DIM),
        out_shape=jax.ShapeDtypeStruct((b, ATT_HEADS, t, HEAD_DIM), F32),
        compiler_params=_cparams(("parallel", "parallel", "parallel")),
        name=name,
    )(qa, kwa, vwa, o_c, o_s, gate_logits)


def _heads(x2d, b, t, n_heads):
    return x2d.reshape(b, t, n_heads, HEAD_DIM).transpose(0, 2, 1, 3)


def _kv_stack(qkv, b, t, col):
    return qkv[:, col:col + 2 * KV_DIM].reshape(b, t, 2, KV_HEADS, HEAD_DIM)


def alibi_slopes():
    return 2.0 ** (-8.0 * jnp.arange(1, ATT_HEADS + 1, dtype=jnp.float32) / ATT_HEADS)


def nsa_prompt_layer(h2d, b, t, norm_g, w_in, cmp_pos, cmp_w1, cmp_w2, w_out, slopes):
    tm = _row_tile(h2d.shape[0], 1024)
    qkv = fused_matmul(h2d, w_in, gain=norm_g, tm=tm, tn=ATT_IN_PAD // 3, name="att_in_p")
    q_h = _heads(qkv[:, :D_MODEL], b, t, ATT_HEADS)
    gl_h = qkv[:, D_MODEL + 6 * KV_DIM:D_MODEL + 6 * KV_DIM + ATT_HEADS * N_BRANCH]
    gl_h = gl_h.reshape(b, t, ATT_HEADS, N_BRANCH).transpose(0, 2, 1, 3)

    n_pages = t // PAGE_SIZE
    page_ids = (jnp.arange(b, dtype=jnp.int32)[:, None] * n_pages + jnp.arange(n_pages, dtype=jnp.int32)[None, :])
    uf, us = cmp_project(qkv, page_ids, D_MODEL // (2 * KV_DIM), cmp_w1, "cmp_u_p")
    kvc = cmp_finish(uf, us, cmp_pos, cmp_w1, cmp_w2, "cmp_out_p")

    n_sel = t // SLC_BLOCK
    tq = 256
    qa = _query_feats(q_h, slopes)
    o_c, sel_bias, blk_used = cmp_attention(qa, kvc, tq=tq, n_sel=n_sel, q_base=0, want_idx=False, name="cmp_attn_p")
    tq_sel, tk_sel = min(512, t), 256
    used = blk_used[:, :, :, 0, :n_sel].reshape(b, KV_HEADS, t // tq_sel, tq_sel // tq, t // tk_sel, tk_sel // SLC_BLOCK)
    chunk_used = (used.max(axis=(3, 5)) > 0.0).astype(jnp.int32)

    nb = sel_bias.shape[-1]
    assert LANES + nb == SEL_FEAT
    ks = _heads(qkv[:, D_MODEL + 2 * KV_DIM:D_MODEL + 3 * KV_DIM], b, t, KV_HEADS).astype(BF16)
    vs = _heads(qkv[:, D_MODEL + 3 * KV_DIM:D_MODEL + 4 * KV_DIM], b, t, KV_HEADS).astype(BF16)
    kpos = jnp.arange(t, dtype=jnp.int32)
    onehot = jax.nn.one_hot(kpos // SLC_BLOCK, nb, dtype=BF16)
    ka = jnp.concatenate([_with_feats(ks, _pos_feats(kpos), LANES),
                          jnp.broadcast_to(onehot, (b, KV_HEADS) + onehot.shape)], axis=-1)
    o_s = sel_attention_dense(chunk_used, qa, sel_bias, ka, _with_ones(vs), tq=tq_sel, tk=tk_sel, name="sel_attn_p")

    front = ((0, 0), (0, 0), (WINDOW, 0), (0, 0))
    kw = jnp.pad(_heads(qkv[:, D_MODEL + 4 * KV_DIM:D_MODEL + 5 * KV_DIM], b, t, KV_HEADS).astype(BF16), front)
    vw = jnp.pad(_heads(qkv[:, D_MODEL + 5 * KV_DIM:D_MODEL + 6 * KV_DIM], b, t, KV_HEADS).astype(BF16), front)
    o = win_attention_merge(qa, kw, vw, o_c, o_s, gl_h, tq=tq, k_valid_from=WINDOW, name="win_attn_p")
    o2d = o.transpose(0, 2, 1, 3).reshape(b * t, D_MODEL)
    h_new = fused_matmul(o2d, w_out, res=h2d, tm=tm, tn=512, name="att_out_p")
    win_rows = min(WINDOW, t)
    return (h_new, _kv_stack(qkv, b, t, D_MODEL), _kv_stack(qkv, b, t, D_MODEL + 2 * KV_DIM),
            _kv_stack(qkv, b, t, D_MODEL + 4 * KV_DIM)[:, t - win_rows:])


def nsa_sample_layer(h2d, b, t, cache_cmp, cache_slc, win_buf, page_table, norm_g, w_in, cmp_pos, cmp_w1, cmp_w2,
                     w_out, slopes):
    n_pages = page_table.shape[1]
    past_len = n_pages * PAGE_SIZE
    tq = 16
    assert t <= tq and t <= 2 * CMP_STRIDE - 1
    qkv = fused_matmul(h2d, w_in, gain=norm_g, tm=h2d.shape[0], tn=ATT_IN_PAD // 3, name="att_in_s")
    tpad = ((0, 0), (0, 0), (0, tq - t), (0, 0))
    q_h = jnp.pad(_heads(qkv[:, :D_MODEL], b, t, ATT_HEADS), tpad)
    gl_h = qkv[:, D_MODEL + 6 * KV_DIM:D_MODEL + 6 * KV_DIM + ATT_HEADS * N_BRANCH]
    gl_h = jnp.pad(gl_h.reshape(b, t, ATT_HEADS, N_BRANCH).transpose(0, 2, 1, 3), tpad)
    new_cmp = _kv_stack(qkv, b, t, D_MODEL)
    new_slc = _kv_stack(qkv, b, t, D_MODEL + 2 * KV_DIM)
    new_win = _kv_stack(qkv, b, t, D_MODEL + 4 * KV_DIM)

    cmp_fm = cache_cmp.transpose(0, 2, 3, 4, 1).reshape(-1, PAGE_SIZE)
    slc_fm = cache_slc.transpose(0, 2, 3, 4, 1)
    uf, us = cmp_project(cmp_fm, page_table, 0, cmp_w1, "cmp_u_s", feature_major=True)
    kvc = cmp_finish(uf, us, cmp_pos, cmp_w1, cmp_w2, "cmp_out_s")
    l_all = past_len + t
    n_sel = -(-l_all // SLC_BLOCK)
    qa = _query_feats(q_h, slopes)
    o_c, idx = cmp_attention(qa, kvc, tq=tq, n_sel=n_sel, q_base=past_len, want_idx=True, name="cmp_attn_s")

    new_page = jnp.pad(new_slc, ((0, 0), (0, PAGE_SIZE - t), (0, 0), (0, 0), (0, 0)))
    o_s = sel_attention_gather(q_h, idx[..., :t], page_table, slc_fm, new_page.transpose(0, 2, 3, 4, 1), slopes,
                               n_tok=t, past_len=past_len, name="sel_attn_s")

    w_rows = win_buf.shape[1]
    win_all = jnp.concatenate([win_buf.astype(F32), new_win], axis=1)
    lk = tq + WINDOW
    shift = WINDOW - w_rows
    kvw = jnp.pad(win_all, ((0, 0), (shift, lk - shift - w_rows - t), (0, 0), (0, 0), (0, 0))).astype(BF16)
    kw = kvw[:, :, 0].transpose(0, 2, 1, 3)
    vw = kvw[:, :, 1].transpose(0, 2, 1, 3)
    o = win_attention_merge(qa, kw, vw, o_c, o_s, gl_h, tq=tq, k_valid_from=shift, name="win_attn_s")
    o2d = o[:, :, :t].transpose(0, 2, 1, 3).reshape(b * t, D_MODEL)
    h_new = fused_matmul(o2d, w_out, res=h2d, tm=h2d.shape[0], tn=512, name="att_out_s")
    return h_new, new_cmp, new_slc, win_all[:, -w_rows:]


def ssm_layer(h2d, b, t, conv_state, ssm_state, norm_g, w_in, conv_prm, w_out, tag):
    t_pad = -(-t // SSD_CHUNK) * SSD_CHUNK
    tm = _row_tile(h2d.shape[0], 1024)
    zx = fused_matmul(h2d, w_in, gain=norm_g, tm=tm, tn=1280, name="ssm_in_" + tag)
    keep = CONV_WIDTH - 1
    tail = zx.reshape(b, t, SSM_IN_PAD)[:, max(t - keep, 0):, D_INNER:D_INNER + CONV_DIM]
    conv_out = tail if t >= keep else jnp.concatenate([conv_state.astype(F32), tail], axis=1)[:, -keep:]
    if t_pad != t:
        zx = jnp.pad(zx.reshape(b, t, SSM_IN_PAD), ((0, 0), (0, t_pad - t), (0, 0))).reshape(b * t_pad, SSM_IN_PAD)
    y, s_new = ssd_mixer(zx, conv_state, ssm_state, conv_prm, batch=b, t_pad=t_pad, t_valid=t, name="ssd_" + tag)
    if t_pad != t:
        y = y.reshape(b, t_pad, D_INNER)[:, :t].reshape(b * t, D_INNER)
    h_new = fused_matmul(y, w_out, res=h2d, tm=tm, tn=512, name="ssm_out_" + tag)
    return h_new, conv_out, s_new


def kernel(x_prompt, x_sample, state_ssm, state_conv, cache_cmp_kv, cache_slc_kv, state_win_kv, page_table, ssm_norm, ssm_w_in, ssm_conv_w, ssm_conv_b, ssm_dt_bias, ssm_a_log, ssm_d, ssm_gate_norm, ssm_w_out, att_norm, att_w_in, att_cmp_pos, att_cmp_w1, att_cmp_w2, att_w_out, mlp_norm, mlp_w_up, mlp_w_down, final_norm):
    bp, tp, _ = x_prompt.shape
    bs, ts, _ = x_sample.shape
    depth = mlp_norm.shape[0]
    slopes = alibi_slopes()
    hp = x_prompt.reshape(bp * tp, D_MODEL)
    hs = x_sample.reshape(bs * ts, D_MODEL)
    ssm_p, conv_p, ssm_s, conv_s = [], [], [], []
    cmp_p, slc_p, win_p, cmp_s, slc_s, win_s = [], [], [], [], [], []
    for i in range(depth):
        j = i // 2
        if i % 2 == 0:
            w_in = jnp.pad(ssm_w_in[j], ((0, 0), (0, SSM_IN_PAD - ssm_w_in.shape[-1]))).astype(BF16)
            w_out = ssm_w_out[j].astype(BF16)
            prm = (ssm_conv_w[j], ssm_conv_b[j], ssm_dt_bias[j], ssm_a_log[j], ssm_d[j], ssm_gate_norm[j])
            zc = jnp.zeros((bp, CONV_WIDTH - 1, CONV_DIM), F32)
            zs = jnp.zeros((bp, SSM_HEADS, SSM_HEAD_DIM, SSM_STATE), F32)
            hp, cp, sp = ssm_layer(hp, bp, tp, zc, zs, ssm_norm[j], w_in, prm, w_out, "p")
            hs, cs, ss = ssm_layer(hs, bs, ts, state_conv[j], state_ssm[j], ssm_norm[j], w_in, prm, w_out, "s")
            conv_p.append(cp); ssm_p.append(sp); conv_s.append(cs); ssm_s.append(ss)
        else:
            w_in = jnp.pad(att_w_in[j], ((0, 0), (0, ATT_IN_PAD - att_w_in.shape[-1]))).astype(BF16)
            w_out = att_w_out[j].astype(BF16)
            prm = (att_norm[j], w_in, att_cmp_pos[j], att_cmp_w1[j], att_cmp_w2[j], w_out, slopes)
            hp, kp, lp, wp = nsa_prompt_layer(hp, bp, tp, *prm)
            hs, kq, lq, wq = nsa_sample_layer(hs, bs, ts, cache_cmp_kv[j], cache_slc_kv[j], state_win_kv[j],
                                              page_table, *prm)
            cmp_p.append(kp); slc_p.append(lp); win_p.append(wp)
            cmp_s.append(kq); slc_s.append(lq); win_s.append(wq)
        fin = final_norm if i == depth - 1 else None
        w_up = mlp_w_up[i].astype(BF16)
        w_down = mlp_w_down[i].astype(BF16)
        hp = mlp_block(hp, mlp_norm[i], w_up, w_down, fin, "mlp%d_p" % i)
        hs = mlp_block(hs, mlp_norm[i], w_up, w_down, fin, "mlp%d_s" % i)
    return (hp.reshape(bp, tp, D_MODEL), hs.reshape(bs, ts, D_MODEL),
            jnp.stack(ssm_p), jnp.stack(conv_p), jnp.stack(cmp_p), jnp.stack(slc_p), jnp.stack(win_p),
            jnp.stack(ssm_s), jnp.stack(conv_s), jnp.stack(cmp_s), jnp.stack(slc_s), jnp.stack(win_s))
```

```python
import functools

import numpy as np
import jax
import jax.numpy as jnp
from jax import lax
from jax.experimental import pallas as pl
from jax.experimental.pallas import tpu as pltpu

F32 = jnp.float32
BF16 = jnp.bfloat16

D_MODEL = 1024
D_INNER = 2048
SSM_HEADS = 32
SSM_HEAD_DIM = 64
SSM_GROUPS = 8
SSM_STATE = 128
CONV_WIDTH = 4
CONV_DIM = 4096
SSM_IN_PAD = 6400
SSD_CHUNK = 128
ATT_HEADS = 16
HEAD_DIM = 64
KV_HEADS = 4
GQA = ATT_HEADS // KV_HEADS
KV_DIM = KV_HEADS * HEAD_DIM
CMP_STRIDE = 16
CMP_HIDDEN = 128
SLC_BLOCK = 64
SLC_SHIFT = 6
SLC_TOP_N = 16
WINDOW = 512
N_BRANCH = 3
ATT_IN_PAD = 2688
PAGE_SIZE = 128
NORM_EPS = 1e-5
NEG_INF = -1e30
LANES = 128
VMEM_LIMIT = 56 * 1024 * 1024


def _cparams(sem):
    return pltpu.CompilerParams(dimension_semantics=sem, vmem_limit_bytes=VMEM_LIMIT)


def _dot(a, b):
    return jnp.dot(a, b, preferred_element_type=F32)


def _dot_nt(a, b):
    return lax.dot_general(a, b, (((1,), (1,)), ((), ())), preferred_element_type=F32)


def _split_dot(a, onehot, left=False, terms=3):
    acc = None
    rem = a
    for _ in range(terms):
        hi = rem.astype(BF16)
        part = _dot(onehot, hi) if left else _dot(hi, onehot)
        acc = part if acc is None else acc + part
        rem = rem - hi.astype(F32)
    return acc


def _silu(x):
    return x * (0.5 * jnp.tanh(0.5 * x) + 0.5)


def _mm_kernel(*refs, has_norm, act, has_res, has_fnorm):
    it = iter(refs)
    x_ref = next(it)
    g_ref = next(it) if has_norm else None
    w_ref = next(it)
    res_ref = next(it) if has_res else None
    fg_ref = next(it) if has_fnorm else None
    o_ref = next(it)
    xn_ref = next(it) if has_norm else None
    if has_norm:
        @pl.when(pl.program_id(1) == 0)
        def _():
            x = x_ref[...].astype(F32)
            ms = jnp.mean(x * x, axis=-1, keepdims=True)
            xn_ref[...] = (x * lax.rsqrt(ms + NORM_EPS) * g_ref[...]).astype(BF16)
        a = xn_ref[...]
    else:
        a = x_ref[...].astype(BF16)
    y = _dot(a, w_ref[...])
    if act == "relu2":
        y = jnp.square(jnp.maximum(y, 0.0))
    if has_res:
        y = y + res_ref[...]
    if has_fnorm:
        ms = jnp.mean(y * y, axis=-1, keepdims=True)
        y = y * lax.rsqrt(ms + NORM_EPS) * fg_ref[...]
    o_ref[...] = y.astype(o_ref.dtype)


def fused_matmul(x, w, *, gain=None, res=None, fgain=None, act=None, out_dtype=F32, tm, tn, name):
    m, k = x.shape
    n = w.shape[1]
    assert m % tm == 0 and n % tn == 0 and w.shape[0] == k
    assert fgain is None or tn == n
    in_specs = [pl.BlockSpec((tm, k), lambda i, j: (i, 0))]
    args = [x]
    if gain is not None:
        in_specs.append(pl.BlockSpec((1, k), lambda i, j: (0, 0)))
        args.append(gain.reshape(1, k).astype(F32))
    in_specs.append(pl.BlockSpec((k, tn), lambda i, j: (0, j)))
    args.append(w)
    if res is not None:
        in_specs.append(pl.BlockSpec((tm, tn), lambda i, j: (i, j)))
        args.append(res)
    if fgain is not None:
        in_specs.append(pl.BlockSpec((1, n), lambda i, j: (0, 0)))
        args.append(fgain.reshape(1, n).astype(F32))
    scratch = [pltpu.VMEM((tm, k), BF16)] if gain is not None else []
    kern = functools.partial(_mm_kernel, has_norm=gain is not None, act=act, has_res=res is not None,
                             has_fnorm=fgain is not None)
    return pl.pallas_call(
        kern,
        grid=(m // tm, n // tn),
        in_specs=in_specs,
        out_specs=pl.BlockSpec((tm, tn), lambda i, j: (i, j)),
        out_shape=jax.ShapeDtypeStruct((m, n), out_dtype),
        scratch_shapes=scratch,
        compiler_params=_cparams(("parallel", "arbitrary")),
        name=name,
    )(*args)


def _row_tile(m, pref):
    for t in (pref, 1024, 512, 256, 128, 64, 32, 16, 8):
        if t <= pref and m % t == 0:
            return t
    raise ValueError(m)


def mlp_block(h, norm_g, w_up, w_down, final_g, name):
    tm = _row_tile(h.shape[0], 1024)
    a = fused_matmul(h, w_up, gain=norm_g, act="relu2", out_dtype=BF16, tm=tm, tn=1024, name=name + "_up")
    if final_g is None:
        return fused_matmul(a, w_down, res=h, tm=tm, tn=512, name=name + "_down")
    tm = _row_tile(h.shape[0], 512)
    return fused_matmul(a, w_down, res=h, fgain=final_g, tm=tm, tn=D_MODEL, name=name + "_down")


def _ssd_kernel(z_ref, x_ref, bc_ref, dt_ref, cw_ref, cb_ref, dtb_ref, alog_ref, dsk_ref, gn_ref,
                c0_ref, s0_ref, tri_ref, ex_ref,
                y_ref, sout_ref,
                ext_ref, st_ref, xs_ref, b_ref, c_ref, *, t_valid, n_chunks):
    c = pl.program_id(1)
    L = SSD_CHUNK

    @pl.when(c == 0)
    def _():
        ext_ref[0:8, :] = c0_ref[0]
        st_ref[...] = s0_ref[0].T

    @pl.when(c > 0)
    def _():
        ext_ref[0:8, :] = ext_ref[L:L + 8, :]

    ext_ref[8:L + 8, 0:D_INNER] = x_ref[...]
    ext_ref[8:L + 8, D_INNER:CONV_DIM] = bc_ref[...]

    cblk = 512
    for i in range(CONV_DIM // cblk):
        sl = slice(i * cblk, (i + 1) * cblk)
        acc = cb_ref[:, sl]
        for k in range(CONV_WIDTH):
            acc = acc + cw_ref[k:k + 1, sl] * ext_ref[5 + k:5 + k + L, sl]
        v = _silu(acc)
        if i < 4:
            xs_ref[:, sl] = v
        elif i < 6:
            b_ref[:, (i - 4) * cblk:(i - 3) * cblk] = v
        else:
            c_ref[:, (i - 6) * cblk:(i - 5) * cblk] = v

    dt_raw = dt_ref[...] + dtb_ref[...]
    dt = jnp.maximum(dt_raw, 0.0) + jnp.log1p(jnp.exp(-jnp.abs(dt_raw)))
    if t_valid % L:
        row = c * L + lax.broadcasted_iota(jnp.int32, (L, LANES), 0)
        dt = jnp.where(row < t_valid, dt, 0.0)
    da = dt * (-jnp.exp(alog_ref[...]))
    acs = _split_dot(da, tri_ref[...], left=True)
    acs_t = acs.T
    expd = _split_dot(jnp.concatenate([acs, dt], axis=0), ex_ref[...])
    a_exp = expd[0:L]
    dt_exp = expd[L:2 * L]

    li = lax.broadcasted_iota(jnp.int32, (L, L), 0)
    si = lax.broadcasted_iota(jnp.int32, (L, L), 1)
    causal = si <= li
    first_half = si < SSM_HEAD_DIM
    gw = D_INNER // SSM_GROUPS
    for g in range(SSM_GROUPS):
        gs = slice(g * gw, (g + 1) * gw)
        bg = b_ref[:, g * SSM_STATE:(g + 1) * SSM_STATE]
        cg = c_ref[:, g * SSM_STATE:(g + 1) * SSM_STATE].astype(BF16)
        cbm = _dot_nt(cg, bg.astype(BF16))
        ag = a_exp[:, gs]
        xg = xs_ref[:, gs]
        xdt = xg * dt_exp[:, gs]
        a_last = ag[L - 1:L, :]
        y_off = _dot(cg, st_ref[:, gs].astype(BF16)) * jnp.exp(ag)
        ys = []
        for p in range(2):
            h0 = 4 * g + 2 * p
            xp = xdt[:, p * LANES:(p + 1) * LANES].astype(BF16)
            outs = []
            for h in (h0, h0 + 1):
                seg = acs[:, h:h + 1] - acs_t[h:h + 1, :]
                lm = jnp.exp(jnp.where(causal, seg, NEG_INF))
                outs.append(_dot((cbm * lm).astype(BF16), xp))
            ys.append(jnp.where(first_half, outs[0], outs[1]))
        y = jnp.concatenate(ys, axis=1) + y_off
        xd = (xdt * jnp.exp(a_last - ag)).astype(BF16)
        st_ref[:, gs] = st_ref[:, gs] * jnp.exp(a_last) + _dot(bg.T.astype(BF16), xd)
        zg = z_ref[:, gs]
        yy = (y + dsk_ref[:, gs] * xg) * _silu(zg)
        ms = jnp.mean(yy * yy, axis=-1, keepdims=True)
        y_ref[:, gs] = (yy * lax.rsqrt(ms + NORM_EPS) * gn_ref[:, gs]).astype(y_ref.dtype)

    @pl.when(c == n_chunks - 1)
    def _():
        sout_ref[0] = st_ref[...].T


def ssd_mixer(zx, conv0, st0, prm, *, batch, t_pad, t_valid, name):
    n_chunks = t_pad // SSD_CHUNK
    L = SSD_CHUNK
    conv_w, conv_b, dt_bias, a_log, d_skip, gate_norm = prm
    lane_pad = lambda v: jnp.pad(v.astype(F32), (0, LANES - SSM_HEADS)).reshape(1, LANES)
    tri = jnp.asarray(np.tril(np.ones((L, L), np.float32)), BF16)
    ex_np = np.zeros((LANES, D_INNER), np.float32)
    ex_np[np.arange(D_INNER) // SSM_HEAD_DIM, np.arange(D_INNER)] = 1.0
    ex = jnp.asarray(ex_np, BF16)
    c0 = jnp.pad(conv0.astype(F32), ((0, 0), (8 - (CONV_WIDTH - 1), 0), (0, 0)))
    row = lambda b, c: (b * n_chunks + c)
    const = lambda shape: pl.BlockSpec(shape, lambda b, c: (0,) * len(shape))
    kern = functools.partial(_ssd_kernel, t_valid=t_valid, n_chunks=n_chunks)
    y, s_out = pl.pallas_call(
        kern,
        grid=(batch, n_chunks),
        in_specs=[
            pl.BlockSpec((L, D_INNER), lambda b, c: (row(b, c), 0)),
            pl.BlockSpec((L, D_INNER), lambda b, c: (row(b, c), 1)),
            pl.BlockSpec((L, D_INNER), lambda b, c: (row(b, c), 2)),
            pl.BlockSpec((L, LANES), lambda b, c: (row(b, c), (D_INNER + CONV_DIM) // LANES)),
            const((CONV_WIDTH, CONV_DIM)), const((1, CONV_DIM)), const((1, LANES)), const((1, LANES)),
            const((1, D_INNER)), const((1, D_INNER)),
            pl.BlockSpec((1, 8, CONV_DIM), lambda b, c: (b, 0, 0)),
            pl.BlockSpec((1, D_INNER, SSM_STATE), lambda b, c: (b, 0, 0)),
            const((L, L)), const((LANES, D_INNER)),
        ],
        out_specs=[
            pl.BlockSpec((L, D_INNER), lambda b, c: (row(b, c), 0)),
            pl.BlockSpec((1, D_INNER, SSM_STATE), lambda b, c: (b, 0, 0)),
        ],
        out_shape=[
            jax.ShapeDtypeStruct((batch * t_pad, D_INNER), BF16),
            jax.ShapeDtypeStruct((batch, D_INNER, SSM_STATE), F32),
        ],
        scratch_shapes=[
            pltpu.VMEM((L + 8, CONV_DIM), F32),
            pltpu.VMEM((SSM_STATE, D_INNER), F32),
            pltpu.VMEM((L, D_INNER), F32),
            pltpu.VMEM((L, SSM_GROUPS * SSM_STATE), F32),
            pltpu.VMEM((L, SSM_GROUPS * SSM_STATE), F32),
        ],
        compiler_params=_cparams(("parallel", "arbitrary")),
        name=name,
    )(zx, zx, zx, zx,
      conv_w.astype(F32).T, conv_b.astype(F32).reshape(1, CONV_DIM), lane_pad(dt_bias), lane_pad(a_log),
      jnp.repeat(d_skip.astype(F32), SSM_HEAD_DIM).reshape(1, D_INNER), gate_norm.astype(F32).reshape(1, D_INNER),
      c0, st0.astype(F32).reshape(batch, D_INNER, SSM_STATE), tri, ex)
    return y, s_out.reshape(batch, SSM_HEADS, SSM_HEAD_DIM, SSM_STATE)


CMP_PAGES = 16


def _cmp_u_kernel(pt_ref, *refs, feature_major):
    del pt_ref
    pages = refs[:CMP_PAGES]
    perm_ref, w_ref, uf_ref, us_ref, y_ref = refs[CMP_PAGES:]
    regroup = _dot_nt if feature_major else _dot
    for p, page in enumerate(pages):
        yp = regroup(perm_ref[...], page[...].astype(BF16))
        for j in range(CMP_STRIDE):
            y_ref[j, p * 8:(p + 1) * 8, :] = yp[j * 8:(j + 1) * 8, :]
    rows = y_ref.shape[1]
    low = lax.broadcasted_iota(jnp.int32, (rows, LANES), 1) < HEAD_DIM
    accs = [None] * (2 * KV_HEADS)
    for a in range(CMP_STRIDE // 4):
        for q in range(4):
            blocks = [y_ref[4 * a + jj, :, q * LANES:(q + 1) * LANES] for jj in range(4)]
            swapped = [pltpu.roll(x, HEAD_DIM, axis=1) for x in blocks]
            for gl in range(2):
                if gl == 0:
                    pairs = [jnp.where(low, blocks[0], swapped[1]), jnp.where(low, blocks[2], swapped[3])]
                else:
                    pairs = [jnp.where(low, swapped[0], blocks[1]), jnp.where(low, swapped[2], blocks[3])]
                part = _dot(jnp.concatenate(pairs, axis=1).astype(BF16), w_ref[q // 2, a])
                k = 2 * q + gl
                accs[k] = part if accs[k] is None else accs[k] + part
    for k in range(2 * KV_HEADS):
        uf_ref[0, k] = accs[k][:, 0:CMP_HIDDEN]
        us_ref[0, k] = accs[k][:, CMP_HIDDEN:2 * CMP_HIDDEN]


def cmp_project(src2d, page_ids, col_block, w1, name, feature_major=False):
    batch, n_pages = page_ids.shape
    assert n_pages % CMP_PAGES == 0
    n_sub = n_pages * (PAGE_SIZE // CMP_STRIDE)
    rows = CMP_PAGES * (PAGE_SIZE // CMP_STRIDE)
    w1r = w1.astype(F32).reshape(2, 2 * CMP_STRIDE, HEAD_DIM, CMP_HIDDEN)
    wj = jnp.concatenate([w1r[:, :CMP_STRIDE], w1r[:, CMP_STRIDE:]], axis=-1)
    wpad = wj.reshape(2, CMP_STRIDE // 4, 4 * HEAD_DIM, 2 * CMP_HIDDEN).astype(BF16)
    perm = np.zeros((PAGE_SIZE, PAGE_SIZE), np.float32)
    for j in range(CMP_STRIDE):
        for s in range(PAGE_SIZE // CMP_STRIDE):
            perm[j * 8 + s, CMP_STRIDE * s + j] = 1.0
    perm = jnp.asarray(perm, BF16)

    def page_spec(p):
        shape = (4 * LANES, PAGE_SIZE) if feature_major else (PAGE_SIZE, 4 * LANES)
        return pl.BlockSpec(shape, lambda b, c, pt: (pt[b, c * CMP_PAGES + p], col_block))

    u_shape = jax.ShapeDtypeStruct((batch, 2 * KV_HEADS, n_sub, CMP_HIDDEN), F32)
    u_spec = pl.BlockSpec((1, 2 * KV_HEADS, rows, CMP_HIDDEN), lambda b, c, pt: (b, 0, c, 0))
    return pl.pallas_call(
        functools.partial(_cmp_u_kernel, feature_major=feature_major),
        grid_spec=pltpu.PrefetchScalarGridSpec(
            num_scalar_prefetch=1,
            grid=(batch, n_pages // CMP_PAGES),
            in_specs=[page_spec(p) for p in range(CMP_PAGES)] + [
                pl.BlockSpec((PAGE_SIZE, PAGE_SIZE), lambda b, c, pt: (0, 0)),
                pl.BlockSpec((2, CMP_STRIDE // 4, 4 * HEAD_DIM, 2 * CMP_HIDDEN), lambda b, c, pt: (0, 0, 0, 0)),
            ],
            out_specs=[u_spec, u_spec],
            scratch_shapes=[pltpu.VMEM((CMP_STRIDE, rows, 4 * LANES), F32)],
        ),
        out_shape=[u_shape, u_shape],
        compiler_params=_cparams(("parallel", "arbitrary")),
        name=name,
    )(page_ids, *([src2d] * CMP_PAGES), perm, wpad)


def _cmp_out_kernel(uf_ref, us_ref, pos_ref, w1_ref, w2_ref, o_ref, sh_ref, *, n_sub):
    pos_term = _dot(pos_ref[0].astype(BF16), w1_ref[0])[0:1, :]
    rows = lax.broadcasted_iota(jnp.int32, (n_sub, HEAD_DIM), 0)
    for g in range(KV_HEADS):
        sh_ref[g, 0:n_sub, :] = us_ref[0, g]
        sh_ref[g, n_sub:n_sub + 8, :] = jnp.zeros((8, CMP_HIDDEN), F32)
        hid = jax.nn.gelu(uf_ref[0, g] + sh_ref[g, pl.ds(1, n_sub), :] + pos_term)
        out = _dot(hid.astype(BF16), w2_ref[0])
        o_ref[0, g] = jnp.where(rows < n_sub - 1, out, 0.0).astype(o_ref.dtype)


def cmp_finish(uf, us, pos_emb, w1, w2, name):
    batch, nkvg, n_sub, _ = uf.shape
    flat = 2 * CMP_STRIDE * HEAD_DIM
    pos = jnp.pad(pos_emb.astype(F32).reshape(2, 1, flat), ((0, 0), (0, 7), (0, 0)))
    assert nkvg == 2 * KV_HEADS
    u_spec = pl.BlockSpec((1, KV_HEADS, n_sub, CMP_HIDDEN), lambda b, kv: (b, kv, 0, 0))
    return pl.pallas_call(
        functools.partial(_cmp_out_kernel, n_sub=n_sub),
        grid=(batch, 2),
        in_specs=[u_spec, u_spec,
                  pl.BlockSpec((1, 8, flat), lambda b, kv: (kv, 0, 0)),
                  pl.BlockSpec((1, flat, CMP_HIDDEN), lambda b, kv: (kv, 0, 0)),
                  pl.BlockSpec((1, CMP_HIDDEN, HEAD_DIM), lambda b, kv: (kv, 0, 0))],
        out_specs=pl.BlockSpec((1, KV_HEADS, n_sub, HEAD_DIM), lambda b, kv: (b, kv, 0, 0)),
        out_shape=jax.ShapeDtypeStruct((batch, nkvg, n_sub, HEAD_DIM), BF16),
        scratch_shapes=[pltpu.VMEM((KV_HEADS, n_sub + 8, CMP_HIDDEN), F32)],
        compiler_params=_cparams(("parallel", "parallel")),
        name=name,
    )(uf, us, pos, w1.astype(BF16), w2.astype(BF16))


def _cmp_attn_kernel(q_ref, kc_ref, vc_ref, wsel_ref, o_ref, sel_ref, *rest, tq, ncp, nblk_pad, n_sel, q_base,
                     want_idx):
    qi = pl.program_id(2)
    use_ref = None if want_idx else rest[0]
    tpos = q_base + qi * tq + lax.broadcasted_iota(jnp.int32, (tq, 1), 0)
    c_end = CMP_STRIDE * lax.broadcasted_iota(jnp.int32, (1, ncp), 1) + (2 * CMP_STRIDE - 1)
    ok = c_end <= tpos
    any_ok = (tpos >= 2 * CMP_STRIDE - 1).astype(F32)
    kc = kc_ref[0, 0]
    vw = jnp.concatenate([vc_ref[0, 0], wsel_ref[...]], axis=1)
    p_slc = None
    if tq < LANES:
        s3 = _dot_nt(q_ref[0].reshape(GQA * tq, LANES), kc).reshape(GQA, tq, ncp)
        s3 = jnp.where(ok[None], s3, NEG_INF)
        e3 = jnp.exp2(s3 - jnp.max(s3, axis=-1, keepdims=True))
        r3 = _dot(e3.reshape(GQA * tq, ncp).astype(BF16), vw).reshape(GQA, tq, LANES + nblk_pad)
        inv3 = any_ok[None] / r3[:, :, HEAD_DIM:HEAD_DIM + 1]
        o_ref[0] = r3[:, :, 0:HEAD_DIM] * inv3
        p_slc = jnp.sum(r3[:, :, LANES:] * inv3, axis=0)
    for h in range(GQA if tq >= LANES else 0):
        s = jnp.where(ok, _dot_nt(q_ref[0, h], kc), NEG_INF)
        e = jnp.exp2(s - jnp.max(s, axis=-1, keepdims=True))
        r = _dot(e.astype(BF16), vw)
        inv = any_ok / r[:, HEAD_DIM:HEAD_DIM + 1]
        o_ref[0, h] = r[:, 0:HEAD_DIM] * inv
        part = r[:, LANES:] * inv
        p_slc = part if p_slc is None else p_slc + part
    tq_l = max(tq, LANES)
    if tq < LANES:
        p_slc = jnp.concatenate([p_slc, jnp.zeros((tq_l - tq, nblk_pad), F32)], axis=0)
    sc = p_slc.T
    blk = lax.broadcasted_iota(jnp.int32, (nblk_pad, tq_l), 0)
    tp = q_base + qi * tq + lax.broadcasted_iota(jnp.int32, (nblk_pad, tq_l), 1)
    cur = jnp.right_shift(tp, SLC_SHIFT)
    forced = (blk == 0) | (blk == cur) | (blk == cur - 1)
    reach = blk * SLC_BLOCK <= tp
    sc = jnp.where(forced, 1e30, jnp.where(reach, sc, -1.0))
    sc = jnp.where(blk < n_sel, sc, -3.0)
    removed = -2.0
    idx_rows = []
    for _ in range(min(SLC_TOP_N, n_sel)):
        mx = jnp.max(sc, axis=0, keepdims=True)
        idx = jnp.min(jnp.where(sc == mx, blk, nblk_pad), axis=0, keepdims=True)
        sc = jnp.where(blk == idx, removed, sc)
        idx_rows.append(idx)
    if want_idx:
        sel_ref[0, 0] = jnp.concatenate(idx_rows, axis=0)
    else:
        chosen_t = (sc == removed).astype(F32).T[0:tq]
        sel_ref[0, 0] = ((chosen_t - 1.0) * 1e30).astype(sel_ref.dtype)
        used = jnp.max(chosen_t, axis=0, keepdims=True)
        use_ref[0, 0, 0] = jnp.broadcast_to(used, (8, nblk_pad))


LOG2E = 1.4426950408889634


def _query_feats(q_h, slopes):
    sl = slopes * LOG2E
    s1 = sl.astype(BF16)
    r1 = sl - s1.astype(F32)
    s2 = r1.astype(BF16)
    s3 = (r1 - s2.astype(F32)).astype(BF16)
    feats = jnp.stack([s1, s2, s3, s1, s2, s3, jnp.ones_like(s1)], axis=-1)
    return _with_feats((q_h * (LOG2E * HEAD_DIM ** -0.5)).astype(BF16), feats[None, :, None, :], LANES)


def _pos_feats(pos, invalid=None):
    lo = jnp.bitwise_and(pos, 255)
    hi = pos - lo
    inv = jnp.zeros_like(pos, F32) if invalid is None else jnp.where(invalid, NEG_INF, 0.0)
    return jnp.concatenate([jnp.stack([lo, lo, lo, hi, hi, hi], axis=-1).astype(BF16), inv.astype(BF16)[..., None]],
                           axis=-1)


def _with_feats(x, feats, width):
    lead = x.shape[:-1]
    f = jnp.broadcast_to(feats, lead + feats.shape[-1:]).astype(BF16)
    pad = jnp.zeros(lead + (width - x.shape[-1] - f.shape[-1],), BF16)
    return jnp.concatenate([x.astype(BF16), f, pad], axis=-1)


def _with_ones(v):
    return _with_feats(v, jnp.ones((1,), BF16), LANES)


def cmp_attention(qa, kvc, *, tq, n_sel, q_base, want_idx, name):
    b, _, t, _ = qa.shape
    ncp = kvc.shape[2]
    nblk_pad = -(-n_sel // LANES) * LANES
    c_end = CMP_STRIDE * jnp.arange(ncp, dtype=jnp.int32) + (2 * CMP_STRIDE - 1)
    kca = _with_feats(kvc[:, :KV_HEADS], _pos_feats(c_end), LANES)
    vca = _with_ones(kvc[:, KV_HEADS:])
    c = np.arange(ncp)[:, None]
    j = np.arange(nblk_pad)[None, :]
    wsel = ((c // 4 == j).astype(np.float32) + ((c + 1) // 4 == j).astype(np.float32)) * (c < ncp - 1)
    tq_l = max(tq, LANES)
    if want_idx:
        sel_shape = jax.ShapeDtypeStruct((b, KV_HEADS, SLC_TOP_N, (t // tq) * tq_l), jnp.int32)
        sel_spec = pl.BlockSpec((1, 1, SLC_TOP_N, tq_l), lambda bi, g, qi: (bi, g, 0, qi))
    else:
        sel_shape = jax.ShapeDtypeStruct((b, KV_HEADS, t, nblk_pad), BF16)
        sel_spec = pl.BlockSpec((1, 1, tq, nblk_pad), lambda bi, g, qi: (bi, g, qi, 0))
    kern = functools.partial(_cmp_attn_kernel, tq=tq, ncp=ncp, nblk_pad=nblk_pad, n_sel=n_sel, q_base=q_base,
                             want_idx=want_idx)
    out_specs = [pl.BlockSpec((1, GQA, tq, HEAD_DIM), lambda bi, g, qi: (bi, g, qi, 0)), sel_spec]
    out_shape = [jax.ShapeDtypeStruct((b, ATT_HEADS, t, HEAD_DIM), F32), sel_shape]
    if not want_idx:
        out_specs.append(pl.BlockSpec((1, 1, 1, 8, nblk_pad), lambda bi, g, qi: (bi, g, qi, 0, 0)))
        out_shape.append(jax.ShapeDtypeStruct((b, KV_HEADS, t // tq, 8, nblk_pad), F32))
    return pl.pallas_call(
        kern,
        grid=(b, KV_HEADS, t // tq),
        in_specs=[
            pl.BlockSpec((1, GQA, tq, LANES), lambda bi, g, qi: (bi, g, qi, 0)),
            pl.BlockSpec((1, 1, ncp, LANES), lambda bi, g, qi: (bi, g, 0, 0)),
            pl.BlockSpec((1, 1, ncp, LANES), lambda bi, g, qi: (bi, g, 0, 0)),
            pl.BlockSpec((ncp, nblk_pad), lambda bi, g, qi: (0, 0)),
        ],
        out_specs=out_specs,
        out_shape=out_shape,
        compiler_params=_cparams(("parallel", "parallel", "parallel")),
        name=name,
    )(qa, kca, vca, jnp.asarray(wsel, BF16))


SEL_FEAT = 256


def _sel_kernel(use_ref, qa_ref, bias_ref, ka_ref, v_ref, o_ref, m_ref, acc_ref, *, tq, tk):
    bi = pl.program_id(0)
    g = pl.program_id(1)
    qi = pl.program_id(2)
    n_qt = pl.num_programs(2)
    per = tq // tk
    n_kc = n_qt * per
    m_ref[...] = jnp.full(m_ref.shape, NEG_INF, F32)
    acc_ref[...] = jnp.zeros(acc_ref.shape, F32)
    rows = lax.broadcasted_iota(jnp.int32, (tq, tk), 0)
    cols = lax.broadcasted_iota(jnp.int32, (tq, tk), 1)

    def chunk(j, diag_offset):
        k0 = pl.multiple_of(j * tk, tk)
        ka = ka_ref[0, 0, pl.ds(k0, tk), :]
        va = v_ref[0, 0, pl.ds(k0, tk), :]
        bias = bias_ref[0, 0]
        for h in range(GQA):
            qf = jnp.concatenate([qa_ref[0, h], bias], axis=1)
            s = _dot_nt(qf, ka)
            if diag_offset is not None:
                s = jnp.where(cols + diag_offset <= rows, s, NEG_INF)
            m_prev = m_ref[h]
            m_new = jnp.maximum(m_prev, jnp.max(s, axis=-1, keepdims=True))
            p = jnp.exp2(s - jnp.concatenate([m_new] * (tk // LANES), axis=1))
            acc_ref[h] = jnp.exp2(m_prev - m_new) * acc_ref[h] + _dot(p.astype(BF16), va)
            m_ref[h] = m_new

    def body(j, carry):
        @pl.when(use_ref[((bi * KV_HEADS + g) * n_qt + qi) * n_kc + j] > 0)
        def _():
            chunk(j, None)
        return carry

    lax.fori_loop(0, qi * per, body, 0)
    for c in range(per):
        chunk(qi * per + c, c * tk)
    for h in range(GQA):
        acc = acc_ref[h]
        o_ref[0, h] = acc[:, 0:HEAD_DIM] / acc[:, HEAD_DIM:HEAD_DIM + 1]


def sel_attention_dense(chunk_used, qa, bias, ka, va, *, tq, tk, name):
    b, _, t, _ = qa.shape
    return pl.pallas_call(
        functools.partial(_sel_kernel, tq=tq, tk=tk),
        grid_spec=pltpu.PrefetchScalarGridSpec(
            num_scalar_prefetch=1,
            grid=(b, KV_HEADS, t // tq),
            in_specs=[
                pl.BlockSpec((1, GQA, tq, LANES), lambda bi, g, qi, *_: (bi, g, qi, 0)),
                pl.BlockSpec((1, 1, tq, LANES), lambda bi, g, qi, *_: (bi, g, qi, 0)),
                pl.BlockSpec((1, 1, t, SEL_FEAT), lambda bi, g, qi, *_: (bi, g, 0, 0)),
                pl.BlockSpec((1, 1, t, LANES), lambda bi, g, qi, *_: (bi, g, 0, 0)),
            ],
            out_specs=pl.BlockSpec((1, GQA, tq, HEAD_DIM), lambda bi, g, qi, *_: (bi, g, qi, 0)),
            scratch_shapes=[pltpu.VMEM((GQA, tq, LANES), F32), pltpu.VMEM((GQA, tq, LANES), F32)],
        ),
        out_shape=jax.ShapeDtypeStruct((b, ATT_HEADS, t, HEAD_DIM), F32),
        compiler_params=_cparams(("parallel", "parallel", "arbitrary")),
        name=name,
    )(chunk_used.reshape(-1), qa, bias, ka, va)


def _sel_gather_kernel(idx_ref, pt_ref, q_ref, slope_ref, cache_ref, new_ref, o_ref, kbuf, vbuf, sem, *,
                       n_tok, tq, past_len, n_steps):
    b = pl.program_id(0)
    g = pl.program_id(1)
    step = b * KV_HEADS + g
    slot = step % 2
    new_blk = past_len // SLC_BLOCK
    halves = PAGE_SIZE // SLC_BLOCK

    def block_copy(bb, gg, sl, t, n, kv, blk, newest):
        buf = vbuf if kv else kbuf
        dst = buf.at[sl, t, :, pl.ds(n * PAGE_SIZE, PAGE_SIZE)]
        if newest:
            src = new_ref.at[bb, kv, gg]
        else:
            page = pt_ref[bb * (past_len // PAGE_SIZE) + jnp.minimum(blk, new_blk - 1) // halves]
            src = cache_ref.at[page, kv, gg]
        return pltpu.make_async_copy(src, dst, sem.at[sl])

    def for_each_block(bb, gg, fn):
        for t in range(n_tok):
            for n in range(SLC_TOP_N):
                blk = idx_ref[((bb * KV_HEADS + gg) * SLC_TOP_N + n) * n_tok + t]
                for kv in range(2):
                    fn(t, n, kv, blk)

    def start(bb, gg, sl):
        def one(t, n, kv, blk):
            @pl.when(blk >= new_blk)
            def _():
                block_copy(bb, gg, sl, t, n, kv, blk, True).start()

            @pl.when(blk < new_blk)
            def _():
                block_copy(bb, gg, sl, t, n, kv, blk, False).start()
        for_each_block(bb, gg, one)

    @pl.when(step == 0)
    def _():
        start(b, g, slot)

    @pl.when(step + 1 < n_steps)
    def _():
        nxt = step + 1
        start(nxt // KV_HEADS, nxt % KV_HEADS, 1 - slot)

    for_each_block(b, g, lambda t, n, kv, blk: block_copy(b, g, slot, t, n, kv, blk, False).wait())

    r = GQA * tq
    q = (q_ref[0].reshape(r, HEAD_DIM) * (HEAD_DIM ** -0.5)).astype(BF16)
    nk = SLC_TOP_N * PAGE_SIZE
    lane = lax.broadcasted_iota(jnp.int32, (1, nk), 1)
    trow = lax.broadcasted_iota(jnp.int32, (tq, 1), 0)
    tpos = past_len + trow
    acc = jnp.zeros((GQA, tq, HEAD_DIM), F32)
    in_page = jnp.bitwise_and(lane, PAGE_SIZE - 1)
    half_of_lane = jnp.right_shift(in_page, SLC_SHIFT)
    slot_of_lane = jnp.right_shift(lane, SLC_SHIFT + 1)
    for t in range(n_tok):
        kpos = in_page
        chosen_half = jnp.zeros((1, nk), jnp.int32)
        for n in range(SLC_TOP_N):
            blk = idx_ref[((b * KV_HEADS + g) * SLC_TOP_N + n) * n_tok + t]
            here = slot_of_lane == n
            kpos = jnp.where(here, kpos + (blk // halves) * PAGE_SIZE, kpos)
            chosen_half = jnp.where(here, blk % halves, chosen_half)
        d = (tpos - kpos).astype(F32)
        ok = (d >= 0.0) & (half_of_lane == chosen_half)
        s = _dot(q, kbuf[slot, t].astype(BF16))
        s3 = s.reshape(GQA, tq, nk) - slope_ref[0] * d[None]
        s3 = jnp.where(ok[None], s3, NEG_INF)
        m = jnp.max(s3, axis=-1, keepdims=True)
        e = jnp.exp(s3 - m)
        p = e / jnp.sum(e, axis=-1, keepdims=True)
        o = _dot_nt(p.reshape(r, nk).astype(BF16), vbuf[slot, t].astype(BF16)).reshape(GQA, tq, HEAD_DIM)
        acc = jnp.where((trow == t)[None], o, acc)
    o_ref[0] = acc


def sel_attention_gather(q_h, idx, page_table, cache, new_kv, slopes, *, n_tok, past_len, name):
    b, _, tq, _ = q_h.shape
    n_steps = b * KV_HEADS
    nk = SLC_TOP_N * PAGE_SIZE
    kern = functools.partial(_sel_gather_kernel, n_tok=n_tok, tq=tq, past_len=past_len, n_steps=n_steps)
    return pl.pallas_call(
        kern,
        grid_spec=pltpu.PrefetchScalarGridSpec(
            num_scalar_prefetch=2,
            grid=(b, KV_HEADS),
            in_specs=[
                pl.BlockSpec((1, GQA, tq, HEAD_DIM), lambda bi, g, *_: (bi, g, 0, 0)),
                pl.BlockSpec((1, GQA, 1, 1), lambda bi, g, *_: (g, 0, 0, 0)),
                pl.BlockSpec(memory_space=pl.ANY),
                pl.BlockSpec(memory_space=pl.ANY),
            ],
            out_specs=pl.BlockSpec((1, GQA, tq, HEAD_DIM), lambda bi, g, *_: (bi, g, 0, 0)),
            scratch_shapes=[pltpu.VMEM((2, n_tok, HEAD_DIM, nk), F32), pltpu.VMEM((2, n_tok, HEAD_DIM, nk), F32),
                            pltpu.SemaphoreType.DMA((2,))],
        ),
        out_shape=jax.ShapeDtypeStruct((b, ATT_HEADS, tq, HEAD_DIM), F32),
        compiler_params=_cparams(("arbitrary", "arbitrary")),
        name=name,
    )(idx.reshape(-1), page_table.reshape(-1), q_h, slopes.reshape(KV_HEADS, GQA, 1, 1), cache, new_kv)


def _win_kernel(q_ref, k_ref, v_ref, oc_ref, os_ref, gl_ref, o_ref, *, tq, nk):
    qi = pl.program_id(2)
    r0 = pl.multiple_of(qi * tq, tq)
    kw = k_ref[0, 0, pl.ds(r0, nk), :]
    vw = v_ref[0, 0, pl.ds(r0, nk), :]
    blockwise = tq % LANES == 0 and WINDOW % tq == 0
    if blockwise:
        rows = lax.broadcasted_iota(jnp.int32, (tq, tq), 0)
        cols = lax.broadcasted_iota(jnp.int32, (tq, tq), 1)
        old_ok = cols > rows
        new_ok = cols <= rows
    else:
        di = (WINDOW + lax.broadcasted_iota(jnp.int32, (tq, 1), 0)) - lax.broadcasted_iota(jnp.int32, (1, nk), 1)
        ok = (di >= 0) & (di < WINDOW)
    if tq < LANES:
        s3 = _dot_nt(q_ref[0].reshape(GQA * tq, LANES), kw).reshape(GQA, tq, nk)
        s3 = jnp.where(ok[None], s3, NEG_INF)
        e3 = jnp.exp2(s3 - jnp.max(s3, axis=-1, keepdims=True))
        r3 = _dot(e3.reshape(GQA * tq, nk).astype(BF16), vw).reshape(GQA, tq, LANES)
        ow3 = r3[:, :, 0:HEAD_DIM] / r3[:, :, HEAD_DIM:HEAD_DIM + 1]
        gates3 = 1.0 / (1.0 + jnp.exp(-gl_ref[0]))
        o_ref[0] = gates3[:, :, 0:1] * oc_ref[0] + gates3[:, :, 1:2] * os_ref[0] + gates3[:, :, 2:3] * ow3
    for h in range(GQA if tq >= LANES else 0):
        s = _dot_nt(q_ref[0, h], kw)
        if blockwise:
            s = jnp.concatenate([jnp.where(old_ok, s[:, 0:tq], NEG_INF), s[:, tq:nk - tq],
                                 jnp.where(new_ok, s[:, nk - tq:nk], NEG_INF)], axis=1)
        else:
            s = jnp.where(ok, s, NEG_INF)
        e = jnp.exp2(s - jnp.max(s, axis=-1, keepdims=True))
        r = _dot(e.astype(BF16), vw)
        ow = r[:, 0:HEAD_DIM] / r[:, HEAD_DIM:HEAD_DIM + 1]
        gates = 1.0 / (1.0 + jnp.exp(-gl_ref[0, h]))
        o_ref[0, h] = gates[:, 0:1] * oc_ref[0, h] + gates[:, 1:2] * os_ref[0, h] + gates[:, 2:3] * ow


def win_attention_merge(qa, kw, vw, o_c, o_s, gate_logits, *, tq, k_valid_from, name):
    b, _, t, _ = qa.shape
    nk = tq + WINDOW
    lk = kw.shape[2]
    assert lk >= t + WINDOW
    krow = jnp.arange(lk, dtype=jnp.int32)
    kwa = _with_feats(kw, _pos_feats(krow, invalid=krow < k_valid_from), LANES)
    vwa = _with_ones(vw)
    head_spec = lambda w: pl.BlockSpec((1, GQA, tq, w), lambda bi, g, qi: (bi, g, qi, 0))
    kv_spec = pl.BlockSpec((1, 1, lk, LANES), lambda bi, g, qi: (bi, g, 0, 0))
    return pl.pallas_call(
        functools.partial(_win_kernel, tq=tq, nk=nk),
        grid=(b, KV_HEADS, t // tq),
        in_specs=[head_spec(LANES), kv_spec, kv_spec,
                  head_spec(HEAD_DIM), head_spec(HEAD_DIM), head_spec(N_BRANCH)],
        out_specs=head_spec(HEAD_DIM),
        out_shape=jax.ShapeDtypeStruct((b, ATT_HEADS, t, HEAD_DIM), F32),
        compiler_params=_cparams(("parallel", "parallel", "parallel")),
        name=name,
    )(qa, kwa, vwa, o_c, o_s, gate_logits)


def _heads(x2d, b, t, n_heads):
    return x2d.reshape(b, t, n_heads, HEAD_DIM).transpose(0, 2, 1, 3)


def _kv_stack(qkv, b, t, col):
    return qkv[:, col:col + 2 * KV_DIM].reshape(b, t, 2, KV_HEADS, HEAD_DIM)


def alibi_slopes():
    return 2.0 ** (-8.0 * jnp.arange(1, ATT_HEADS + 1, dtype=jnp.float32) / ATT_HEADS)


def nsa_prompt_layer(h2d, b, t, norm_g, w_in, cmp_pos, cmp_w1, cmp_w2, w_out, slopes):
    tm = _row_tile(h2d.shape[0], 1024)
    qkv = fused_matmul(h2d, w_in, gain=norm_g, tm=tm, tn=ATT_IN_PAD // 3, name="att_in_p")
    q_h = _heads(qkv[:, :D_MODEL], b, t, ATT_HEADS)
    gl_h = qkv[:, D_MODEL + 6 * KV_DIM:D_MODEL + 6 * KV_DIM + ATT_HEADS * N_BRANCH]
    gl_h = gl_h.reshape(b, t, ATT_HEADS, N_BRANCH).transpose(0, 2, 1, 3)

    n_pages = t // PAGE_SIZE
    page_ids = (jnp.arange(b, dtype=jnp.int32)[:, None] * n_pages + jnp.arange(n_pages, dtype=jnp.int32)[None, :])
    uf, us = cmp_project(qkv, page_ids, D_MODEL // (2 * KV_DIM), cmp_w1, "cmp_u_p")
    kvc = cmp_finish(uf, us, cmp_pos, cmp_w1, cmp_w2, "cmp_out_p")

    n_sel = t // SLC_BLOCK
    tq = 256
    qa = _query_feats(q_h, slopes)
    o_c, sel_bias, blk_used = cmp_attention(qa, kvc, tq=tq, n_sel=n_sel, q_base=0, want_idx=False, name="cmp_attn_p")
    tq_sel, tk_sel = min(512, t), min(512, t)
    used = blk_used[:, :, :, 0, :n_sel].reshape(b, KV_HEADS, t // tq_sel, tq_sel // tq, t // tk_sel, tk_sel // SLC_BLOCK)
    chunk_used = (used.max(axis=(3, 5)) > 0.0).astype(jnp.int32)

    nb = sel_bias.shape[-1]
    assert LANES + nb == SEL_FEAT
    ks = _heads(qkv[:, D_MODEL + 2 * KV_DIM:D_MODEL + 3 * KV_DIM], b, t, KV_HEADS).astype(BF16)
    vs = _heads(qkv[:, D_MODEL + 3 * KV_DIM:D_MODEL + 4 * KV_DIM], b, t, KV_HEADS).astype(BF16)
    kpos = jnp.arange(t, dtype=jnp.int32)
    onehot = jax.nn.one_hot(kpos // SLC_BLOCK, nb, dtype=BF16)
    ka = jnp.concatenate([_with_feats(ks, _pos_feats(kpos), LANES),
                          jnp.broadcast_to(onehot, (b, KV_HEADS) + onehot.shape)], axis=-1)
    o_s = sel_attention_dense(chunk_used, qa, sel_bias, ka, _with_ones(vs), tq=tq_sel, tk=tk_sel, name="sel_attn_p")

    front = ((0, 0), (0, 0), (WINDOW, 0), (0, 0))
    kw = jnp.pad(_heads(qkv[:, D_MODEL + 4 * KV_DIM:D_MODEL + 5 * KV_DIM], b, t, KV_HEADS).astype(BF16), front)
    vw = jnp.pad(_heads(qkv[:, D_MODEL + 5 * KV_DIM:D_MODEL + 6 * KV_DIM], b, t, KV_HEADS).astype(BF16), front)
    o = win_attention_merge(qa, kw, vw, o_c, o_s, gl_h, tq=tq, k_valid_from=WINDOW, name="win_attn_p")
    o2d = o.transpose(0, 2, 1, 3).reshape(b * t, D_MODEL)
    h_new = fused_matmul(o2d, w_out, res=h2d, tm=tm, tn=512, name="att_out_p")
    win_rows = min(WINDOW, t)
    return (h_new, _kv_stack(qkv, b, t, D_MODEL), _kv_stack(qkv, b, t, D_MODEL + 2 * KV_DIM),
            _kv_stack(qkv, b, t, D_MODEL + 4 * KV_DIM)[:, t - win_rows:])


def nsa_sample_layer(h2d, b, t, cache_cmp, cache_slc, win_buf, page_table, norm_g, w_in, cmp_pos, cmp_w1, cmp_w2,
                     w_out, slopes):
    n_pages = page_table.shape[1]
    past_len = n_pages * PAGE_SIZE
    tq = 16
    assert t <= tq and t <= 2 * CMP_STRIDE - 1
    qkv = fused_matmul(h2d, w_in, gain=norm_g, tm=h2d.shape[0], tn=ATT_IN_PAD // 3, name="att_in_s")
    tpad = ((0, 0), (0, 0), (0, tq - t), (0, 0))
    q_h = jnp.pad(_heads(qkv[:, :D_MODEL], b, t, ATT_HEADS), tpad)
    gl_h = qkv[:, D_MODEL + 6 * KV_DIM:D_MODEL + 6 * KV_DIM + ATT_HEADS * N_BRANCH]
    gl_h = jnp.pad(gl_h.reshape(b, t, ATT_HEADS, N_BRANCH).transpose(0, 2, 1, 3), tpad)
    new_cmp = _kv_stack(qkv, b, t, D_MODEL)
    new_slc = _kv_stack(qkv, b, t, D_MODEL + 2 * KV_DIM)
    new_win = _kv_stack(qkv, b, t, D_MODEL + 4 * KV_DIM)

    cmp_fm = cache_cmp.transpose(0, 2, 3, 4, 1).reshape(-1, PAGE_SIZE)
    slc_fm = cache_slc.transpose(0, 2, 3, 4, 1)
    uf, us = cmp_project(cmp_fm, page_table, 0, cmp_w1, "cmp_u_s", feature_major=True)
    kvc = cmp_finish(uf, us, cmp_pos, cmp_w1, cmp_w2, "cmp_out_s")
    l_all = past_len + t
    n_sel = -(-l_all // SLC_BLOCK)
    qa = _query_feats(q_h, slopes)
    o_c, idx = cmp_attention(qa, kvc, tq=tq, n_sel=n_sel, q_base=past_len, want_idx=True, name="cmp_attn_s")

    new_page = jnp.pad(new_slc, ((0, 0), (0, PAGE_SIZE - t), (0, 0), (0, 0), (0, 0)))
    o_s = sel_attention_gather(q_h, idx[..., :t], page_table, slc_fm, new_page.transpose(0, 2, 3, 4, 1), slopes,
                               n_tok=t, past_len=past_len, name="sel_attn_s")

    w_rows = win_buf.shape[1]
    win_all = jnp.concatenate([win_buf.astype(F32), new_win], axis=1)
    lk = tq + WINDOW
    shift = WINDOW - w_rows
    kvw = jnp.pad(win_all, ((0, 0), (shift, lk - shift - w_rows - t), (0, 0), (0, 0), (0, 0))).astype(BF16)
    kw = kvw[:, :, 0].transpose(0, 2, 1, 3)
    vw = kvw[:, :, 1].transpose(0, 2, 1, 3)
    o = win_attention_merge(qa, kw, vw, o_c, o_s, gl_h, tq=tq, k_valid_from=shift, name="win_attn_s")
    o2d = o[:, :, :t].transpose(0, 2, 1, 3).reshape(b * t, D_MODEL)
    h_new = fused_matmul(o2d, w_out, res=h2d, tm=h2d.shape[0], tn=512, name="att_out_s")
    return h_new, new_cmp, new_slc, win_all[:, -w_rows:]


def ssm_layer(h2d, b, t, conv_state, ssm_state, norm_g, w_in, conv_prm, w_out, tag):
    t_pad = -(-t // SSD_CHUNK) * SSD_CHUNK
    tm = _row_tile(h2d.shape[0], 1024)
    zx = fused_matmul(h2d, w_in, gain=norm_g, tm=tm, tn=1280, name="ssm_in_" + tag)
    keep = CONV_WIDTH - 1
    tail = zx.reshape(b, t, SSM_IN_PAD)[:, max(t - keep, 0):, D_INNER:D_INNER + CONV_DIM]
    conv_out = tail if t >= keep else jnp.concatenate([conv_state.astype(F32), tail], axis=1)[:, -keep:]
    if t_pad != t:
        zx = jnp.pad(zx.reshape(b, t, SSM_IN_PAD), ((0, 0), (0, t_pad - t), (0, 0))).reshape(b * t_pad, SSM_IN_PAD)
    y, s_new = ssd_mixer(zx, conv_state, ssm_state, conv_prm, batch=b, t_pad=t_pad, t_valid=t, name="ssd_" + tag)
    if t_pad != t:
        y = y.reshape(b, t_pad, D_INNER)[:, :t].reshape(b * t, D_INNER)
    h_new = fused_matmul(y, w_out, res=h2d, tm=tm, tn=512, name="ssm_out_" + tag)
    return h_new, conv_out, s_new


def kernel(x_prompt, x_sample, state_ssm, state_conv, cache_cmp_kv, cache_slc_kv, state_win_kv, page_table, ssm_norm, ssm_w_in, ssm_conv_w, ssm_conv_b, ssm_dt_bias, ssm_a_log, ssm_d, ssm_gate_norm, ssm_w_out, att_norm, att_w_in, att_cmp_pos, att_cmp_w1, att_cmp_w2, att_w_out, mlp_norm, mlp_w_up, mlp_w_down, final_norm):
    bp, tp, _ = x_prompt.shape
    bs, ts, _ = x_sample.shape
    depth = mlp_norm.shape[0]
    slopes = alibi_slopes()
    hp = x_prompt.reshape(bp * tp, D_MODEL)
    hs = x_sample.reshape(bs * ts, D_MODEL)
    ssm_p, conv_p, ssm_s, conv_s = [], [], [], []
    cmp_p, slc_p, win_p, cmp_s, slc_s, win_s = [], [], [], [], [], []
    for i in range(depth):
        j = i // 2
        if i % 2 == 0:
            w_in = jnp.pad(ssm_w_in[j], ((0, 0), (0, SSM_IN_PAD - ssm_w_in.shape[-1]))).astype(BF16)
            w_out = ssm_w_out[j].astype(BF16)
            prm = (ssm_conv_w[j], ssm_conv_b[j], ssm_dt_bias[j], ssm_a_log[j], ssm_d[j], ssm_gate_norm[j])
            zc = jnp.zeros((bp, CONV_WIDTH - 1, CONV_DIM), F32)
            zs = jnp.zeros((bp, SSM_HEADS, SSM_HEAD_DIM, SSM_STATE), F32)
            hp, cp, sp = ssm_layer(hp, bp, tp, zc, zs, ssm_norm[j], w_in, prm, w_out, "p")
            hs, cs, ss = ssm_layer(hs, bs, ts, state_conv[j], state_ssm[j], ssm_norm[j], w_in, prm, w_out, "s")
            conv_p.append(cp); ssm_p.append(sp); conv_s.append(cs); ssm_s.append(ss)
        else:
            w_in = jnp.pad(att_w_in[j], ((0, 0), (0, ATT_IN_PAD - att_w_in.shape[-1]))).astype(BF16)
            w_out = att_w_out[j].astype(BF16)
            prm = (att_norm[j], w_in, att_cmp_pos[j], att_cmp_w1[j], att_cmp_w2[j], w_out, slopes)
            hp, kp, lp, wp = nsa_prompt_layer(hp, bp, tp, *prm)
            hs, kq, lq, wq = nsa_sample_layer(hs, bs, ts, cache_cmp_kv[j], cache_slc_kv[j], state_win_kv[j],
                                              page_table, *prm)
            cmp_p.append(kp); slc_p.append(lp); win_p.append(wp)
            cmp_s.append(kq); slc_s.append(lq); win_s.append(wq)
        fin = final_norm if i == depth - 1 else None
        w_up = mlp_w_up[i].astype(BF16)
        w_down = mlp_w_down[i].astype(BF16)
        hp = mlp_block(hp, mlp_norm[i], w_up, w_down, fin, "mlp%d_p" % i)
        hs = mlp_block(hs, mlp_norm[i], w_up, w_down, fin, "mlp%d_s" % i)
    return (hp.reshape(bp, tp, D_MODEL), hs.reshape(bs, ts, D_MODEL),
            jnp.stack(ssm_p), jnp.stack(conv_p), jnp.stack(cmp_p), jnp.stack(slc_p), jnp.stack(win_p),
            jnp.stack(ssm_s), jnp.stack(conv_s), jnp.stack(cmp_s), jnp.stack(slc_s), jnp.stack(win_s))
```

```python
import functools

import numpy as np
import jax
import jax.numpy as jnp
from jax import lax
from jax.experimental import pallas as pl
from jax.experimental.pallas import tpu as pltpu

F32 = jnp.float32
BF16 = jnp.bfloat16

D_MODEL = 1024
D_INNER = 2048
SSM_HEADS = 32
SSM_HEAD_DIM = 64
SSM_GROUPS = 8
SSM_STATE = 128
CONV_WIDTH = 4
CONV_DIM = 4096
SSM_IN_PAD = 6400
SSD_CHUNK = 128
ATT_HEADS = 16
HEAD_DIM = 64
KV_HEADS = 4
GQA = ATT_HEADS // KV_HEADS
KV_DIM = KV_HEADS * HEAD_DIM
CMP_STRIDE = 16
CMP_HIDDEN = 128
SLC_BLOCK = 64
SLC_SHIFT = 6
SLC_TOP_N = 16
WINDOW = 512
N_BRANCH = 3
ATT_IN_PAD = 2688
PAGE_SIZE = 128
NORM_EPS = 1e-5
NEG_INF = -1e30
LANES = 128
VMEM_LIMIT = 56 * 1024 * 1024


def _cparams(sem):
    return pltpu.CompilerParams(dimension_semantics=sem, vmem_limit_bytes=VMEM_LIMIT)


def _dot(a, b):
    return jnp.dot(a, b, preferred_element_type=F32)


def _dot_nt(a, b):
    return lax.dot_general(a, b, (((1,), (1,)), ((), ())), preferred_element_type=F32)


def _split_dot(a, onehot, left=False, terms=3):
    acc = None
    rem = a
    for _ in range(terms):
        hi = rem.astype(BF16)
        part = _dot(onehot, hi) if left else _dot(hi, onehot)
        acc = part if acc is None else acc + part
        rem = rem - hi.astype(F32)
    return acc


def _silu(x):
    return x * (0.5 * jnp.tanh(0.5 * x) + 0.5)


def _mm_kernel(*refs, has_norm, act, has_res, has_fnorm):
    it = iter(refs)
    x_ref = next(it)
    g_ref = next(it) if has_norm else None
    w_ref = next(it)
    res_ref = next(it) if has_res else None
    fg_ref = next(it) if has_fnorm else None
    o_ref = next(it)
    xn_ref = next(it) if has_norm else None
    if has_norm:
        @pl.when(pl.program_id(1) == 0)
        def _():
            x = x_ref[...].astype(F32)
            ms = jnp.mean(x * x, axis=-1, keepdims=True)
            xn_ref[...] = (x * lax.rsqrt(ms + NORM_EPS) * g_ref[...]).astype(BF16)
        a = xn_ref[...]
    else:
        a = x_ref[...].astype(BF16)
    y = _dot(a, w_ref[...])
    if act == "relu2":
        y = jnp.square(jnp.maximum(y, 0.0))
    if has_res:
        y = y + res_ref[...]
    if has_fnorm:
        ms = jnp.mean(y * y, axis=-1, keepdims=True)
        y = y * lax.rsqrt(ms + NORM_EPS) * fg_ref[...]
    o_ref[...] = y.astype(o_ref.dtype)


def fused_matmul(x, w, *, gain=None, res=None, fgain=None, act=None, out_dtype=F32, tm, tn, name):
    m, k = x.shape
    n = w.shape[1]
    assert m % tm == 0 and n % tn == 0 and w.shape[0] == k
    assert fgain is None or tn == n
    in_specs = [pl.BlockSpec((tm, k), lambda i, j: (i, 0))]
    args = [x]
    if gain is not None:
        in_specs.append(pl.BlockSpec((1, k), lambda i, j: (0, 0)))
        args.append(gain.reshape(1, k).astype(F32))
    in_specs.append(pl.BlockSpec((k, tn), lambda i, j: (0, j)))
    args.append(w)
    if res is not None:
        in_specs.append(pl.BlockSpec((tm, tn), lambda i, j: (i, j)))
        args.append(res)
    if fgain is not None:
        in_specs.append(pl.BlockSpec((1, n), lambda i, j: (0, 0)))
        args.append(fgain.reshape(1, n).astype(F32))
    scratch = [pltpu.VMEM((tm, k), BF16)] if gain is not None else []
    kern = functools.partial(_mm_kernel, has_norm=gain is not None, act=act, has_res=res is not None,
                             has_fnorm=fgain is not None)
    return pl.pallas_call(
        kern,
        grid=(m // tm, n // tn),
        in_specs=in_specs,
        out_specs=pl.BlockSpec((tm, tn), lambda i, j: (i, j)),
        out_shape=jax.ShapeDtypeStruct((m, n), out_dtype),
        scratch_shapes=scratch,
        compiler_params=_cparams(("parallel", "arbitrary")),
        name=name,
    )(*args)


def _row_tile(m, pref):
    for t in (pref, 1024, 512, 256, 128, 64, 32, 16, 8):
        if t <= pref and m % t == 0:
            return t
    raise ValueError(m)


def mlp_block(h, norm_g, w_up, w_down, final_g, name):
    tm = _row_tile(h.shape[0], 1024)
    a = fused_matmul(h, w_up, gain=norm_g, act="relu2", out_dtype=BF16, tm=tm, tn=1024, name=name + "_up")
    if final_g is None:
        return fused_matmul(a, w_down, res=h, tm=tm, tn=512, name=name + "_down")
    tm = _row_tile(h.shape[0], 512)
    return fused_matmul(a, w_down, res=h, fgain=final_g, tm=tm, tn=D_MODEL, name=name + "_down")


def _ssd_kernel(z_ref, x_ref, bc_ref, dt_ref, cw_ref, cb_ref, dtb_ref, alog_ref, dsk_ref, gn_ref,
                c0_ref, s0_ref, tri_ref, ex_ref,
                y_ref, sout_ref,
                ext_ref, st_ref, xs_ref, b_ref, c_ref, *, t_valid, n_chunks):
    c = pl.program_id(1)
    L = SSD_CHUNK

    @pl.when(c == 0)
    def _():
        ext_ref[0:8, :] = c0_ref[0]
        st_ref[...] = s0_ref[0].T

    @pl.when(c > 0)
    def _():
        ext_ref[0:8, :] = ext_ref[L:L + 8, :]

    ext_ref[8:L + 8, 0:D_INNER] = x_ref[...]
    ext_ref[8:L + 8, D_INNER:CONV_DIM] = bc_ref[...]

    cblk = 512
    for i in range(CONV_DIM // cblk):
        sl = slice(i * cblk, (i + 1) * cblk)
        acc = cb_ref[:, sl]
        for k in range(CONV_WIDTH):
            acc = acc + cw_ref[k:k + 1, sl] * ext_ref[5 + k:5 + k + L, sl]
        v = _silu(acc)
        if i < 4:
            xs_ref[:, sl] = v
        elif i < 6:
            b_ref[:, (i - 4) * cblk:(i - 3) * cblk] = v
        else:
            c_ref[:, (i - 6) * cblk:(i - 5) * cblk] = v

    dt_raw = dt_ref[...] + dtb_ref[...]
    dt = jnp.maximum(dt_raw, 0.0) + jnp.log1p(jnp.exp(-jnp.abs(dt_raw)))
    if t_valid % L:
        row = c * L + lax.broadcasted_iota(jnp.int32, (L, LANES), 0)
        dt = jnp.where(row < t_valid, dt, 0.0)
    da = dt * (-jnp.exp(alog_ref[...]))
    acs = _split_dot(da, tri_ref[...], left=True)
    acs_t = acs.T
    expd = _split_dot(jnp.concatenate([acs, dt], axis=0), ex_ref[...])
    a_exp = expd[0:L]
    dt_exp = expd[L:2 * L]

    li = lax.broadcasted_iota(jnp.int32, (L, L), 0)
    si = lax.broadcasted_iota(jnp.int32, (L, L), 1)
    causal = si <= li
    first_half = si < SSM_HEAD_DIM
    gw = D_INNER // SSM_GROUPS
    for g in range(SSM_GROUPS):
        gs = slice(g * gw, (g + 1) * gw)
        bg = b_ref[:, g * SSM_STATE:(g + 1) * SSM_STATE]
        cg = c_ref[:, g * SSM_STATE:(g + 1) * SSM_STATE].astype(BF16)
        cbm = _dot_nt(cg, bg.astype(BF16))
        ag = a_exp[:, gs]
        xg = xs_ref[:, gs]
        xdt = xg * dt_exp[:, gs]
        a_last = ag[L - 1:L, :]
        y_off = _dot(cg, st_ref[:, gs].astype(BF16)) * jnp.exp(ag)
        ys = []
        for p in range(2):
            h0 = 4 * g + 2 * p
            xp = xdt[:, p * LANES:(p + 1) * LANES].astype(BF16)
            outs = []
            for h in (h0, h0 + 1):
                seg = acs[:, h:h + 1] - acs_t[h:h + 1, :]
                lm = jnp.exp(jnp.where(causal, seg, NEG_INF))
                outs.append(_dot((cbm * lm).astype(BF16), xp))
            ys.append(jnp.where(first_half, outs[0], outs[1]))
        y = jnp.concatenate(ys, axis=1) + y_off
        xd = (xdt * jnp.exp(a_last - ag)).astype(BF16)
        st_ref[:, gs] = st_ref[:, gs] * jnp.exp(a_last) + _dot(bg.T.astype(BF16), xd)
        zg = z_ref[:, gs]
        yy = (y + dsk_ref[:, gs] * xg) * _silu(zg)
        ms = jnp.mean(yy * yy, axis=-1, keepdims=True)
        y_ref[:, gs] = (yy * lax.rsqrt(ms + NORM_EPS) * gn_ref[:, gs]).astype(y_ref.dtype)

    @pl.when(c == n_chunks - 1)
    def _():
        sout_ref[0] = st_ref[...].T


def ssd_mixer(zx, conv0, st0, prm, *, batch, t_pad, t_valid, name):
    n_chunks = t_pad // SSD_CHUNK
    L = SSD_CHUNK
    conv_w, conv_b, dt_bias, a_log, d_skip, gate_norm = prm
    lane_pad = lambda v: jnp.pad(v.astype(F32), (0, LANES - SSM_HEADS)).reshape(1, LANES)
    tri = jnp.asarray(np.tril(np.ones((L, L), np.float32)), BF16)
    ex_np = np.zeros((LANES, D_INNER), np.float32)
    ex_np[np.arange(D_INNER) // SSM_HEAD_DIM, np.arange(D_INNER)] = 1.0
    ex = jnp.asarray(ex_np, BF16)
    c0 = jnp.pad(conv0.astype(F32), ((0, 0), (8 - (CONV_WIDTH - 1), 0), (0, 0)))
    row = lambda b, c: (b * n_chunks + c)
    const = lambda shape: pl.BlockSpec(shape, lambda b, c: (0,) * len(shape))
    kern = functools.partial(_ssd_kernel, t_valid=t_valid, n_chunks=n_chunks)
    y, s_out = pl.pallas_call(
        kern,
        grid=(batch, n_chunks),
        in_specs=[
            pl.BlockSpec((L, D_INNER), lambda b, c: (row(b, c), 0)),
            pl.BlockSpec((L, D_INNER), lambda b, c: (row(b, c), 1)),
            pl.BlockSpec((L, D_INNER), lambda b, c: (row(b, c), 2)),
            pl.BlockSpec((L, LANES), lambda b, c: (row(b, c), (D_INNER + CONV_DIM) // LANES)),
            const((CONV_WIDTH, CONV_DIM)), const((1, CONV_DIM)), const((1, LANES)), const((1, LANES)),
            const((1, D_INNER)), const((1, D_INNER)),
            pl.BlockSpec((1, 8, CONV_DIM), lambda b, c: (b, 0, 0)),
            pl.BlockSpec((1, D_INNER, SSM_STATE), lambda b, c: (b, 0, 0)),
            const((L, L)), const((LANES, D_INNER)),
        ],
        out_specs=[
            pl.BlockSpec((L, D_INNER), lambda b, c: (row(b, c), 0)),
            pl.BlockSpec((1, D_INNER, SSM_STATE), lambda b, c: (b, 0, 0)),
        ],
        out_shape=[
            jax.ShapeDtypeStruct((batch * t_pad, D_INNER), BF16),
            jax.ShapeDtypeStruct((batch, D_INNER, SSM_STATE), F32),
        ],
        scratch_shapes=[
            pltpu.VMEM((L + 8, CONV_DIM), F32),
            pltpu.VMEM((SSM_STATE, D_INNER), F32),
            pltpu.VMEM((L, D_INNER), F32),
            pltpu.VMEM((L, SSM_GROUPS * SSM_STATE), F32),
            pltpu.VMEM((L, SSM_GROUPS * SSM_STATE), F32),
        ],
        compiler_params=_cparams(("parallel", "arbitrary")),
        name=name,
    )(zx, zx, zx, zx,
      conv_w.astype(F32).T, conv_b.astype(F32).reshape(1, CONV_DIM), lane_pad(dt_bias), lane_pad(a_log),
      jnp.repeat(d_skip.astype(F32), SSM_HEAD_DIM).reshape(1, D_INNER), gate_norm.astype(F32).reshape(1, D_INNER),
      c0, st0.astype(F32).reshape(batch, D_INNER, SSM_STATE), tri, ex)
    return y, s_out.reshape(batch, SSM_HEADS, SSM_HEAD_DIM, SSM_STATE)


CMP_PAGES = 16


def _cmp_u_kernel(pt_ref, *refs, feature_major):
    del pt_ref
    pages = refs[:CMP_PAGES]
    perm_ref, w_ref, uf_ref, us_ref, y_ref = refs[CMP_PAGES:]
    regroup = _dot_nt if feature_major else _dot
    for p, page in enumerate(pages):
        yp = regroup(perm_ref[...], page[...].astype(BF16))
        for j in range(CMP_STRIDE):
            y_ref[j, p * 8:(p + 1) * 8, :] = yp[j * 8:(j + 1) * 8, :]
    rows = y_ref.shape[1]
    low = lax.broadcasted_iota(jnp.int32, (rows, LANES), 1) < HEAD_DIM
    accs = [None] * (2 * KV_HEADS)
    for a in range(CMP_STRIDE // 4):
        for q in range(4):
            blocks = [y_ref[4 * a + jj, :, q * LANES:(q + 1) * LANES] for jj in range(4)]
            swapped = [pltpu.roll(x, HEAD_DIM, axis=1) for x in blocks]
            for gl in range(2):
                if gl == 0:
                    pairs = [jnp.where(low, blocks[0], swapped[1]), jnp.where(low, blocks[2], swapped[3])]
                else:
                    pairs = [jnp.where(low, swapped[0], blocks[1]), jnp.where(low, swapped[2], blocks[3])]
                part = _dot(jnp.concatenate(pairs, axis=1).astype(BF16), w_ref[q // 2, a])
                k = 2 * q + gl
                accs[k] = part if accs[k] is None else accs[k] + part
    for k in range(2 * KV_HEADS):
        uf_ref[0, k] = accs[k][:, 0:CMP_HIDDEN]
        us_ref[0, k] = accs[k][:, CMP_HIDDEN:2 * CMP_HIDDEN]


def cmp_project(src2d, page_ids, col_block, w1, name, feature_major=False):
    batch, n_pages = page_ids.shape
    assert n_pages % CMP_PAGES == 0
    n_sub = n_pages * (PAGE_SIZE // CMP_STRIDE)
    rows = CMP_PAGES * (PAGE_SIZE // CMP_STRIDE)
    w1r = w1.astype(F32).reshape(2, 2 * CMP_STRIDE, HEAD_DIM, CMP_HIDDEN)
    wj = jnp.concatenate([w1r[:, :CMP_STRIDE], w1r[:, CMP_STRIDE:]], axis=-1)
    wpad = wj.reshape(2, CMP_STRIDE // 4, 4 * HEAD_DIM, 2 * CMP_HIDDEN).astype(BF16)
    perm = np.zeros((PAGE_SIZE, PAGE_SIZE), np.float32)
    for j in range(CMP_STRIDE):
        for s in range(PAGE_SIZE // CMP_STRIDE):
            perm[j * 8 + s, CMP_STRIDE * s + j] = 1.0
    perm = jnp.asarray(perm, BF16)

    def page_spec(p):
        shape = (4 * LANES, PAGE_SIZE) if feature_major else (PAGE_SIZE, 4 * LANES)
        return pl.BlockSpec(shape, lambda b, c, pt: (pt[b, c * CMP_PAGES + p], col_block))

    u_shape = jax.ShapeDtypeStruct((batch, 2 * KV_HEADS, n_sub, CMP_HIDDEN), F32)
    u_spec = pl.BlockSpec((1, 2 * KV_HEADS, rows, CMP_HIDDEN), lambda b, c, pt: (b, 0, c, 0))
    return pl.pallas_call(
        functools.partial(_cmp_u_kernel, feature_major=feature_major),
        grid_spec=pltpu.PrefetchScalarGridSpec(
            num_scalar_prefetch=1,
            grid=(batch, n_pages // CMP_PAGES),
            in_specs=[page_spec(p) for p in range(CMP_PAGES)] + [
                pl.BlockSpec((PAGE_SIZE, PAGE_SIZE), lambda b, c, pt: (0, 0)),
                pl.BlockSpec((2, CMP_STRIDE // 4, 4 * HEAD_DIM, 2 * CMP_HIDDEN), lambda b, c, pt: (0, 0, 0, 0)),
            ],
            out_specs=[u_spec, u_spec],
            scratch_shapes=[pltpu.VMEM((CMP_STRIDE, rows, 4 * LANES), F32)],
        ),
        out_shape=[u_shape, u_shape],
        compiler_params=_cparams(("parallel", "arbitrary")),
        name=name,
    )(page_ids, *([src2d] * CMP_PAGES), perm, wpad)


def _cmp_out_kernel(uf_ref, us_ref, pos_ref, w1_ref, w2_ref, o_ref, sh_ref, *, n_sub):
    pos_term = _dot(pos_ref[0].astype(BF16), w1_ref[0])[0:1, :]
    rows = lax.broadcasted_iota(jnp.int32, (n_sub, HEAD_DIM), 0)
    for g in range(KV_HEADS):
        sh_ref[g, 0:n_sub, :] = us_ref[0, g]
        sh_ref[g, n_sub:n_sub + 8, :] = jnp.zeros((8, CMP_HIDDEN), F32)
        hid = jax.nn.gelu(uf_ref[0, g] + sh_ref[g, pl.ds(1, n_sub), :] + pos_term)
        out = _dot(hid.astype(BF16), w2_ref[0])
        o_ref[0, g] = jnp.where(rows < n_sub - 1, out, 0.0).astype(o_ref.dtype)


def cmp_finish(uf, us, pos_emb, w1, w2, name):
    batch, nkvg, n_sub, _ = uf.shape
    flat = 2 * CMP_STRIDE * HEAD_DIM
    pos = jnp.pad(pos_emb.astype(F32).reshape(2, 1, flat), ((0, 0), (0, 7), (0, 0)))
    assert nkvg == 2 * KV_HEADS
    u_spec = pl.BlockSpec((1, KV_HEADS, n_sub, CMP_HIDDEN), lambda b, kv: (b, kv, 0, 0))
    return pl.pallas_call(
        functools.partial(_cmp_out_kernel, n_sub=n_sub),
        grid=(batch, 2),
        in_specs=[u_spec, u_spec,
                  pl.BlockSpec((1, 8, flat), lambda b, kv: (kv, 0, 0)),
                  pl.BlockSpec((1, flat, CMP_HIDDEN), lambda b, kv: (kv, 0, 0)),
                  pl.BlockSpec((1, CMP_HIDDEN, HEAD_DIM), lambda b, kv: (kv, 0, 0))],
        out_specs=pl.BlockSpec((1, KV_HEADS, n_sub, HEAD_DIM), lambda b, kv: (b, kv, 0, 0)),
        out_shape=jax.ShapeDtypeStruct((batch, nkvg, n_sub, HEAD_DIM), BF16),
        scratch_shapes=[pltpu.VMEM((KV_HEADS, n_sub + 8, CMP_HIDDEN), F32)],
        compiler_params=_cparams(("parallel", "parallel")),
        name=name,
    )(uf, us, pos, w1.astype(BF16), w2.astype(BF16))


def _cmp_attn_kernel(q_ref, kc_ref, vc_ref, wsel_ref, o_ref, sel_ref, *rest, tq, ncp, nblk_pad, n_sel, q_base,
                     want_idx):
    qi = pl.program_id(2)
    use_ref = None if want_idx else rest[0]
    tpos = q_base + qi * tq + lax.broadcasted_iota(jnp.int32, (tq, 1), 0)
    c_end = CMP_STRIDE * lax.broadcasted_iota(jnp.int32, (1, ncp), 1) + (2 * CMP_STRIDE - 1)
    ok = c_end <= tpos
    any_ok = (tpos >= 2 * CMP_STRIDE - 1).astype(F32)
    kc = kc_ref[0, 0]
    vw = jnp.concatenate([vc_ref[0, 0], wsel_ref[...]], axis=1)
    p_slc = None
    if tq < LANES:
        s3 = _dot_nt(q_ref[0].reshape(GQA * tq, LANES), kc).reshape(GQA, tq, ncp)
        s3 = jnp.where(ok[None], s3, NEG_INF)
        e3 = jnp.exp2(s3 - jnp.max(s3, axis=-1, keepdims=True))
        r3 = _dot(e3.reshape(GQA * tq, ncp).astype(BF16), vw).reshape(GQA, tq, LANES + nblk_pad)
        inv3 = any_ok[None] / r3[:, :, HEAD_DIM:HEAD_DIM + 1]
        o_ref[0] = r3[:, :, 0:HEAD_DIM] * inv3
        p_slc = jnp.sum(r3[:, :, LANES:] * inv3, axis=0)
    for h in range(GQA if tq >= LANES else 0):
        s = jnp.where(ok, _dot_nt(q_ref[0, h], kc), NEG_INF)
        e = jnp.exp2(s - jnp.max(s, axis=-1, keepdims=True))
        r = _dot(e.astype(BF16), vw)
        inv = any_ok / r[:, HEAD_DIM:HEAD_DIM + 1]
        o_ref[0, h] = r[:, 0:HEAD_DIM] * inv
        part = r[:, LANES:] * inv
        p_slc = part if p_slc is None else p_slc + part
    tq_l = max(tq, LANES)
    if tq < LANES:
        p_slc = jnp.concatenate([p_slc, jnp.zeros((tq_l - tq, nblk_pad), F32)], axis=0)
    sc = p_slc.T
    blk = lax.broadcasted_iota(jnp.int32, (nblk_pad, tq_l), 0)
    tp = q_base + qi * tq + lax.broadcasted_iota(jnp.int32, (nblk_pad, tq_l), 1)
    cur = jnp.right_shift(tp, SLC_SHIFT)
    forced = (blk == 0) | (blk == cur) | (blk == cur - 1)
    reach = blk * SLC_BLOCK <= tp
    sc = jnp.where(forced, 1e30, jnp.where(reach, sc, -1.0))
    sc = jnp.where(blk < n_sel, sc, -3.0)
    removed = -2.0
    idx_rows = []
    for _ in range(min(SLC_TOP_N, n_sel)):
        mx = jnp.max(sc, axis=0, keepdims=True)
        idx = jnp.min(jnp.where(sc == mx, blk, nblk_pad), axis=0, keepdims=True)
        sc = jnp.where(blk == idx, removed, sc)
        idx_rows.append(idx)
    if want_idx:
        sel_ref[0, 0] = jnp.concatenate(idx_rows, axis=0)
    else:
        chosen_t = (sc == removed).astype(F32).T[0:tq]
        sel_ref[0, 0] = ((chosen_t - 1.0) * 1e30).astype(sel_ref.dtype)
        used = jnp.max(chosen_t, axis=0, keepdims=True)
        use_ref[0, 0, 0] = jnp.broadcast_to(used, (8, nblk_pad))


LOG2E = 1.4426950408889634


def _query_feats(q_h, slopes):
    sl = slopes * LOG2E
    s1 = sl.astype(BF16)
    r1 = sl - s1.astype(F32)
    s2 = r1.astype(BF16)
    s3 = (r1 - s2.astype(F32)).astype(BF16)
    feats = jnp.stack([s1, s2, s3, s1, s2, s3, jnp.ones_like(s1)], axis=-1)
    return _with_feats((q_h * (LOG2E * HEAD_DIM ** -0.5)).astype(BF16), feats[None, :, None, :], LANES)


def _pos_feats(pos, invalid=None):
    lo = jnp.bitwise_and(pos, 255)
    hi = pos - lo
    inv = jnp.zeros_like(pos, F32) if invalid is None else jnp.where(invalid, NEG_INF, 0.0)
    return jnp.concatenate([jnp.stack([lo, lo, lo, hi, hi, hi], axis=-1).astype(BF16), inv.astype(BF16)[..., None]],
                           axis=-1)


def _with_feats(x, feats, width):
    lead = x.shape[:-1]
    f = jnp.broadcast_to(feats, lead + feats.shape[-1:]).astype(BF16)
    pad = jnp.zeros(lead + (width - x.shape[-1] - f.shape[-1],), BF16)
    return jnp.concatenate([x.astype(BF16), f, pad], axis=-1)


def _with_ones(v):
    return _with_feats(v, jnp.ones((1,), BF16), LANES)


def cmp_attention(qa, kvc, *, tq, n_sel, q_base, want_idx, name):
    b, _, t, _ = qa.shape
    ncp = kvc.shape[2]
    nblk_pad = -(-n_sel // LANES) * LANES
    c_end = CMP_STRIDE * jnp.arange(ncp, dtype=jnp.int32) + (2 * CMP_STRIDE - 1)
    kca = _with_feats(kvc[:, :KV_HEADS], _pos_feats(c_end), LANES)
    vca = _with_ones(kvc[:, KV_HEADS:])
    c = np.arange(ncp)[:, None]
    j = np.arange(nblk_pad)[None, :]
    wsel = ((c // 4 == j).astype(np.float32) + ((c + 1) // 4 == j).astype(np.float32)) * (c < ncp - 1)
    tq_l = max(tq, LANES)
    if want_idx:
        sel_shape = jax.ShapeDtypeStruct((b, KV_HEADS, SLC_TOP_N, (t // tq) * tq_l), jnp.int32)
        sel_spec = pl.BlockSpec((1, 1, SLC_TOP_N, tq_l), lambda bi, g, qi: (bi, g, 0, qi))
    else:
        sel_shape = jax.ShapeDtypeStruct((b, KV_HEADS, t, nblk_pad), BF16)
        sel_spec = pl.BlockSpec((1, 1, tq, nblk_pad), lambda bi, g, qi: (bi, g, qi, 0))
    kern = functools.partial(_cmp_attn_kernel, tq=tq, ncp=ncp, nblk_pad=nblk_pad, n_sel=n_sel, q_base=q_base,
                             want_idx=want_idx)
    out_specs = [pl.BlockSpec((1, GQA, tq, HEAD_DIM), lambda bi, g, qi: (bi, g, qi, 0)), sel_spec]
    out_shape = [jax.ShapeDtypeStruct((b, ATT_HEADS, t, HEAD_DIM), F32), sel_shape]
    if not want_idx:
        out_specs.append(pl.BlockSpec((1, 1, 1, 8, nblk_pad), lambda bi, g, qi: (bi, g, qi, 0, 0)))
        out_shape.append(jax.ShapeDtypeStruct((b, KV_HEADS, t // tq, 8, nblk_pad), F32))
    return pl.pallas_call(
        kern,
        grid=(b, KV_HEADS, t // tq),
        in_specs=[
            pl.BlockSpec((1, GQA, tq, LANES), lambda bi, g, qi: (bi, g, qi, 0)),
            pl.BlockSpec((1, 1, ncp, LANES), lambda bi, g, qi: (bi, g, 0, 0)),
            pl.BlockSpec((1, 1, ncp, LANES), lambda bi, g, qi: (bi, g, 0, 0)),
            pl.BlockSpec((ncp, nblk_pad), lambda bi, g, qi: (0, 0)),
        ],
        out_specs=out_specs,
        out_shape=out_shape,
        compiler_params=_cparams(("parallel", "parallel", "parallel")),
        name=name,
    )(qa, kca, vca, jnp.asarray(wsel, BF16))


SEL_FEAT = 256


def _sel_kernel(use_ref, qa_ref, bias_ref, ka_ref, v_ref, o_ref, m_ref, acc_ref, *, tq, tk):
    bi = pl.program_id(0)
    g = pl.program_id(1)
    qi = pl.program_id(2)
    n_qt = pl.num_programs(2)
    per = tq // tk
    n_kc = n_qt * per
    m_ref[...] = jnp.full(m_ref.shape, NEG_INF, F32)
    acc_ref[...] = jnp.zeros(acc_ref.shape, F32)
    rows = lax.broadcasted_iota(jnp.int32, (tq, tk), 0)
    cols = lax.broadcasted_iota(jnp.int32, (tq, tk), 1)

    def chunk(j, diag_offset):
        k0 = pl.multiple_of(j * tk, tk)
        ka = ka_ref[0, 0, pl.ds(k0, tk), :]
        va = v_ref[0, 0, pl.ds(k0, tk), :]
        bias = bias_ref[0, 0]
        for h in range(GQA):
            qf = jnp.concatenate([qa_ref[0, h], bias], axis=1)
            s = _dot_nt(qf, ka)
            if diag_offset is not None:
                s = jnp.where(cols + diag_offset <= rows, s, NEG_INF)
            m_prev = m_ref[h]
            m_new = jnp.maximum(m_prev, jnp.max(s, axis=-1, keepdims=True))
            p = jnp.exp2(s - jnp.concatenate([m_new] * (tk // LANES), axis=1))
            acc_ref[h] = jnp.exp2(m_prev - m_new) * acc_ref[h] + _dot(p.astype(BF16), va)
            m_ref[h] = m_new

    def body(j, carry):
        @pl.when(use_ref[((bi * KV_HEADS + g) * n_qt + qi) * n_kc + j] > 0)
        def _():
            chunk(j, None)
        return carry

    lax.fori_loop(0, qi * per, body, 0)
    for c in range(per):
        chunk(qi * per + c, c * tk)
    for h in range(GQA):
        acc = acc_ref[h]
        o_ref[0, h] = acc[:, 0:HEAD_DIM] / acc[:, HEAD_DIM:HEAD_DIM + 1]


def sel_attention_dense(chunk_used, qa, bias, ka, va, *, tq, tk, name):
    b, _, t, _ = qa.shape
    return pl.pallas_call(
        functools.partial(_sel_kernel, tq=tq, tk=tk),
        grid_spec=pltpu.PrefetchScalarGridSpec(
            num_scalar_prefetch=1,
            grid=(b, KV_HEADS, t // tq),
            in_specs=[
                pl.BlockSpec((1, GQA, tq, LANES), lambda bi, g, qi, *_: (bi, g, qi, 0)),
                pl.BlockSpec((1, 1, tq, LANES), lambda bi, g, qi, *_: (bi, g, qi, 0)),
                pl.BlockSpec((1, 1, t, SEL_FEAT), lambda bi, g, qi, *_: (bi, g, 0, 0)),
                pl.BlockSpec((1, 1, t, LANES), lambda bi, g, qi, *_: (bi, g, 0, 0)),
            ],
            out_specs=pl.BlockSpec((1, GQA, tq, HEAD_DIM), lambda bi, g, qi, *_: (bi, g, qi, 0)),
            scratch_shapes=[pltpu.VMEM((GQA, tq, LANES), F32), pltpu.VMEM((GQA, tq, LANES), F32)],
        ),
        out_shape=jax.ShapeDtypeStruct((b, ATT_HEADS, t, HEAD_DIM), F32),
        compiler_params=_cparams(("parallel", "parallel", "arbitrary")),
        name=name,
    )(chunk_used.reshape(-1), qa, bias, ka, va)


def _sel_gather_kernel(idx_ref, pt_ref, q_ref, slope_ref, cache_ref, new_ref, o_ref, kbuf, vbuf, sem, *,
                       n_tok, tq, past_len, n_steps):
    b = pl.program_id(0)
    g = pl.program_id(1)
    step = b * KV_HEADS + g
    slot = step % 2
    new_blk = past_len // SLC_BLOCK
    halves = PAGE_SIZE // SLC_BLOCK

    def block_copy(bb, gg, sl, t, n, kv, blk, newest):
        buf = vbuf if kv else kbuf
        dst = buf.at[sl, t, :, pl.ds(n * PAGE_SIZE, PAGE_SIZE)]
        if newest:
            src = new_ref.at[bb, kv, gg]
        else:
            page = pt_ref[bb * (past_len // PAGE_SIZE) + jnp.minimum(blk, new_blk - 1) // halves]
            src = cache_ref.at[page, kv, gg]
        return pltpu.make_async_copy(src, dst, sem.at[sl])

    def for_each_block(bb, gg, fn):
        for t in range(n_tok):
            for n in range(SLC_TOP_N):
                blk = idx_ref[((bb * KV_HEADS + gg) * SLC_TOP_N + n) * n_tok + t]
                for kv in range(2):
                    fn(t, n, kv, blk)

    def start(bb, gg, sl):
        def one(t, n, kv, blk):
            @pl.when(blk >= new_blk)
            def _():
                block_copy(bb, gg, sl, t, n, kv, blk, True).start()

            @pl.when(blk < new_blk)
            def _():
                block_copy(bb, gg, sl, t, n, kv, blk, False).start()
        for_each_block(bb, gg, one)

    @pl.when(step == 0)
    def _():
        start(b, g, slot)

    @pl.when(step + 1 < n_steps)
    def _():
        nxt = step + 1
        start(nxt // KV_HEADS, nxt % KV_HEADS, 1 - slot)

    for_each_block(b, g, lambda t, n, kv, blk: block_copy(b, g, slot, t, n, kv, blk, False).wait())

    r = GQA * tq
    q = (q_ref[0].reshape(r, HEAD_DIM) * (HEAD_DIM ** -0.5)).astype(BF16)
    nk = SLC_TOP_N * PAGE_SIZE
    lane = lax.broadcasted_iota(jnp.int32, (1, nk), 1)
    trow = lax.broadcasted_iota(jnp.int32, (tq, 1), 0)
    tpos = past_len + trow
    acc = jnp.zeros((GQA, tq, HEAD_DIM), F32)
    in_page = jnp.bitwise_and(lane, PAGE_SIZE - 1)
    half_of_lane = jnp.right_shift(in_page, SLC_SHIFT)
    slot_of_lane = jnp.right_shift(lane, SLC_SHIFT + 1)
    for t in range(n_tok):
        kpos = in_page
        chosen_half = jnp.zeros((1, nk), jnp.int32)
        for n in range(SLC_TOP_N):
            blk = idx_ref[((b * KV_HEADS + g) * SLC_TOP_N + n) * n_tok + t]
            here = slot_of_lane == n
            kpos = jnp.where(here, kpos + (blk // halves) * PAGE_SIZE, kpos)
            chosen_half = jnp.where(here, blk % halves, chosen_half)
        d = (tpos - kpos).astype(F32)
        ok = (d >= 0.0) & (half_of_lane == chosen_half)
        s = _dot(q, kbuf[slot, t].astype(BF16))
        s3 = s.reshape(GQA, tq, nk) - slope_ref[0] * d[None]
        s3 = jnp.where(ok[None], s3, NEG_INF)
        m = jnp.max(s3, axis=-1, keepdims=True)
        e = jnp.exp(s3 - m)
        p = e / jnp.sum(e, axis=-1, keepdims=True)
        o = _dot_nt(p.reshape(r, nk).astype(BF16), vbuf[slot, t].astype(BF16)).reshape(GQA, tq, HEAD_DIM)
        acc = jnp.where((trow == t)[None], o, acc)
    o_ref[0] = acc


def sel_attention_gather(q_h, idx, page_table, cache, new_kv, slopes, *, n_tok, past_len, name):
    b, _, tq, _ = q_h.shape
    n_steps = b * KV_HEADS
    nk = SLC_TOP_N * PAGE_SIZE
    kern = functools.partial(_sel_gather_kernel, n_tok=n_tok, tq=tq, past_len=past_len, n_steps=n_steps)
    return pl.pallas_call(
        kern,
        grid_spec=pltpu.PrefetchScalarGridSpec(
            num_scalar_prefetch=2,
            grid=(b, KV_HEADS),
            in_specs=[
                pl.BlockSpec((1, GQA, tq, HEAD_DIM), lambda bi, g, *_: (bi, g, 0, 0)),
                pl.BlockSpec((1, GQA, 1, 1), lambda bi, g, *_: (g, 0, 0, 0)),
                pl.BlockSpec(memory_space=pl.ANY),
                pl.BlockSpec(memory_space=pl.ANY),
            ],
            out_specs=pl.BlockSpec((1, GQA, tq, HEAD_DIM), lambda bi, g, *_: (bi, g, 0, 0)),
            scratch_shapes=[pltpu.VMEM((2, n_tok, HEAD_DIM, nk), F32), pltpu.VMEM((2, n_tok, HEAD_DIM, nk), F32),
                            pltpu.SemaphoreType.DMA((2,))],
        ),
        out_shape=jax.ShapeDtypeStruct((b, ATT_HEADS, tq, HEAD_DIM), F32),
        compiler_params=_cparams(("arbitrary", "arbitrary")),
        name=name,
    )(idx.reshape(-1), page_table.reshape(-1), q_h, slopes.reshape(KV_HEADS, GQA, 1, 1), cache, new_kv)


def _win_kernel(q_ref, k_ref, v_ref, oc_ref, os_ref, gl_ref, o_ref, *, tq, nk):
    qi = pl.program_id(2)
    r0 = pl.multiple_of(qi * tq, tq)
    kw = k_ref[0, 0, pl.ds(r0, nk), :]
    vw = v_ref[0, 0, pl.ds(r0, nk), :]
    blockwise = tq % LANES == 0 and WINDOW % tq == 0
    if blockwise:
        rows = lax.broadcasted_iota(jnp.int32, (tq, tq), 0)
        cols = lax.broadcasted_iota(jnp.int32, (tq, tq), 1)
        old_ok = cols > rows
        new_ok = cols <= rows
    else:
        di = (WINDOW + lax.broadcasted_iota(jnp.int32, (tq, 1), 0)) - lax.broadcasted_iota(jnp.int32, (1, nk), 1)
        ok = (di >= 0) & (di < WINDOW)
    if tq < LANES:
        s3 = _dot_nt(q_ref[0].reshape(GQA * tq, LANES), kw).reshape(GQA, tq, nk)
        s3 = jnp.where(ok[None], s3, NEG_INF)
        e3 = jnp.exp2(s3 - jnp.max(s3, axis=-1, keepdims=True))
        r3 = _dot(e3.reshape(GQA * tq, nk).astype(BF16), vw).reshape(GQA, tq, LANES)
        ow3 = r3[:, :, 0:HEAD_DIM] / r3[:, :, HEAD_DIM:HEAD_DIM + 1]
        gates3 = 1.0 / (1.0 + jnp.exp(-gl_ref[0]))
        o_ref[0] = gates3[:, :, 0:1] * oc_ref[0] + gates3[:, :, 1:2] * os_ref[0] + gates3[:, :, 2:3] * ow3
    for h in range(GQA if tq >= LANES else 0):
        s = _dot_nt(q_ref[0, h], kw)
        if blockwise:
            s = jnp.concatenate([jnp.where(old_ok, s[:, 0:tq], NEG_INF), s[:, tq:nk - tq],
                                 jnp.where(new_ok, s[:, nk - tq:nk], NEG_INF)], axis=1)
        else:
            s = jnp.where(ok, s, NEG_INF)
        e = jnp.exp2(s - jnp.max(s, axis=-1, keepdims=True))
        r = _dot(e.astype(BF16), vw)
        ow = r[:, 0:HEAD_DIM] / r[:, HEAD_DIM:HEAD_DIM + 1]
        gates = 1.0 / (1.0 + jnp.exp(-gl_ref[0, h]))
        o_ref[0, h] = gates[:, 0:1] * oc_ref[0, h] + gates[:, 1:2] * os_ref[0, h] + gates[:, 2:3] * ow


def win_attention_merge(qa, kw, vw, o_c, o_s, gate_logits, *, tq, k_valid_from, name):
    b, _, t, _ = qa.shape
    nk = tq + WINDOW
    lk = kw.shape[2]
    assert lk >= t + WINDOW
    krow = jnp.arange(lk, dtype=jnp.int32)
    kwa = _with_feats(kw, _pos_feats(krow, invalid=krow < k_valid_from), LANES)
    vwa = _with_ones(vw)
    head_spec = lambda w: pl.BlockSpec((1, GQA, tq, w), lambda bi, g, qi: (bi, g, qi, 0))
    kv_spec = pl.BlockSpec((1, 1, lk, LANES), lambda bi, g, qi: (bi, g, 0, 0))
    return pl.pallas_call(
        functools.partial(_win_kernel, tq=tq, nk=nk),
        grid=(b, KV_HEADS, t // tq),
        in_specs=[head_spec(LANES), kv_spec, kv_spec,
                  head_spec(HEAD_DIM), head_spec(HEAD_DIM), head_spec(N_BRANCH)],
        out_specs=head_spec(HEAD_DIM),
        out_shape=jax.ShapeDtypeStruct((b, ATT_HEADS, t, HEAD_DIM), F32),
        compiler_params=_cparams(("parallel", "parallel", "parallel")),
        name=name,
    )(qa, kwa, vwa, o_c, o_s, gate_logits)


def _heads(x2d, b, t, n_heads):
    return x2d.reshape(b, t, n_heads, HEAD_DIM).transpose(0, 2, 1, 3)


def _kv_stack(qkv, b, t, col):
    return qkv[:, col:col + 2 * KV_DIM].reshape(b, t, 2, KV_HEADS, HEAD_DIM)


def alibi_slopes():
    return 2.0 ** (-8.0 * jnp.arange(1, ATT_HEADS + 1, dtype=jnp.float32) / ATT_HEADS)


def nsa_prompt_layer(h2d, b, t, norm_g, w_in, cmp_pos, cmp_w1, cmp_w2, w_out, slopes):
    tm = _row_tile(h2d.shape[0], 1024)
    qkv = fused_matmul(h2d, w_in, gain=norm_g, tm=tm, tn=ATT_IN_PAD // 3, name="att_in_p")
    q_h = _heads(qkv[:, :D_MODEL], b, t, ATT_HEADS)
    gl_h = qkv[:, D_MODEL + 6 * KV_DIM:D_MODEL + 6 * KV_DIM + ATT_HEADS * N_BRANCH]
    gl_h = gl_h.reshape(b, t, ATT_HEADS, N_BRANCH).transpose(0, 2, 1, 3)

    n_pages = t // PAGE_SIZE
    page_ids = (jnp.arange(b, dtype=jnp.int32)[:, None] * n_pages + jnp.arange(n_pages, dtype=jnp.int32)[None, :])
    uf, us = cmp_project(qkv, page_ids, D_MODEL // (2 * KV_DIM), cmp_w1, "cmp_u_p")
    kvc = cmp_finish(uf, us, cmp_pos, cmp_w1, cmp_w2, "cmp_out_p")

    n_sel = t // SLC_BLOCK
    tq = 256
    tq_cmp = min(512, t)
    qa = _query_feats(q_h, slopes)
    o_c, sel_bias, blk_used = cmp_attention(qa, kvc, tq=tq_cmp, n_sel=n_sel, q_base=0, want_idx=False,
                                            name="cmp_attn_p")
    tq_sel, tk_sel = min(512, t), min(512, t)
    used = blk_used[:, :, :, 0, :n_sel].reshape(b, KV_HEADS, t // tq_sel, tq_sel // tq_cmp, t // tk_sel,
                                                tk_sel // SLC_BLOCK)
    chunk_used = (used.max(axis=(3, 5)) > 0.0).astype(jnp.int32)

    nb = sel_bias.shape[-1]
    assert LANES + nb == SEL_FEAT
    ks = _heads(qkv[:, D_MODEL + 2 * KV_DIM:D_MODEL + 3 * KV_DIM], b, t, KV_HEADS).astype(BF16)
    vs = _heads(qkv[:, D_MODEL + 3 * KV_DIM:D_MODEL + 4 * KV_DIM], b, t, KV_HEADS).astype(BF16)
    kpos = jnp.arange(t, dtype=jnp.int32)
    onehot = jax.nn.one_hot(kpos // SLC_BLOCK, nb, dtype=BF16)
    ka = jnp.concatenate([_with_feats(ks, _pos_feats(kpos), LANES),
                          jnp.broadcast_to(onehot, (b, KV_HEADS) + onehot.shape)], axis=-1)
    o_s = sel_attention_dense(chunk_used, qa, sel_bias, ka, _with_ones(vs), tq=tq_sel, tk=tk_sel, name="sel_attn_p")

    front = ((0, 0), (0, 0), (WINDOW, 0), (0, 0))
    kw = jnp.pad(_heads(qkv[:, D_MODEL + 4 * KV_DIM:D_MODEL + 5 * KV_DIM], b, t, KV_HEADS).astype(BF16), front)
    vw = jnp.pad(_heads(qkv[:, D_MODEL + 5 * KV_DIM:D_MODEL + 6 * KV_DIM], b, t, KV_HEADS).astype(BF16), front)
    o = win_attention_merge(qa, kw, vw, o_c, o_s, gl_h, tq=tq, k_valid_from=WINDOW, name="win_attn_p")
    o2d = o.transpose(0, 2, 1, 3).reshape(b * t, D_MODEL)
    h_new = fused_matmul(o2d, w_out, res=h2d, tm=tm, tn=512, name="att_out_p")
    win_rows = min(WINDOW, t)
    return (h_new, _kv_stack(qkv, b, t, D_MODEL), _kv_stack(qkv, b, t, D_MODEL + 2 * KV_DIM),
            _kv_stack(qkv, b, t, D_MODEL + 4 * KV_DIM)[:, t - win_rows:])


def nsa_sample_layer(h2d, b, t, cache_cmp, cache_slc, win_buf, page_table, norm_g, w_in, cmp_pos, cmp_w1, cmp_w2,
                     w_out, slopes):
    n_pages = page_table.shape[1]
    past_len = n_pages * PAGE_SIZE
    tq = 16
    assert t <= tq and t <= 2 * CMP_STRIDE - 1
    qkv = fused_matmul(h2d, w_in, gain=norm_g, tm=h2d.shape[0], tn=ATT_IN_PAD // 3, name="att_in_s")
    tpad = ((0, 0), (0, 0), (0, tq - t), (0, 0))
    q_h = jnp.pad(_heads(qkv[:, :D_MODEL], b, t, ATT_HEADS), tpad)
    gl_h = qkv[:, D_MODEL + 6 * KV_DIM:D_MODEL + 6 * KV_DIM + ATT_HEADS * N_BRANCH]
    gl_h = jnp.pad(gl_h.reshape(b, t, ATT_HEADS, N_BRANCH).transpose(0, 2, 1, 3), tpad)
    new_cmp = _kv_stack(qkv, b, t, D_MODEL)
    new_slc = _kv_stack(qkv, b, t, D_MODEL + 2 * KV_DIM)
    new_win = _kv_stack(qkv, b, t, D_MODEL + 4 * KV_DIM)

    cmp_fm = cache_cmp.transpose(0, 2, 3, 4, 1).reshape(-1, PAGE_SIZE)
    slc_fm = cache_slc.transpose(0, 2, 3, 4, 1)
    uf, us = cmp_project(cmp_fm, page_table, 0, cmp_w1, "cmp_u_s", feature_major=True)
    kvc = cmp_finish(uf, us, cmp_pos, cmp_w1, cmp_w2, "cmp_out_s")
    l_all = past_len + t
    n_sel = -(-l_all // SLC_BLOCK)
    qa = _query_feats(q_h, slopes)
    o_c, idx = cmp_attention(qa, kvc, tq=tq, n_sel=n_sel, q_base=past_len, want_idx=True, name="cmp_attn_s")

    new_page = jnp.pad(new_slc, ((0, 0), (0, PAGE_SIZE - t), (0, 0), (0, 0), (0, 0)))
    o_s = sel_attention_gather(q_h, idx[..., :t], page_table, slc_fm, new_page.transpose(0, 2, 3, 4, 1), slopes,
                               n_tok=t, past_len=past_len, name="sel_attn_s")

    w_rows = win_buf.shape[1]
    win_all = jnp.concatenate([win_buf.astype(F32), new_win], axis=1)
    lk = tq + WINDOW
    shift = WINDOW - w_rows
    kvw = jnp.pad(win_all, ((0, 0), (shift, lk - shift - w_rows - t), (0, 0), (0, 0), (0, 0))).astype(BF16)
    kw = kvw[:, :, 0].transpose(0, 2, 1, 3)
    vw = kvw[:, :, 1].transpose(0, 2, 1, 3)
    o = win_attention_merge(qa, kw, vw, o_c, o_s, gl_h, tq=tq, k_valid_from=shift, name="win_attn_s")
    o2d = o[:, :, :t].transpose(0, 2, 1, 3).reshape(b * t, D_MODEL)
    h_new = fused_matmul(o2d, w_out, res=h2d, tm=h2d.shape[0], tn=512, name="att_out_s")
    return h_new, new_cmp, new_slc, win_all[:, -w_rows:]


def ssm_layer(h2d, b, t, conv_state, ssm_state, norm_g, w_in, conv_prm, w_out, tag):
    t_pad = -(-t // SSD_CHUNK) * SSD_CHUNK
    tm = _row_tile(h2d.shape[0], 1024)
    zx = fused_matmul(h2d, w_in, gain=norm_g, tm=tm, tn=1280, name="ssm_in_" + tag)
    keep = CONV_WIDTH - 1
    tail = zx.reshape(b, t, SSM_IN_PAD)[:, max(t - keep, 0):, D_INNER:D_INNER + CONV_DIM]
    conv_out = tail if t >= keep else jnp.concatenate([conv_state.astype(F32), tail], axis=1)[:, -keep:]
    if t_pad != t:
        zx = jnp.pad(zx.reshape(b, t, SSM_IN_PAD), ((0, 0), (0, t_pad - t), (0, 0))).reshape(b * t_pad, SSM_IN_PAD)
    y, s_new = ssd_mixer(zx, conv_state, ssm_state, conv_prm, batch=b, t_pad=t_pad, t_valid=t, name="ssd_" + tag)
    if t_pad != t:
        y = y.reshape(b, t_pad, D_INNER)[:, :t].reshape(b * t, D_INNER)
    h_new = fused_matmul(y, w_out, res=h2d, tm=tm, tn=512, name="ssm_out_" + tag)
    return h_new, conv_out, s_new


def kernel(x_prompt, x_sample, state_ssm, state_conv, cache_cmp_kv, cache_slc_kv, state_win_kv, page_table, ssm_norm, ssm_w_in, ssm_conv_w, ssm_conv_b, ssm_dt_bias, ssm_a_log, ssm_d, ssm_gate_norm, ssm_w_out, att_norm, att_w_in, att_cmp_pos, att_cmp_w1, att_cmp_w2, att_w_out, mlp_norm, mlp_w_up, mlp_w_down, final_norm):
    bp, tp, _ = x_prompt.shape
    bs, ts, _ = x_sample.shape
    depth = mlp_norm.shape[0]
    slopes = alibi_slopes()
    hp = x_prompt.reshape(bp * tp, D_MODEL)
    hs = x_sample.reshape(bs * ts, D_MODEL)
    ssm_p, conv_p, ssm_s, conv_s = [], [], [], []
    cmp_p, slc_p, win_p, cmp_s, slc_s, win_s = [], [], [], [], [], []
    for i in range(depth):
        j = i // 2
        if i % 2 == 0:
            w_in = jnp.pad(ssm_w_in[j], ((0, 0), (0, SSM_IN_PAD - ssm_w_in.shape[-1]))).astype(BF16)
            w_out = ssm_w_out[j].astype(BF16)
            prm = (ssm_conv_w[j], ssm_conv_b[j], ssm_dt_bias[j], ssm_a_log[j], ssm_d[j], ssm_gate_norm[j])
            zc = jnp.zeros((bp, CONV_WIDTH - 1, CONV_DIM), F32)
            zs = jnp.zeros((bp, SSM_HEADS, SSM_HEAD_DIM, SSM_STATE), F32)
            hp, cp, sp = ssm_layer(hp, bp, tp, zc, zs, ssm_norm[j], w_in, prm, w_out, "p")
            hs, cs, ss = ssm_layer(hs, bs, ts, state_conv[j], state_ssm[j], ssm_norm[j], w_in, prm, w_out, "s")
            conv_p.append(cp); ssm_p.append(sp); conv_s.append(cs); ssm_s.append(ss)
        else:
            w_in = jnp.pad(att_w_in[j], ((0, 0), (0, ATT_IN_PAD - att_w_in.shape[-1]))).astype(BF16)
            w_out = att_w_out[j].astype(BF16)
            prm = (att_norm[j], w_in, att_cmp_pos[j], att_cmp_w1[j], att_cmp_w2[j], w_out, slopes)
            hp, kp, lp, wp = nsa_prompt_layer(hp, bp, tp, *prm)
            hs, kq, lq, wq = nsa_sample_layer(hs, bs, ts, cache_cmp_kv[j], cache_slc_kv[j], state_win_kv[j],
                                              page_table, *prm)
            cmp_p.append(kp); slc_p.append(lp); win_p.append(wp)
            cmp_s.append(kq); slc_s.append(lq); win_s.append(wq)
        fin = final_norm if i == depth - 1 else None
        w_up = mlp_w_up[i].astype(BF16)
        w_down = mlp_w_down[i].astype(BF16)
        hp = mlp_block(hp, mlp_norm[i], w_up, w_down, fin, "mlp%d_p" % i)
        hs = mlp_block(hs, mlp_norm[i], w_up, w_down, fin, "mlp%d_s" % i)
    return (hp.reshape(bp, tp, D_MODEL), hs.reshape(bs, ts, D_MODEL),
            jnp.stack(ssm_p), jnp.stack(conv_p), jnp.stack(cmp_p), jnp.stack(slc_p), jnp.stack(win_p),
            jnp.stack(ssm_s), jnp.stack(conv_s), jnp.stack(cmp_s), jnp.stack(slc_s), jnp.stack(win_s))
```

```python
import functools

import numpy as np
import jax
import jax.numpy as jnp
from jax import lax
from jax.experimental import pallas as pl
from jax.experimental.pallas import tpu as pltpu

F32 = jnp.float32
BF16 = jnp.bfloat16

D_MODEL = 1024
D_INNER = 2048
SSM_HEADS = 32
SSM_HEAD_DIM = 64
SSM_GROUPS = 8
SSM_STATE = 128
CONV_WIDTH = 4
CONV_DIM = 4096
SSM_IN_PAD = 6400
SSD_CHUNK = 128
ATT_HEADS = 16
HEAD_DIM = 64
KV_HEADS = 4
GQA = ATT_HEADS // KV_HEADS
KV_DIM = KV_HEADS * HEAD_DIM
CMP_STRIDE = 16
CMP_HIDDEN = 128
SLC_BLOCK = 64
SLC_SHIFT = 6
SLC_TOP_N = 16
WINDOW = 512
N_BRANCH = 3
ATT_IN_PAD = 2688
PAGE_SIZE = 128
NORM_EPS = 1e-5
NEG_INF = -1e30
LANES = 128
VMEM_LIMIT = 56 * 1024 * 1024


def _cparams(sem):
    return pltpu.CompilerParams(dimension_semantics=sem, vmem_limit_bytes=VMEM_LIMIT)


def _dot(a, b):
    return jnp.dot(a, b, preferred_element_type=F32)


def _dot_nt(a, b):
    return lax.dot_general(a, b, (((1,), (1,)), ((), ())), preferred_element_type=F32)


def _split_dot(a, onehot, left=False, terms=3):
    acc = None
    rem = a
    for _ in range(terms):
        hi = rem.astype(BF16)
        part = _dot(onehot, hi) if left else _dot(hi, onehot)
        acc = part if acc is None else acc + part
        rem = rem - hi.astype(F32)
    return acc


def _silu(x):
    return x * (0.5 * jnp.tanh(0.5 * x) + 0.5)


def _mm_kernel(*refs, has_norm, act, has_res, has_fnorm):
    it = iter(refs)
    x_ref = next(it)
    g_ref = next(it) if has_norm else None
    w_ref = next(it)
    res_ref = next(it) if has_res else None
    fg_ref = next(it) if has_fnorm else None
    o_ref = next(it)
    xn_ref = next(it) if has_norm else None
    if has_norm:
        @pl.when(pl.program_id(1) == 0)
        def _():
            x = x_ref[...].astype(F32)
            ms = jnp.mean(x * x, axis=-1, keepdims=True)
            xn_ref[...] = (x * lax.rsqrt(ms + NORM_EPS) * g_ref[...]).astype(BF16)
        a = xn_ref[...]
    else:
        a = x_ref[...].astype(BF16)
    y = _dot(a, w_ref[...])
    if act == "relu2":
        y = jnp.square(jnp.maximum(y, 0.0))
    if has_res:
        y = y + res_ref[...]
    if has_fnorm:
        ms = jnp.mean(y * y, axis=-1, keepdims=True)
        y = y * lax.rsqrt(ms + NORM_EPS) * fg_ref[...]
    o_ref[...] = y.astype(o_ref.dtype)


def fused_matmul(x, w, *, gain=None, res=None, fgain=None, act=None, out_dtype=F32, tm, tn, name):
    m, k = x.shape
    n = w.shape[1]
    assert m % tm == 0 and n % tn == 0 and w.shape[0] == k
    assert fgain is None or tn == n
    in_specs = [pl.BlockSpec((tm, k), lambda i, j: (i, 0))]
    args = [x]
    if gain is not None:
        in_specs.append(pl.BlockSpec((1, k), lambda i, j: (0, 0)))
        args.append(gain.reshape(1, k).astype(F32))
    in_specs.append(pl.BlockSpec((k, tn), lambda i, j: (0, j)))
    args.append(w)
    if res is not None:
        in_specs.append(pl.BlockSpec((tm, tn), lambda i, j: (i, j)))
        args.append(res)
    if fgain is not None:
        in_specs.append(pl.BlockSpec((1, n), lambda i, j: (0, 0)))
        args.append(fgain.reshape(1, n).astype(F32))
    scratch = [pltpu.VMEM((tm, k), BF16)] if gain is not None else []
    kern = functools.partial(_mm_kernel, has_norm=gain is not None, act=act, has_res=res is not None,
                             has_fnorm=fgain is not None)
    return pl.pallas_call(
        kern,
        grid=(m // tm, n // tn),
        in_specs=in_specs,
        out_specs=pl.BlockSpec((tm, tn), lambda i, j: (i, j)),
        out_shape=jax.ShapeDtypeStruct((m, n), out_dtype),
        scratch_shapes=scratch,
        compiler_params=_cparams(("parallel", "arbitrary")),
        name=name,
    )(*args)


def _row_tile(m, pref):
    for t in (pref, 1024, 512, 256, 128, 64, 32, 16, 8):
        if t <= pref and m % t == 0:
            return t
    raise ValueError(m)


def mlp_block(h, norm_g, w_up, w_down, final_g, name):
    tm = _row_tile(h.shape[0], 1024)
    a = fused_matmul(h, w_up, gain=norm_g, act="relu2", out_dtype=BF16, tm=tm, tn=1024, name=name + "_up")
    if final_g is None:
        return fused_matmul(a, w_down, res=h, tm=tm, tn=512, name=name + "_down")
    tm = _row_tile(h.shape[0], 512)
    return fused_matmul(a, w_down, res=h, fgain=final_g, tm=tm, tn=D_MODEL, name=name + "_down")


def _ssd_kernel(z_ref, x_ref, bc_ref, dt_ref, cw_ref, cb_ref, dtb_ref, alog_ref, dsk_ref, gn_ref,
                c0_ref, s0_ref, tri_ref, ex_ref,
                y_ref, sout_ref,
                ext_ref, st_ref, xs_ref, b_ref, c_ref, *, t_valid, n_chunks):
    c = pl.program_id(1)
    L = SSD_CHUNK

    @pl.when(c == 0)
    def _():
        ext_ref[0:8, :] = c0_ref[0]
        st_ref[...] = s0_ref[0].T

    @pl.when(c > 0)
    def _():
        ext_ref[0:8, :] = ext_ref[L:L + 8, :]

    ext_ref[8:L + 8, 0:D_INNER] = x_ref[...]
    ext_ref[8:L + 8, D_INNER:CONV_DIM] = bc_ref[...]

    cblk = 512
    for i in range(CONV_DIM // cblk):
        sl = slice(i * cblk, (i + 1) * cblk)
        acc = cb_ref[:, sl]
        for k in range(CONV_WIDTH):
            acc = acc + cw_ref[k:k + 1, sl] * ext_ref[5 + k:5 + k + L, sl]
        v = _silu(acc)
        if i < 4:
            xs_ref[:, sl] = v
        elif i < 6:
            b_ref[:, (i - 4) * cblk:(i - 3) * cblk] = v
        else:
            c_ref[:, (i - 6) * cblk:(i - 5) * cblk] = v

    dt_raw = dt_ref[...] + dtb_ref[...]
    dt = jnp.maximum(dt_raw, 0.0) + jnp.log1p(jnp.exp(-jnp.abs(dt_raw)))
    if t_valid % L:
        row = c * L + lax.broadcasted_iota(jnp.int32, (L, LANES), 0)
        dt = jnp.where(row < t_valid, dt, 0.0)
    da = dt * (-jnp.exp(alog_ref[...]))
    acs = _split_dot(da, tri_ref[...], left=True)
    acs_t = acs.T
    expd = _split_dot(jnp.concatenate([acs, dt], axis=0), ex_ref[...])
    a_exp = expd[0:L]
    dt_exp = expd[L:2 * L]

    li = lax.broadcasted_iota(jnp.int32, (L, L), 0)
    si = lax.broadcasted_iota(jnp.int32, (L, L), 1)
    causal = si <= li
    first_half = si < SSM_HEAD_DIM
    gw = D_INNER // SSM_GROUPS
    for g in range(SSM_GROUPS):
        gs = slice(g * gw, (g + 1) * gw)
        bg = b_ref[:, g * SSM_STATE:(g + 1) * SSM_STATE]
        cg = c_ref[:, g * SSM_STATE:(g + 1) * SSM_STATE].astype(BF16)
        cbm = _dot_nt(cg, bg.astype(BF16))
        ag = a_exp[:, gs]
        xg = xs_ref[:, gs]
        xdt = xg * dt_exp[:, gs]
        a_last = ag[L - 1:L, :]
        y_off = _dot(cg, st_ref[:, gs].astype(BF16)) * jnp.exp(ag)
        ys = []
        for p in range(2):
            h0 = 4 * g + 2 * p
            xp = xdt[:, p * LANES:(p + 1) * LANES].astype(BF16)
            outs = []
            for h in (h0, h0 + 1):
                seg = acs[:, h:h + 1] - acs_t[h:h + 1, :]
                lm = jnp.exp(jnp.where(causal, seg, NEG_INF))
                outs.append(_dot((cbm * lm).astype(BF16), xp))
            ys.append(jnp.where(first_half, outs[0], outs[1]))
        y = jnp.concatenate(ys, axis=1) + y_off
        xd = (xdt * jnp.exp(a_last - ag)).astype(BF16)
        st_ref[:, gs] = st_ref[:, gs] * jnp.exp(a_last) + _dot(bg.T.astype(BF16), xd)
        zg = z_ref[:, gs]
        yy = (y + dsk_ref[:, gs] * xg) * _silu(zg)
        ms = jnp.mean(yy * yy, axis=-1, keepdims=True)
        y_ref[:, gs] = (yy * lax.rsqrt(ms + NORM_EPS) * gn_ref[:, gs]).astype(y_ref.dtype)

    @pl.when(c == n_chunks - 1)
    def _():
        sout_ref[0] = st_ref[...].T


def ssd_mixer(zx, conv0, st0, prm, *, batch, t_pad, t_valid, name):
    n_chunks = t_pad // SSD_CHUNK
    L = SSD_CHUNK
    conv_w, conv_b, dt_bias, a_log, d_skip, gate_norm = prm
    lane_pad = lambda v: jnp.pad(v.astype(F32), (0, LANES - SSM_HEADS)).reshape(1, LANES)
    tri = jnp.asarray(np.tril(np.ones((L, L), np.float32)), BF16)
    ex_np = np.zeros((LANES, D_INNER), np.float32)
    ex_np[np.arange(D_INNER) // SSM_HEAD_DIM, np.arange(D_INNER)] = 1.0
    ex = jnp.asarray(ex_np, BF16)
    c0 = jnp.pad(conv0.astype(F32), ((0, 0), (8 - (CONV_WIDTH - 1), 0), (0, 0)))
    row = lambda b, c: (b * n_chunks + c)
    const = lambda shape: pl.BlockSpec(shape, lambda b, c: (0,) * len(shape))
    kern = functools.partial(_ssd_kernel, t_valid=t_valid, n_chunks=n_chunks)
    y, s_out = pl.pallas_call(
        kern,
        grid=(batch, n_chunks),
        in_specs=[
            pl.BlockSpec((L, D_INNER), lambda b, c: (row(b, c), 0)),
            pl.BlockSpec((L, D_INNER), lambda b, c: (row(b, c), 1)),
            pl.BlockSpec((L, D_INNER), lambda b, c: (row(b, c), 2)),
            pl.BlockSpec((L, LANES), lambda b, c: (row(b, c), (D_INNER + CONV_DIM) // LANES)),
            const((CONV_WIDTH, CONV_DIM)), const((1, CONV_DIM)), const((1, LANES)), const((1, LANES)),
            const((1, D_INNER)), const((1, D_INNER)),
            pl.BlockSpec((1, 8, CONV_DIM), lambda b, c: (b, 0, 0)),
            pl.BlockSpec((1, D_INNER, SSM_STATE), lambda b, c: (b, 0, 0)),
            const((L, L)), const((LANES, D_INNER)),
        ],
        out_specs=[
            pl.BlockSpec((L, D_INNER), lambda b, c: (row(b, c), 0)),
            pl.BlockSpec((1, D_INNER, SSM_STATE), lambda b, c: (b, 0, 0)),
        ],
        out_shape=[
            jax.ShapeDtypeStruct((batch * t_pad, D_INNER), BF16),
            jax.ShapeDtypeStruct((batch, D_INNER, SSM_STATE), F32),
        ],
        scratch_shapes=[
            pltpu.VMEM((L + 8, CONV_DIM), F32),
            pltpu.VMEM((SSM_STATE, D_INNER), F32),
            pltpu.VMEM((L, D_INNER), F32),
            pltpu.VMEM((L, SSM_GROUPS * SSM_STATE), F32),
            pltpu.VMEM((L, SSM_GROUPS * SSM_STATE), F32),
        ],
        compiler_params=_cparams(("parallel", "arbitrary")),
        name=name,
    )(zx, zx, zx, zx,
      conv_w.astype(F32).T, conv_b.astype(F32).reshape(1, CONV_DIM), lane_pad(dt_bias), lane_pad(a_log),
      jnp.repeat(d_skip.astype(F32), SSM_HEAD_DIM).reshape(1, D_INNER), gate_norm.astype(F32).reshape(1, D_INNER),
      c0, st0.astype(F32).reshape(batch, D_INNER, SSM_STATE), tri, ex)
    return y, s_out.reshape(batch, SSM_HEADS, SSM_HEAD_DIM, SSM_STATE)


CMP_PAGES = 16


def _cmp_u_kernel(pt_ref, *refs, feature_major):
    del pt_ref
    pages = refs[:CMP_PAGES]
    perm_ref, w_ref, uf_ref, us_ref, y_ref = refs[CMP_PAGES:]
    regroup = _dot_nt if feature_major else _dot
    for p, page in enumerate(pages):
        yp = regroup(perm_ref[...], page[...].astype(BF16))
        for j in range(CMP_STRIDE):
            y_ref[j, p * 8:(p + 1) * 8, :] = yp[j * 8:(j + 1) * 8, :]
    rows = y_ref.shape[1]
    low = lax.broadcasted_iota(jnp.int32, (rows, LANES), 1) < HEAD_DIM
    accs = [None] * (2 * KV_HEADS)
    for a in range(CMP_STRIDE // 4):
        for q in range(4):
            blocks = [y_ref[4 * a + jj, :, q * LANES:(q + 1) * LANES] for jj in range(4)]
            swapped = [pltpu.roll(x, HEAD_DIM, axis=1) for x in blocks]
            for gl in range(2):
                if gl == 0:
                    pairs = [jnp.where(low, blocks[0], swapped[1]), jnp.where(low, blocks[2], swapped[3])]
                else:
                    pairs = [jnp.where(low, swapped[0], blocks[1]), jnp.where(low, swapped[2], blocks[3])]
                part = _dot(jnp.concatenate(pairs, axis=1).astype(BF16), w_ref[q // 2, a])
                k = 2 * q + gl
                accs[k] = part if accs[k] is None else accs[k] + part
    for k in range(2 * KV_HEADS):
        uf_ref[0, k] = accs[k][:, 0:CMP_HIDDEN]
        us_ref[0, k] = accs[k][:, CMP_HIDDEN:2 * CMP_HIDDEN]


def cmp_project(src2d, page_ids, col_block, w1, name, feature_major=False):
    batch, n_pages = page_ids.shape
    assert n_pages % CMP_PAGES == 0
    n_sub = n_pages * (PAGE_SIZE // CMP_STRIDE)
    rows = CMP_PAGES * (PAGE_SIZE // CMP_STRIDE)
    w1r = w1.astype(F32).reshape(2, 2 * CMP_STRIDE, HEAD_DIM, CMP_HIDDEN)
    wj = jnp.concatenate([w1r[:, :CMP_STRIDE], w1r[:, CMP_STRIDE:]], axis=-1)
    wpad = wj.reshape(2, CMP_STRIDE // 4, 4 * HEAD_DIM, 2 * CMP_HIDDEN).astype(BF16)
    perm = np.zeros((PAGE_SIZE, PAGE_SIZE), np.float32)
    for j in range(CMP_STRIDE):
        for s in range(PAGE_SIZE // CMP_STRIDE):
            perm[j * 8 + s, CMP_STRIDE * s + j] = 1.0
    perm = jnp.asarray(perm, BF16)

    def page_spec(p):
        shape = (4 * LANES, PAGE_SIZE) if feature_major else (PAGE_SIZE, 4 * LANES)
        return pl.BlockSpec(shape, lambda b, c, pt: (pt[b, c * CMP_PAGES + p], col_block))

    u_shape = jax.ShapeDtypeStruct((batch, 2 * KV_HEADS, n_sub, CMP_HIDDEN), F32)
    u_spec = pl.BlockSpec((1, 2 * KV_HEADS, rows, CMP_HIDDEN), lambda b, c, pt: (b, 0, c, 0))
    return pl.pallas_call(
        functools.partial(_cmp_u_kernel, feature_major=feature_major),
        grid_spec=pltpu.PrefetchScalarGridSpec(
            num_scalar_prefetch=1,
            grid=(batch, n_pages // CMP_PAGES),
            in_specs=[page_spec(p) for p in range(CMP_PAGES)] + [
                pl.BlockSpec((PAGE_SIZE, PAGE_SIZE), lambda b, c, pt: (0, 0)),
                pl.BlockSpec((2, CMP_STRIDE // 4, 4 * HEAD_DIM, 2 * CMP_HIDDEN), lambda b, c, pt: (0, 0, 0, 0)),
            ],
            out_specs=[u_spec, u_spec],
            scratch_shapes=[pltpu.VMEM((CMP_STRIDE, rows, 4 * LANES), F32)],
        ),
        out_shape=[u_shape, u_shape],
        compiler_params=_cparams(("parallel", "arbitrary")),
        name=name,
    )(page_ids, *([src2d] * CMP_PAGES), perm, wpad)


def _cmp_out_kernel(uf_ref, us_ref, pos_ref, w1_ref, w2_ref, o_ref, sh_ref, *, n_sub):
    pos_term = _dot(pos_ref[0].astype(BF16), w1_ref[0])[0:1, :]
    rows = lax.broadcasted_iota(jnp.int32, (n_sub, HEAD_DIM), 0)
    for g in range(KV_HEADS):
        sh_ref[g, 0:n_sub, :] = us_ref[0, g]
        sh_ref[g, n_sub:n_sub + 8, :] = jnp.zeros((8, CMP_HIDDEN), F32)
        hid = jax.nn.gelu(uf_ref[0, g] + sh_ref[g, pl.ds(1, n_sub), :] + pos_term)
        out = _dot(hid.astype(BF16), w2_ref[0])
        o_ref[0, g] = jnp.where(rows < n_sub - 1, out, 0.0).astype(o_ref.dtype)


def cmp_finish(uf, us, pos_emb, w1, w2, name):
    batch, nkvg, n_sub, _ = uf.shape
    flat = 2 * CMP_STRIDE * HEAD_DIM
    pos = jnp.pad(pos_emb.astype(F32).reshape(2, 1, flat), ((0, 0), (0, 7), (0, 0)))
    assert nkvg == 2 * KV_HEADS
    u_spec = pl.BlockSpec((1, KV_HEADS, n_sub, CMP_HIDDEN), lambda b, kv: (b, kv, 0, 0))
    return pl.pallas_call(
        functools.partial(_cmp_out_kernel, n_sub=n_sub),
        grid=(batch, 2),
        in_specs=[u_spec, u_spec,
                  pl.BlockSpec((1, 8, flat), lambda b, kv: (kv, 0, 0)),
                  pl.BlockSpec((1, flat, CMP_HIDDEN), lambda b, kv: (kv, 0, 0)),
                  pl.BlockSpec((1, CMP_HIDDEN, HEAD_DIM), lambda b, kv: (kv, 0, 0))],
        out_specs=pl.BlockSpec((1, KV_HEADS, n_sub, HEAD_DIM), lambda b, kv: (b, kv, 0, 0)),
        out_shape=jax.ShapeDtypeStruct((batch, nkvg, n_sub, HEAD_DIM), BF16),
        scratch_shapes=[pltpu.VMEM((KV_HEADS, n_sub + 8, CMP_HIDDEN), F32)],
        compiler_params=_cparams(("parallel", "parallel")),
        name=name,
    )(uf, us, pos, w1.astype(BF16), w2.astype(BF16))


def _cmp_attn_kernel(q_ref, kc_ref, vc_ref, wsel_ref, o_ref, sel_ref, *rest, tq, ncp, nblk_pad, n_sel, q_base,
                     want_idx):
    qi = pl.program_id(2)
    use_ref = None if want_idx else rest[0]
    tpos = q_base + qi * tq + lax.broadcasted_iota(jnp.int32, (tq, 1), 0)
    c_end = CMP_STRIDE * lax.broadcasted_iota(jnp.int32, (1, ncp), 1) + (2 * CMP_STRIDE - 1)
    ok = c_end <= tpos
    any_ok = (tpos >= 2 * CMP_STRIDE - 1).astype(F32)
    kc = kc_ref[0, 0]
    vw = jnp.concatenate([vc_ref[0, 0], wsel_ref[...]], axis=1)
    p_slc = None
    if tq < LANES:
        s3 = _dot_nt(q_ref[0].reshape(GQA * tq, LANES), kc).reshape(GQA, tq, ncp)
        s3 = jnp.where(ok[None], s3, NEG_INF)
        e3 = jnp.exp2(s3 - jnp.max(s3, axis=-1, keepdims=True))
        r3 = _dot(e3.reshape(GQA * tq, ncp).astype(BF16), vw).reshape(GQA, tq, LANES + nblk_pad)
        inv3 = any_ok[None] / r3[:, :, HEAD_DIM:HEAD_DIM + 1]
        o_ref[0] = r3[:, :, 0:HEAD_DIM] * inv3
        p_slc = jnp.sum(r3[:, :, LANES:] * inv3, axis=0)
    for h in range(GQA if tq >= LANES else 0):
        s = jnp.where(ok, _dot_nt(q_ref[0, h], kc), NEG_INF)
        e = jnp.exp2(s - jnp.max(s, axis=-1, keepdims=True))
        r = _dot(e.astype(BF16), vw)
        inv = any_ok / r[:, HEAD_DIM:HEAD_DIM + 1]
        o_ref[0, h] = r[:, 0:HEAD_DIM] * inv
        part = r[:, LANES:] * inv
        p_slc = part if p_slc is None else p_slc + part
    tq_l = max(tq, LANES)
    if tq < LANES:
        p_slc = jnp.concatenate([p_slc, jnp.zeros((tq_l - tq, nblk_pad), F32)], axis=0)
    sc = p_slc.T
    blk = lax.broadcasted_iota(jnp.int32, (nblk_pad, tq_l), 0)
    tp = q_base + qi * tq + lax.broadcasted_iota(jnp.int32, (nblk_pad, tq_l), 1)
    cur = jnp.right_shift(tp, SLC_SHIFT)
    forced = (blk == 0) | (blk == cur) | (blk == cur - 1)
    reach = blk * SLC_BLOCK <= tp
    sc = jnp.where(forced, 1e30, jnp.where(reach, sc, -1.0))
    sc = jnp.where(blk < n_sel, sc, -3.0)
    removed = -2.0
    idx_rows = []
    for _ in range(min(SLC_TOP_N, n_sel)):
        mx = jnp.max(sc, axis=0, keepdims=True)
        idx = jnp.min(jnp.where(sc == mx, blk, nblk_pad), axis=0, keepdims=True)
        sc = jnp.where(blk == idx, removed, sc)
        idx_rows.append(idx)
    if want_idx:
        sel_ref[0, 0] = jnp.concatenate(idx_rows, axis=0)
    else:
        chosen_t = (sc == removed).astype(F32).T[0:tq]
        sel_ref[0, 0] = ((chosen_t - 1.0) * 1e30).astype(sel_ref.dtype)
        used = jnp.max(chosen_t, axis=0, keepdims=True)
        use_ref[0, 0, 0] = jnp.broadcast_to(used, (8, nblk_pad))


LOG2E = 1.4426950408889634


def _query_feats(q_h, slopes):
    sl = slopes * LOG2E
    s1 = sl.astype(BF16)
    r1 = sl - s1.astype(F32)
    s2 = r1.astype(BF16)
    s3 = (r1 - s2.astype(F32)).astype(BF16)
    feats = jnp.stack([s1, s2, s3, s1, s2, s3, jnp.ones_like(s1)], axis=-1)
    return _with_feats((q_h * (LOG2E * HEAD_DIM ** -0.5)).astype(BF16), feats[None, :, None, :], LANES)


def _pos_feats(pos, invalid=None):
    lo = jnp.bitwise_and(pos, 255)
    hi = pos - lo
    inv = jnp.zeros_like(pos, F32) if invalid is None else jnp.where(invalid, NEG_INF, 0.0)
    return jnp.concatenate([jnp.stack([lo, lo, lo, hi, hi, hi], axis=-1).astype(BF16), inv.astype(BF16)[..., None]],
                           axis=-1)


def _with_feats(x, feats, width):
    lead = x.shape[:-1]
    f = jnp.broadcast_to(feats, lead + feats.shape[-1:]).astype(BF16)
    pad = jnp.zeros(lead + (width - x.shape[-1] - f.shape[-1],), BF16)
    return jnp.concatenate([x.astype(BF16), f, pad], axis=-1)


def _with_ones(v):
    return _with_feats(v, jnp.ones((1,), BF16), LANES)


def cmp_attention(qa, kvc, *, tq, n_sel, q_base, want_idx, name):
    b, _, t, _ = qa.shape
    ncp = kvc.shape[2]
    nblk_pad = -(-n_sel // LANES) * LANES
    c_end = CMP_STRIDE * jnp.arange(ncp, dtype=jnp.int32) + (2 * CMP_STRIDE - 1)
    kca = _with_feats(kvc[:, :KV_HEADS], _pos_feats(c_end), LANES)
    vca = _with_ones(kvc[:, KV_HEADS:])
    c = np.arange(ncp)[:, None]
    j = np.arange(nblk_pad)[None, :]
    wsel = ((c // 4 == j).astype(np.float32) + ((c + 1) // 4 == j).astype(np.float32)) * (c < ncp - 1)
    tq_l = max(tq, LANES)
    if want_idx:
        sel_shape = jax.ShapeDtypeStruct((b, KV_HEADS, SLC_TOP_N, (t // tq) * tq_l), jnp.int32)
        sel_spec = pl.BlockSpec((1, 1, SLC_TOP_N, tq_l), lambda bi, g, qi: (bi, g, 0, qi))
    else:
        sel_shape = jax.ShapeDtypeStruct((b, KV_HEADS, t, nblk_pad), BF16)
        sel_spec = pl.BlockSpec((1, 1, tq, nblk_pad), lambda bi, g, qi: (bi, g, qi, 0))
    kern = functools.partial(_cmp_attn_kernel, tq=tq, ncp=ncp, nblk_pad=nblk_pad, n_sel=n_sel, q_base=q_base,
                             want_idx=want_idx)
    out_specs = [pl.BlockSpec((1, GQA, tq, HEAD_DIM), lambda bi, g, qi: (bi, g, qi, 0)), sel_spec]
    out_shape = [jax.ShapeDtypeStruct((b, ATT_HEADS, t, HEAD_DIM), F32), sel_shape]
    if not want_idx:
        out_specs.append(pl.BlockSpec((1, 1, 1, 8, nblk_pad), lambda bi, g, qi: (bi, g, qi, 0, 0)))
        out_shape.append(jax.ShapeDtypeStruct((b, KV_HEADS, t // tq, 8, nblk_pad), F32))
    return pl.pallas_call(
        kern,
        grid=(b, KV_HEADS, t // tq),
        in_specs=[
            pl.BlockSpec((1, GQA, tq, LANES), lambda bi, g, qi: (bi, g, qi, 0)),
            pl.BlockSpec((1, 1, ncp, LANES), lambda bi, g, qi: (bi, g, 0, 0)),
            pl.BlockSpec((1, 1, ncp, LANES), lambda bi, g, qi: (bi, g, 0, 0)),
            pl.BlockSpec((ncp, nblk_pad), lambda bi, g, qi: (0, 0)),
        ],
        out_specs=out_specs,
        out_shape=out_shape,
        compiler_params=_cparams(("parallel", "parallel", "parallel")),
        name=name,
    )(qa, kca, vca, jnp.asarray(wsel, BF16))


SEL_FEAT = 256


def _sel_kernel(use_ref, qa_ref, bias_ref, ka_ref, v_ref, o_ref, m_ref, acc_ref, *, tq, tk):
    bi = pl.program_id(0)
    g = pl.program_id(1)
    qi = pl.program_id(2)
    n_qt = pl.num_programs(2)
    per = tq // tk
    n_kc = n_qt * per
    m_ref[...] = jnp.full(m_ref.shape, NEG_INF, F32)
    acc_ref[...] = jnp.zeros(acc_ref.shape, F32)
    rows = lax.broadcasted_iota(jnp.int32, (tq, tk), 0)
    cols = lax.broadcasted_iota(jnp.int32, (tq, tk), 1)

    def chunk(j, diag_offset):
        k0 = pl.multiple_of(j * tk, tk)
        ka = ka_ref[0, 0, pl.ds(k0, tk), :]
        va = v_ref[0, 0, pl.ds(k0, tk), :]
        bias = bias_ref[0, 0]
        for h in range(GQA):
            qf = jnp.concatenate([qa_ref[0, h], bias], axis=1)
            s = _dot_nt(qf, ka)
            if diag_offset is not None:
                s = jnp.where(cols + diag_offset <= rows, s, NEG_INF)
            m_prev = m_ref[h]
            m_new = jnp.maximum(m_prev, jnp.max(s, axis=-1, keepdims=True))
            p = jnp.exp2(s - jnp.concatenate([m_new] * (tk // LANES), axis=1))
            acc_ref[h] = jnp.exp2(m_prev - m_new) * acc_ref[h] + _dot(p.astype(BF16), va)
            m_ref[h] = m_new

    def body(j, carry):
        @pl.when(use_ref[((bi * KV_HEADS + g) * n_qt + qi) * n_kc + j] > 0)
        def _():
            chunk(j, None)
        return carry

    lax.fori_loop(0, qi * per, body, 0)
    for c in range(per):
        chunk(qi * per + c, c * tk)
    for h in range(GQA):
        acc = acc_ref[h]
        o_ref[0, h] = acc[:, 0:HEAD_DIM] / acc[:, HEAD_DIM:HEAD_DIM + 1]


def sel_attention_dense(chunk_used, qa, bias, ka, va, *, tq, tk, name):
    b, _, t, _ = qa.shape
    return pl.pallas_call(
        functools.partial(_sel_kernel, tq=tq, tk=tk),
        grid_spec=pltpu.PrefetchScalarGridSpec(
            num_scalar_prefetch=1,
            grid=(b, KV_HEADS, t // tq),
            in_specs=[
                pl.BlockSpec((1, GQA, tq, LANES), lambda bi, g, qi, *_: (bi, g, qi, 0)),
                pl.BlockSpec((1, 1, tq, LANES), lambda bi, g, qi, *_: (bi, g, qi, 0)),
                pl.BlockSpec((1, 1, t, SEL_FEAT), lambda bi, g, qi, *_: (bi, g, 0, 0)),
                pl.BlockSpec((1, 1, t, LANES), lambda bi, g, qi, *_: (bi, g, 0, 0)),
            ],
            out_specs=pl.BlockSpec((1, GQA, tq, HEAD_DIM), lambda bi, g, qi, *_: (bi, g, qi, 0)),
            scratch_shapes=[pltpu.VMEM((GQA, tq, LANES), F32), pltpu.VMEM((GQA, tq, LANES), F32)],
        ),
        out_shape=jax.ShapeDtypeStruct((b, ATT_HEADS, t, HEAD_DIM), F32),
        compiler_params=_cparams(("parallel", "parallel", "arbitrary")),
        name=name,
    )(chunk_used.reshape(-1), qa, bias, ka, va)


def _sel_gather_kernel(idx_ref, pt_ref, q_ref, slope_ref, cache_ref, new_ref, o_ref, kbuf, vbuf, sem, *,
                       n_tok, tq, past_len, n_steps):
    b = pl.program_id(0)
    g = pl.program_id(1)
    step = b * KV_HEADS + g
    slot = step % 2
    new_blk = past_len // SLC_BLOCK
    halves = PAGE_SIZE // SLC_BLOCK

    def block_copy(bb, gg, sl, t, n, kv, blk, newest):
        buf = vbuf if kv else kbuf
        dst = buf.at[sl, t, :, pl.ds(n * PAGE_SIZE, PAGE_SIZE)]
        if newest:
            src = new_ref.at[bb, kv, gg]
        else:
            page = pt_ref[bb * (past_len // PAGE_SIZE) + jnp.minimum(blk, new_blk - 1) // halves]
            src = cache_ref.at[page, kv, gg]
        return pltpu.make_async_copy(src, dst, sem.at[sl])

    def for_each_block(bb, gg, fn):
        for t in range(n_tok):
            for n in range(SLC_TOP_N):
                blk = idx_ref[((bb * KV_HEADS + gg) * SLC_TOP_N + n) * n_tok + t]
                for kv in range(2):
                    fn(t, n, kv, blk)

    def start(bb, gg, sl):
        def one(t, n, kv, blk):
            @pl.when(blk >= new_blk)
            def _():
                block_copy(bb, gg, sl, t, n, kv, blk, True).start(priority=kv)

            @pl.when(blk < new_blk)
            def _():
                block_copy(bb, gg, sl, t, n, kv, blk, False).start(priority=kv)
        for_each_block(bb, gg, one)

    @pl.when(step == 0)
    def _():
        start(b, g, slot)

    @pl.when(step + 1 < n_steps)
    def _():
        nxt = step + 1
        start(nxt // KV_HEADS, nxt % KV_HEADS, 1 - slot)

    for_each_block(b, g, lambda t, n, kv, blk: block_copy(b, g, slot, t, n, kv, blk, False).wait())

    r = GQA * tq
    q = (q_ref[0].reshape(r, HEAD_DIM) * (HEAD_DIM ** -0.5)).astype(BF16)
    nk = SLC_TOP_N * PAGE_SIZE
    lane = lax.broadcasted_iota(jnp.int32, (1, nk), 1)
    trow = lax.broadcasted_iota(jnp.int32, (tq, 1), 0)
    tpos = past_len + trow
    acc = jnp.zeros((GQA, tq, HEAD_DIM), F32)
    in_page = jnp.bitwise_and(lane, PAGE_SIZE - 1)
    half_of_lane = jnp.right_shift(in_page, SLC_SHIFT)
    slot_of_lane = jnp.right_shift(lane, SLC_SHIFT + 1)
    for t in range(n_tok):
        kpos = in_page
        chosen_half = jnp.zeros((1, nk), jnp.int32)
        for n in range(SLC_TOP_N):
            blk = idx_ref[((b * KV_HEADS + g) * SLC_TOP_N + n) * n_tok + t]
            here = slot_of_lane == n
            kpos = jnp.where(here, kpos + (blk // halves) * PAGE_SIZE, kpos)
            chosen_half = jnp.where(here, blk % halves, chosen_half)
        d = (tpos - kpos).astype(F32)
        ok = (d >= 0.0) & (half_of_lane == chosen_half)
        s = _dot(q, kbuf[slot, t].astype(BF16))
        s3 = s.reshape(GQA, tq, nk) - slope_ref[0] * d[None]
        s3 = jnp.where(ok[None], s3, NEG_INF)
        m = jnp.max(s3, axis=-1, keepdims=True)
        e = jnp.exp(s3 - m)
        p = e / jnp.sum(e, axis=-1, keepdims=True)
        o = _dot_nt(p.reshape(r, nk).astype(BF16), vbuf[slot, t].astype(BF16)).reshape(GQA, tq, HEAD_DIM)
        acc = jnp.where((trow == t)[None], o, acc)
    o_ref[0] = acc


def sel_attention_gather(q_h, idx, page_table, cache, new_kv, slopes, *, n_tok, past_len, name):
    b, _, tq, _ = q_h.shape
    n_steps = b * KV_HEADS
    nk = SLC_TOP_N * PAGE_SIZE
    kern = functools.partial(_sel_gather_kernel, n_tok=n_tok, tq=tq, past_len=past_len, n_steps=n_steps)
    return pl.pallas_call(
        kern,
        grid_spec=pltpu.PrefetchScalarGridSpec(
            num_scalar_prefetch=2,
            grid=(b, KV_HEADS),
            in_specs=[
                pl.BlockSpec((1, GQA, tq, HEAD_DIM), lambda bi, g, *_: (bi, g, 0, 0)),
                pl.BlockSpec((1, GQA, 1, 1), lambda bi, g, *_: (g, 0, 0, 0)),
                pl.BlockSpec(memory_space=pl.ANY),
                pl.BlockSpec(memory_space=pl.ANY),
            ],
            out_specs=pl.BlockSpec((1, GQA, tq, HEAD_DIM), lambda bi, g, *_: (bi, g, 0, 0)),
            scratch_shapes=[pltpu.VMEM((2, n_tok, HEAD_DIM, nk), F32), pltpu.VMEM((2, n_tok, HEAD_DIM, nk), F32),
                            pltpu.SemaphoreType.DMA((2,))],
        ),
        out_shape=jax.ShapeDtypeStruct((b, ATT_HEADS, tq, HEAD_DIM), F32),
        compiler_params=_cparams(("arbitrary", "arbitrary")),
        name=name,
    )(idx.reshape(-1), page_table.reshape(-1), q_h, slopes.reshape(KV_HEADS, GQA, 1, 1), cache, new_kv)


def _win_kernel(q_ref, k_ref, v_ref, oc_ref, os_ref, gl_ref, o_ref, *, tq, nk):
    qi = pl.program_id(2)
    r0 = pl.multiple_of(qi * tq, tq)
    kw = k_ref[0, 0, pl.ds(r0, nk), :]
    vw = v_ref[0, 0, pl.ds(r0, nk), :]
    blockwise = tq % LANES == 0 and WINDOW % tq == 0
    if blockwise:
        rows = lax.broadcasted_iota(jnp.int32, (tq, tq), 0)
        cols = lax.broadcasted_iota(jnp.int32, (tq, tq), 1)
        old_ok = cols > rows
        new_ok = cols <= rows
    else:
        di = (WINDOW + lax.broadcasted_iota(jnp.int32, (tq, 1), 0)) - lax.broadcasted_iota(jnp.int32, (1, nk), 1)
        ok = (di >= 0) & (di < WINDOW)
    if tq < LANES:
        s3 = _dot_nt(q_ref[0].reshape(GQA * tq, LANES), kw).reshape(GQA, tq, nk)
        s3 = jnp.where(ok[None], s3, NEG_INF)
        e3 = jnp.exp2(s3 - jnp.max(s3, axis=-1, keepdims=True))
        r3 = _dot(e3.reshape(GQA * tq, nk).astype(BF16), vw).reshape(GQA, tq, LANES)
        ow3 = r3[:, :, 0:HEAD_DIM] / r3[:, :, HEAD_DIM:HEAD_DIM + 1]
        gates3 = 1.0 / (1.0 + jnp.exp(-gl_ref[0]))
        o_ref[0] = gates3[:, :, 0:1] * oc_ref[0] + gates3[:, :, 1:2] * os_ref[0] + gates3[:, :, 2:3] * ow3
    for h in range(GQA if tq >= LANES else 0):
        s = _dot_nt(q_ref[0, h], kw)
        if blockwise:
            s = jnp.concatenate([jnp.where(old_ok, s[:, 0:tq], NEG_INF), s[:, tq:nk - tq],
                                 jnp.where(new_ok, s[:, nk - tq:nk], NEG_INF)], axis=1)
        else:
            s = jnp.where(ok, s, NEG_INF)
        e = jnp.exp2(s - jnp.max(s, axis=-1, keepdims=True))
        r = _dot(e.astype(BF16), vw)
        ow = r[:, 0:HEAD_DIM] / r[:, HEAD_DIM:HEAD_DIM + 1]
        gates = 1.0 / (1.0 + jnp.exp(-gl_ref[0, h]))
        o_ref[0, h] = gates[:, 0:1] * oc_ref[0, h] + gates[:, 1:2] * os_ref[0, h] + gates[:, 2:3] * ow


def win_attention_merge(qa, kw, vw, o_c, o_s, gate_logits, *, tq, k_valid_from, name):
    b, _, t, _ = qa.shape
    nk = tq + WINDOW
    lk = kw.shape[2]
    assert lk >= t + WINDOW
    krow = jnp.arange(lk, dtype=jnp.int32)
    kwa = _with_feats(kw, _pos_feats(krow, invalid=krow < k_valid_from), LANES)
    vwa = _with_ones(vw)
    head_spec = lambda w: pl.BlockSpec((1, GQA, tq, w), lambda bi, g, qi: (bi, g, qi, 0))
    kv_spec = pl.BlockSpec((1, 1, lk, LANES), lambda bi, g, qi: (bi, g, 0, 0))
    return pl.pallas_call(
        functools.partial(_win_kernel, tq=tq, nk=nk),
        grid=(b, KV_HEADS, t // tq),
        in_specs=[head_spec(LANES), kv_spec, kv_spec,
                  head_spec(HEAD_DIM), head_spec(HEAD_DIM), head_spec(N_BRANCH)],
        out_specs=head_spec(HEAD_DIM),
        out_shape=jax.ShapeDtypeStruct((b, ATT_HEADS, t, HEAD_DIM), F32),
        compiler_params=_cparams(("parallel", "parallel", "parallel")),
        name=name,
    )(qa, kwa, vwa, o_c, o_s, gate_logits)


def _heads(x2d, b, t, n_heads):
    return x2d.reshape(b, t, n_heads, HEAD_DIM).transpose(0, 2, 1, 3)


def _kv_stack(qkv, b, t, col):
    return qkv[:, col:col + 2 * KV_DIM].reshape(b, t, 2, KV_HEADS, HEAD_DIM)


def alibi_slopes():
    return 2.0 ** (-8.0 * jnp.arange(1, ATT_HEADS + 1, dtype=jnp.float32) / ATT_HEADS)


def nsa_prompt_layer(h2d, b, t, norm_g, w_in, cmp_pos, cmp_w1, cmp_w2, w_out, slopes):
    tm = _row_tile(h2d.shape[0], 1024)
    qkv = fused_matmul(h2d, w_in, gain=norm_g, tm=tm, tn=ATT_IN_PAD // 3, name="att_in_p")
    q_h = _heads(qkv[:, :D_MODEL], b, t, ATT_HEADS)
    gl_h = qkv[:, D_MODEL + 6 * KV_DIM:D_MODEL + 6 * KV_DIM + ATT_HEADS * N_BRANCH]
    gl_h = gl_h.reshape(b, t, ATT_HEADS, N_BRANCH).transpose(0, 2, 1, 3)

    n_pages = t // PAGE_SIZE
    page_ids = (jnp.arange(b, dtype=jnp.int32)[:, None] * n_pages + jnp.arange(n_pages, dtype=jnp.int32)[None, :])
    uf, us = cmp_project(qkv, page_ids, D_MODEL // (2 * KV_DIM), cmp_w1, "cmp_u_p")
    kvc = cmp_finish(uf, us, cmp_pos, cmp_w1, cmp_w2, "cmp_out_p")

    n_sel = t // SLC_BLOCK
    tq = 256
    tq_cmp = min(512, t)
    qa = _query_feats(q_h, slopes)
    o_c, sel_bias, blk_used = cmp_attention(qa, kvc, tq=tq_cmp, n_sel=n_sel, q_base=0, want_idx=False,
                                            name="cmp_attn_p")
    tq_sel, tk_sel = min(512, t), min(512, t)
    used = blk_used[:, :, :, 0, :n_sel].reshape(b, KV_HEADS, t // tq_sel, tq_sel // tq_cmp, t // tk_sel,
                                                tk_sel // SLC_BLOCK)
    chunk_used = (used.max(axis=(3, 5)) > 0.0).astype(jnp.int32)

    nb = sel_bias.shape[-1]
    assert LANES + nb == SEL_FEAT
    ks = _heads(qkv[:, D_MODEL + 2 * KV_DIM:D_MODEL + 3 * KV_DIM], b, t, KV_HEADS).astype(BF16)
    vs = _heads(qkv[:, D_MODEL + 3 * KV_DIM:D_MODEL + 4 * KV_DIM], b, t, KV_HEADS).astype(BF16)
    kpos = jnp.arange(t, dtype=jnp.int32)
    onehot = jax.nn.one_hot(kpos // SLC_BLOCK, nb, dtype=BF16)
    ka = jnp.concatenate([_with_feats(ks, _pos_feats(kpos), LANES),
                          jnp.broadcast_to(onehot, (b, KV_HEADS) + onehot.shape)], axis=-1)
    o_s = sel_attention_dense(chunk_used, qa, sel_bias, ka, _with_ones(vs), tq=tq_sel, tk=tk_sel, name="sel_attn_p")

    front = ((0, 0), (0, 0), (WINDOW, 0), (0, 0))
    kw = jnp.pad(_heads(qkv[:, D_MODEL + 4 * KV_DIM:D_MODEL + 5 * KV_DIM], b, t, KV_HEADS).astype(BF16), front)
    vw = jnp.pad(_heads(qkv[:, D_MODEL + 5 * KV_DIM:D_MODEL + 6 * KV_DIM], b, t, KV_HEADS).astype(BF16), front)
    o = win_attention_merge(qa, kw, vw, o_c, o_s, gl_h, tq=tq, k_valid_from=WINDOW, name="win_attn_p")
    o2d = o.transpose(0, 2, 1, 3).reshape(b * t, D_MODEL)
    h_new = fused_matmul(o2d, w_out, res=h2d, tm=tm, tn=512, name="att_out_p")
    win_rows = min(WINDOW, t)
    return (h_new, _kv_stack(qkv, b, t, D_MODEL), _kv_stack(qkv, b, t, D_MODEL + 2 * KV_DIM),
            _kv_stack(qkv, b, t, D_MODEL + 4 * KV_DIM)[:, t - win_rows:])


def nsa_sample_layer(h2d, b, t, cache_cmp, cache_slc, win_buf, page_table, norm_g, w_in, cmp_pos, cmp_w1, cmp_w2,
                     w_out, slopes):
    n_pages = page_table.shape[1]
    past_len = n_pages * PAGE_SIZE
    tq = 16
    assert t <= tq and t <= 2 * CMP_STRIDE - 1
    qkv = fused_matmul(h2d, w_in, gain=norm_g, tm=h2d.shape[0], tn=ATT_IN_PAD // 3, name="att_in_s")
    tpad = ((0, 0), (0, 0), (0, tq - t), (0, 0))
    q_h = jnp.pad(_heads(qkv[:, :D_MODEL], b, t, ATT_HEADS), tpad)
    gl_h = qkv[:, D_MODEL + 6 * KV_DIM:D_MODEL + 6 * KV_DIM + ATT_HEADS * N_BRANCH]
    gl_h = jnp.pad(gl_h.reshape(b, t, ATT_HEADS, N_BRANCH).transpose(0, 2, 1, 3), tpad)
    new_cmp = _kv_stack(qkv, b, t, D_MODEL)
    new_slc = _kv_stack(qkv, b, t, D_MODEL + 2 * KV_DIM)
    new_win = _kv_stack(qkv, b, t, D_MODEL + 4 * KV_DIM)

    cmp_fm = cache_cmp.transpose(0, 2, 3, 4, 1).reshape(-1, PAGE_SIZE)
    slc_fm = cache_slc.transpose(0, 2, 3, 4, 1)
    uf, us = cmp_project(cmp_fm, page_table, 0, cmp_w1, "cmp_u_s", feature_major=True)
    kvc = cmp_finish(uf, us, cmp_pos, cmp_w1, cmp_w2, "cmp_out_s")
    l_all = past_len + t
    n_sel = -(-l_all // SLC_BLOCK)
    qa = _query_feats(q_h, slopes)
    o_c, idx = cmp_attention(qa, kvc, tq=tq, n_sel=n_sel, q_base=past_len, want_idx=True, name="cmp_attn_s")

    new_page = jnp.pad(new_slc, ((0, 0), (0, PAGE_SIZE - t), (0, 0), (0, 0), (0, 0)))
    o_s = sel_attention_gather(q_h, idx[..., :t], page_table, slc_fm, new_page.transpose(0, 2, 3, 4, 1), slopes,
                               n_tok=t, past_len=past_len, name="sel_attn_s")

    w_rows = win_buf.shape[1]
    win_all = jnp.concatenate([win_buf.astype(F32), new_win], axis=1)
    lk = tq + WINDOW
    shift = WINDOW - w_rows
    kvw = jnp.pad(win_all, ((0, 0), (shift, lk - shift - w_rows - t), (0, 0), (0, 0), (0, 0))).astype(BF16)
    kw = kvw[:, :, 0].transpose(0, 2, 1, 3)
    vw = kvw[:, :, 1].transpose(0, 2, 1, 3)
    o = win_attention_merge(qa, kw, vw, o_c, o_s, gl_h, tq=tq, k_valid_from=shift, name="win_attn_s")
    o2d = o[:, :, :t].transpose(0, 2, 1, 3).reshape(b * t, D_MODEL)
    h_new = fused_matmul(o2d, w_out, res=h2d, tm=h2d.shape[0], tn=512, name="att_out_s")
    return h_new, new_cmp, new_slc, win_all[:, -w_rows:]


def ssm_layer(h2d, b, t, conv_state, ssm_state, norm_g, w_in, conv_prm, w_out, tag):
    t_pad = -(-t // SSD_CHUNK) * SSD_CHUNK
    tm = _row_tile(h2d.shape[0], 1024)
    zx = fused_matmul(h2d, w_in, gain=norm_g, tm=tm, tn=1280, name="ssm_in_" + tag)
    keep = CONV_WIDTH - 1
    tail = zx.reshape(b, t, SSM_IN_PAD)[:, max(t - keep, 0):, D_INNER:D_INNER + CONV_DIM]
    conv_out = tail if t >= keep else jnp.concatenate([conv_state.astype(F32), tail], axis=1)[:, -keep:]
    if t_pad != t:
        zx = jnp.pad(zx.reshape(b, t, SSM_IN_PAD), ((0, 0), (0, t_pad - t), (0, 0))).reshape(b * t_pad, SSM_IN_PAD)
    y, s_new = ssd_mixer(zx, conv_state, ssm_state, conv_prm, batch=b, t_pad=t_pad, t_valid=t, name="ssd_" + tag)
    if t_pad != t:
        y = y.reshape(b, t_pad, D_INNER)[:, :t].reshape(b * t, D_INNER)
    h_new = fused_matmul(y, w_out, res=h2d, tm=tm, tn=512, name="ssm_out_" + tag)
    return h_new, conv_out, s_new


def kernel(x_prompt, x_sample, state_ssm, state_conv, cache_cmp_kv, cache_slc_kv, state_win_kv, page_table, ssm_norm, ssm_w_in, ssm_conv_w, ssm_conv_b, ssm_dt_bias, ssm_a_log, ssm_d, ssm_gate_norm, ssm_w_out, att_norm, att_w_in, att_cmp_pos, att_cmp_w1, att_cmp_w2, att_w_out, mlp_norm, mlp_w_up, mlp_w_down, final_norm):
    bp, tp, _ = x_prompt.shape
    bs, ts, _ = x_sample.shape
    depth = mlp_norm.shape[0]
    slopes = alibi_slopes()
    hp = x_prompt.reshape(bp * tp, D_MODEL)
    hs = x_sample.reshape(bs * ts, D_MODEL)
    ssm_p, conv_p, ssm_s, conv_s = [], [], [], []
    cmp_p, slc_p, win_p, cmp_s, slc_s, win_s = [], [], [], [], [], []
    for i in range(depth):
        j = i // 2
        if i % 2 == 0:
            w_in = jnp.pad(ssm_w_in[j], ((0, 0), (0, SSM_IN_PAD - ssm_w_in.shape[-1]))).astype(BF16)
            w_out = ssm_w_out[j].astype(BF16)
            prm = (ssm_conv_w[j], ssm_conv_b[j], ssm_dt_bias[j], ssm_a_log[j], ssm_d[j], ssm_gate_norm[j])
            zc = jnp.zeros((bp, CONV_WIDTH - 1, CONV_DIM), F32)
            zs = jnp.zeros((bp, SSM_HEADS, SSM_HEAD_DIM, SSM_STATE), F32)
            hp, cp, sp = ssm_layer(hp, bp, tp, zc, zs, ssm_norm[j], w_in, prm, w_out, "p")
            hs, cs, ss = ssm_layer(hs, bs, ts, state_conv[j], state_ssm[j], ssm_norm[j], w_in, prm, w_out, "s")
            conv_p.append(cp); ssm_p.append(sp); conv_s.append(cs); ssm_s.append(ss)
        else:
            w_in = jnp.pad(att_w_in[j], ((0, 0), (0, ATT_IN_PAD - att_w_in.shape[-1]))).astype(BF16)
            w_out = att_w_out[j].astype(BF16)
            prm = (att_norm[j], w_in, att_cmp_pos[j], att_cmp_w1[j], att_cmp_w2[j], w_out, slopes)
            hp, kp, lp, wp = nsa_prompt_layer(hp, bp, tp, *prm)
            hs, kq, lq, wq = nsa_sample_layer(hs, bs, ts, cache_cmp_kv[j], cache_slc_kv[j], state_win_kv[j],
                                              page_table, *prm)
            cmp_p.append(kp); slc_p.append(lp); win_p.append(wp)
            cmp_s.append(kq); slc_s.append(lq); win_s.append(wq)
        fin = final_norm if i == depth - 1 else None
        w_up = mlp_w_up[i].astype(BF16)
        w_down = mlp_w_down[i].astype(BF16)
        hp = mlp_block(hp, mlp_norm[i], w_up, w_down, fin, "mlp%d_p" % i)
        hs = mlp_block(hs, mlp_norm[i], w_up, w_down, fin, "mlp%d_s" % i)
    return (hp.reshape(bp, tp, D_MODEL), hs.reshape(bs, ts, D_MODEL),
            jnp.stack(ssm_p), jnp.stack(conv_p), jnp.stack(cmp_p), jnp.stack(slc_p), jnp.stack(win_p),
            jnp.stack(ssm_s), jnp.stack(conv_s), jnp.stack(cmp_s), jnp.stack(slc_s), jnp.stack(win_s))
```
